```python
import jax, jax.numpy as jnp
from jax import lax
import numpy as np

D_MODEL = 1024
BATCH = 8
SEQ = 8192
DEPTH = 2

CTX_LEN = 256
GRID_W = 64
N_MIXERS = 4
GROUP_W = D_MODEL // N_MIXERS
HEADS = 4
HEAD_DIM = GROUP_W // HEADS
N_HEADS_TOTAL = N_MIXERS * HEADS
GLA_DK = HEAD_DIM // 2
GLA_RANK = 16
GLA_TAU = 16.0
D_FF = 4 * D_MODEL
CHUNK = 64
ROPE_BASE = 10000.0
RMS_EPS = 1e-6

IN_LAYOUT = (
    ('hg_q', GROUP_W), ('hg_f_fwd', GROUP_W), ('hg_f_bwd', GROUP_W), ('hg_i', GROUP_W), ('hg_g', GROUP_W),
    ('ml_q', GROUP_W), ('ml_k', GROUP_W), ('ml_v', GROUP_W), ('ml_if', 2 * 2 * HEADS), ('ml_o', GROUP_W),
    ('rt_q', GROUP_W), ('rt_k', GROUP_W), ('rt_v', GROUP_W), ('rt_g', GROUP_W),
    ('gl_q', HEADS * GLA_DK), ('gl_k', HEADS * GLA_DK), ('gl_v', GROUP_W),
    ('gl_a_fwd', GLA_RANK), ('gl_a_bwd', GLA_RANK), ('gl_g', GROUP_W),
)
IN_DIM = sum(s for _, s in IN_LAYOUT)

kernel_name = 'hybrid_bidir_recurrent_dit_block'


def rmsnorm(x):
    xf = x.astype(jnp.float32)
    return (xf * lax.rsqrt(jnp.mean(xf * xf, axis=-1, keepdims=True) + RMS_EPS)).astype(x.dtype)


def modulate(x, shift, scale):
    return rmsnorm(x) * (1.0 + scale) + shift


def split_proj(p):
    idx = np.cumsum([s for _, s in IN_LAYOUT])[:-1].tolist()
    parts = jnp.split(p, idx, axis=-1)
    return {name: t for (name, _), t in zip(IN_LAYOUT, parts)}


def to_heads(t):
    B, L, _ = t.shape
    return t.reshape(B, L, HEADS, -1).transpose(0, 2, 1, 3).astype(jnp.float32)


def from_heads(t):
    B, H, L, d = t.shape
    return t.transpose(0, 2, 1, 3).reshape(B, L, H * d)


def to_chunks(t):
    L = t.shape[2]
    t = t.reshape(t.shape[:2] + (L // CHUNK, CHUNK) + t.shape[3:])
    return jnp.moveaxis(t, 2, 0)


def from_chunks(t):
    t = jnp.moveaxis(t, 0, 2)
    return t.reshape(t.shape[:2] + (t.shape[2] * t.shape[3],) + t.shape[4:])


def chunk_linear(q, k, v, log_a, S0):
    causal = jnp.tril(jnp.ones((CHUNK, CHUNK), dtype=bool))

    def step(S, inp):
        qc, kc, vc, ac = inp
        b = jnp.cumsum(ac, axis=-2)
        if ac.shape[-1] == 1:
            diff = b[..., :, None, 0] - b[..., None, :, 0]
            decay = jnp.exp(jnp.where(causal, diff, -jnp.inf))
            attn = jnp.einsum('bhtk,bhsk->bhts', qc, kc) * decay
        else:
            diff = b[..., :, None, :] - b[..., None, :, :]
            decay = jnp.exp(jnp.where(causal[..., None], diff, -jnp.inf))
            attn = jnp.einsum('bhtk,bhsk,bhtsk->bhts', qc, kc, decay)
        b_last = b[..., -1:, :]
        o = (jnp.einsum('bhts,bhsv->bhtv', attn, vc)
             + jnp.einsum('bhtk,bhkv->bhtv', qc * jnp.exp(b), S))
        S_new = (jnp.exp(b_last[..., 0, :])[..., None] * S
                 + jnp.einsum('bhsk,bhsv->bhkv', kc * jnp.exp(b_last - b), vc))
        return S_new, o

    S, o = lax.scan(step, S0, (to_chunks(q), to_chunks(k), to_chunks(v), to_chunks(log_a)))
    return from_chunks(o), S


def chunk_mlstm(q, k, v, log_i, log_f, state):
    causal = jnp.tril(jnp.ones((CHUNK, CHUNK), dtype=bool))

    def step(carry, inp):
        Cm, n, m = carry
        qc, kc, vc, ic, fc = inp
        b = jnp.cumsum(fc, axis=-1)
        w = jnp.where(causal, b[..., :, None] - b[..., None, :] + ic[..., None, :], -jnp.inf)
        inter = b + m[..., None]
        m_t = jnp.maximum(jnp.max(w, axis=-1), inter)
        P = jnp.exp(w - m_t[..., None])
        g = jnp.exp(inter - m_t)
        s = jnp.einsum('bhtk,bhsk->bhts', qc, kc) * P
        num = (jnp.einsum('bhts,bhsv->bhtv', s, vc)
               + g[..., None] * jnp.einsum('bhtk,bhkv->bhtv', qc, Cm))
        den = jnp.sum(s, axis=-1) + g * jnp.einsum('bhtk,bhk->bht', qc, n)
        h = num / jnp.maximum(jnp.abs(den), jnp.exp(-m_t))[..., None]
        m_new = m_t[..., -1]
        a_state = jnp.exp(b[..., -1] + m - m_new)
        wk = jnp.exp(b[..., -1:] - b + ic - m_new[..., None])
        C_new = a_state[..., None, None] * Cm + jnp.einsum('bhsk,bhsv->bhkv', kc * wk[..., None], vc)
        n_new = a_state[..., None] * n + jnp.einsum('bhs,bhsk->bhk', wk, kc)
        return (C_new, n_new, m_new), h

    st, h = lax.scan(step, state, (to_chunks(q), to_chunks(k), to_chunks(v),
                                   to_chunks(log_i), to_chunks(log_f)))
    return from_chunks(h), st


def flip_seq(ts):
    return tuple(jnp.flip(t, axis=2) for t in ts)


def bidir(scan_fn, init, c_f, c_b, l_f, l_b):
    yc_f, s_f = scan_fn(*c_f, init)
    yc_b, s_b = scan_fn(*flip_seq(c_b), init)
    yl_f, _ = scan_fn(*l_f, s_f)
    yl_b, _ = scan_fn(*flip_seq(l_b), s_b)
    return yc_f + jnp.flip(yc_b, axis=2), yl_f + jnp.flip(yl_b, axis=2)


def hgrn2_inputs(p, lb_f, lb_b):
    q = jax.nn.silu(to_heads(p['hg_q']))
    v = to_heads(p['hg_i'])

    def direction(logits, lb):
        lb = lb.reshape(1, HEADS, 1, HEAD_DIM)
        z = to_heads(logits)
        log_f = jnp.logaddexp(jnp.log(lb), jnp.log1p(-lb) + jax.nn.log_sigmoid(z))
        k = (1.0 - lb) * jax.nn.sigmoid(-z)
        return (q, k, v, log_f)

    return direction(p['hg_f_fwd'], lb_f), direction(p['hg_f_bwd'], lb_b)


def hgrn2_mixer(pc, pl, lb):
    B = pl['hg_q'].shape[0]
    init = jnp.zeros((B, HEADS, HEAD_DIM, HEAD_DIM), jnp.float32)
    c_f, c_b = hgrn2_inputs(pc, lb[0], lb[1])
    l_f, l_b = hgrn2_inputs(pl, lb[0], lb[1])
    return bidir(chunk_linear, init, c_f, c_b, l_f, l_b)


def mlstm_inputs(p, gate_bias):
    q = to_heads(p['ml_q'])
    k = to_heads(p['ml_k']) * HEAD_DIM ** -0.5
    v = to_heads(p['ml_v'])
    B, L, _ = p['ml_if'].shape
    pre = p['ml_if'].astype(jnp.float32).reshape(B, L, 2, 2, HEADS) + gate_bias
    pre = jnp.transpose(pre, (2, 3, 0, 4, 1))
    fwd = (q, k, v, pre[0, 0], jax.nn.log_sigmoid(pre[0, 1]))
    bwd = (q, k, v, pre[1, 0], jax.nn.log_sigmoid(pre[1, 1]))
    return fwd, bwd


def mlstm_mixer(pc, pl, gate_bias):
    B = pl['ml_q'].shape[0]
    init = (jnp.zeros((B, HEADS, HEAD_DIM, HEAD_DIM), jnp.float32),
            jnp.zeros((B, HEADS, HEAD_DIM), jnp.float32),
            jnp.zeros((B, HEADS), jnp.float32))
    c_f, c_b = mlstm_inputs(pc, gate_bias.astype(jnp.float32))
    l_f, l_b = mlstm_inputs(pl, gate_bias.astype(jnp.float32))
    return bidir(chunk_mlstm, init, c_f, c_b, l_f, l_b)


def grid_rotary(rows):
    r = jnp.repeat(jnp.arange(rows), GRID_W).astype(jnp.float32)
    col = jnp.tile(jnp.arange(GRID_W), rows).astype(jnp.float32)
    n_freq = HEAD_DIM // 4
    inv = ROPE_BASE ** (-jnp.arange(n_freq, dtype=jnp.float32) / n_freq)
    ang_r = r[:, None] * inv[None, :]
    ang_c = col[:, None] * inv[None, :]
    return (jnp.cos(ang_r), jnp.sin(ang_r), jnp.cos(ang_c), jnp.sin(ang_c))


def rotate(x, cos, sin):
    x1, x2 = jnp.split(x, 2, axis=-1)
    return jnp.concatenate([x1 * cos - x2 * sin, x1 * sin + x2 * cos], axis=-1)


def rope2d(x, rot):
    cr, sr, cc, sc = rot
    xa, xb = jnp.split(x, 2, axis=-1)
    return jnp.concatenate([rotate(xa, cr, sr), rotate(xb, cc, sc)], axis=-1)


def retention_inputs(p, decay_logit, rot):
    q = to_heads(p['rt_q'])
    k = to_heads(p['rt_k']) * HEAD_DIM ** -0.5
    v = to_heads(p['rt_v'])
    if rot is not None:
        q = rope2d(q, rot)
        k = rope2d(k, rot)
    B, _, L, _ = q.shape
    log_g = jax.nn.log_sigmoid(decay_logit.astype(jnp.float32))
    la_f = jnp.broadcast_to(log_g[0][None, :, None, None], (B, HEADS, L, 1))
    la_b = jnp.broadcast_to(log_g[1][None, :, None, None], (B, HEADS, L, 1))
    return (q, k, v, la_f), (q, k, v, la_b)


def retention_mixer(pc, pl, decay_logit, rot):
    B = pl['rt_q'].shape[0]
    init = jnp.zeros((B, HEADS, HEAD_DIM, HEAD_DIM), jnp.float32)
    c_f, c_b = retention_inputs(pc, decay_logit, None)
    l_f, l_b = retention_inputs(pl, decay_logit, rot)
    return bidir(chunk_linear, init, c_f, c_b, l_f, l_b)


def gla_inputs(p, w_a, b_a):
    q = to_heads(p['gl_q'])
    k = to_heads(p['gl_k']) * GLA_DK ** -0.5
    v = to_heads(p['gl_v'])

    def log_alpha(z, d):
        za = z.astype(jnp.float32) @ w_a[d].astype(jnp.float32) + b_a[d].astype(jnp.float32)
        return to_heads(jax.nn.log_sigmoid(za) / GLA_TAU)

    return (q, k, v, log_alpha(p['gl_a_fwd'], 0)), (q, k, v, log_alpha(p['gl_a_bwd'], 1))


def gla_mixer(pc, pl, w_a, b_a):
    B = pl['gl_q'].shape[0]
    init = jnp.zeros((B, HEADS, GLA_DK, HEAD_DIM), jnp.float32)
    c_f, c_b = gla_inputs(pc, w_a, b_a)
    l_f, l_b = gla_inputs(pl, w_a, b_a)
    return bidir(chunk_linear, init, c_f, c_b, l_f, l_b)


def out_gates(p):
    return jnp.concatenate([jax.nn.sigmoid(p['hg_g']), jax.nn.sigmoid(p['ml_o']),
                            jax.nn.silu(p['rt_g']), jax.nn.silu(p['gl_g'])], axis=-1)


def head_norm(y, g):
    B, L, _ = y.shape
    yh = y.astype(jnp.float32).reshape(B, L, N_HEADS_TOTAL, -1)
    yh = yh * lax.rsqrt(jnp.mean(yh * yh, axis=-1, keepdims=True) + RMS_EPS)
    return yh.reshape(B, L, -1) * g


def mixer_out(raw, p, g, w_o):
    return (head_norm(raw, g) * out_gates(p)).astype(p['hg_g'].dtype) @ w_o


def sq_relu_mlp(h, w1, w2):
    return jnp.square(jax.nn.relu(h @ w1)) @ w2


def setup_inputs(seed: int = 0) -> dict:
    key = jax.random.key(seed)
    ks = jax.random.split(key, 20)
    f32 = jnp.float32
    x = jax.random.normal(ks[0], (BATCH, SEQ, D_MODEL), f32)
    c = jax.random.normal(ks[1], (BATCH, D_MODEL), f32)
    ctx = jax.random.normal(ks[2], (BATCH, CTX_LEN, D_MODEL), f32)
    c_ctx = jax.random.normal(ks[3], (D_MODEL,), f32)
    w_ada = jax.random.normal(ks[4], (DEPTH, D_MODEL, 6 * D_MODEL), f32) * (0.5 * D_MODEL ** -0.5)
    b_ada = 0.01 * jax.random.normal(ks[5], (DEPTH, 6 * D_MODEL), f32)
    w_in = jax.random.normal(ks[6], (DEPTH, D_MODEL, IN_DIM), f32) * D_MODEL ** -0.5
    g_heads = 1.0 + 0.02 * jax.random.normal(ks[7], (DEPTH, D_MODEL), f32)
    hgrn_lb_logits = 0.5 * jax.random.normal(ks[8], (DEPTH, 2, GROUP_W), f32)
    ig_bias = 0.1 * jax.random.normal(ks[9], (DEPTH, 2, 1, HEADS), f32)
    fg_bias = jnp.linspace(3.0, 6.0, HEADS, dtype=f32)[None, None, None, :] + 0.1 * jax.random.normal(ks[10], (DEPTH, 2, 1, HEADS), f32)
    ml_gate_bias = jnp.concatenate([ig_bias, fg_bias], axis=2)
    rt_base = jnp.log(2.0 ** (5.0 + jnp.arange(HEADS, dtype=f32)) - 1.0)
    rt_decay_logit = rt_base[None, None, :] + 0.1 * jax.random.normal(ks[11], (DEPTH, 2, HEADS), f32)
    gla_w_a = jax.random.normal(ks[12], (DEPTH, 2, GLA_RANK, HEADS * GLA_DK), f32) * GLA_RANK ** -0.5
    gla_b_a = 0.1 * jax.random.normal(ks[13], (DEPTH, 2, HEADS * GLA_DK), f32)
    w_out = jax.random.normal(ks[14], (DEPTH, D_MODEL, D_MODEL), f32) * D_MODEL ** -0.5
    w_ff1 = jax.random.normal(ks[15], (DEPTH, D_MODEL, D_FF), f32) * D_MODEL ** -0.5
    w_ff2 = jax.random.normal(ks[16], (DEPTH, D_FF, D_MODEL), f32) * D_FF ** -0.5
    g_final = 1.0 + 0.02 * jax.random.normal(ks[17], (D_MODEL,), f32)
    return {'x': x, 'c': c, 'ctx': ctx, 'c_ctx': c_ctx, 'w_ada': w_ada, 'b_ada': b_ada,
            'w_in': w_in, 'g_heads': g_heads, 'hgrn_lb_logits': hgrn_lb_logits,
            'ml_gate_bias': ml_gate_bias, 'rt_decay_logit': rt_decay_logit,
            'gla_w_a': gla_w_a, 'gla_b_a': gla_b_a, 'w_out': w_out,
            'w_ff1': w_ff1, 'w_ff2': w_ff2, 'g_final': g_final}


def reference(x, c, ctx, c_ctx, w_ada, b_ada, w_in, g_heads, hgrn_lb_logits, ml_gate_bias,
              rt_decay_logit, gla_w_a, gla_b_a, w_out, w_ff1, w_ff2, g_final):
    ROWS = x.shape[1] // GRID_W
    rot = grid_rotary(ROWS)
    sm = jax.nn.softmax(hgrn_lb_logits.astype(jnp.float32), axis=0)
    lb_all = jnp.maximum(jnp.cumsum(sm, axis=0) - sm[:1], 0.0)
    xl, xc = x, ctx
    for layer in range(DEPTH):
        last = layer == DEPTH - 1
        mod_l = (jax.nn.silu(c) @ w_ada[layer] + b_ada[layer])[:, None, :]
        mod_c = (jax.nn.silu(c_ctx) @ w_ada[layer] + b_ada[layer])[None, None, :]
        sh1_l, sc1_l, g1_l, sh2_l, sc2_l, g2_l = jnp.split(mod_l, 6, axis=-1)
        sh1_c, sc1_c, g1_c, sh2_c, sc2_c, g2_c = jnp.split(mod_c, 6, axis=-1)
        pl = split_proj(modulate(xl, sh1_l, sc1_l) @ w_in[layer])
        pc = split_proj(modulate(xc, sh1_c, sc1_c) @ w_in[layer])
        mixed = (hgrn2_mixer(pc, pl, lb_all[layer]),
                 mlstm_mixer(pc, pl, ml_gate_bias[layer]),
                 retention_mixer(pc, pl, rt_decay_logit[layer], rot),
                 gla_mixer(pc, pl, gla_w_a[layer], gla_b_a[layer]))
        raw_l = jnp.concatenate([from_heads(m[1]) for m in mixed], axis=-1)
        xl = xl + g1_l * mixer_out(raw_l, pl, g_heads[layer], w_out[layer])
        if not last:
            raw_c = jnp.concatenate([from_heads(m[0]) for m in mixed], axis=-1)
            xc = xc + g1_c * mixer_out(raw_c, pc, g_heads[layer], w_out[layer])
        xl = xl + g2_l * sq_relu_mlp(modulate(xl, sh2_l, sc2_l), w_ff1[layer], w_ff2[layer])
        if not last:
            xc = xc + g2_c * sq_relu_mlp(modulate(xc, sh2_c, sc2_c), w_ff1[layer], w_ff2[layer])
    return rmsnorm(xl) * g_final
```

```python
import functools

import numpy as np
import jax
import jax.numpy as jnp
from jax import lax
from jax.experimental import pallas as pl
from jax.experimental.pallas import tpu as pltpu

F32 = jnp.float32
BF16 = jnp.bfloat16

D_MODEL = 1024
GROUP_W = 256
HEADS = 4
HEAD_DIM = 64
GLA_DK = 32
GLA_W = HEADS * GLA_DK
GLA_RANK = 16
GLA_TAU = 16.0
D_FF = 4 * D_MODEL
GRID_W = 64
ROPE_BASE = 10000.0
RMS_EPS = 1e-6

CHUNK = 64
SUB = 8
ROW_TILE = 256
LEVELS = (32, 16, 8)
N_DECAY_BLOCKS = 2 + 2 * len(LEVELS)

P_MIX = 3200
P_SMALL = 3072
IN_PAD = P_MIX + D_MODEL

VMEM_LIMIT = 56 * 1024 * 1024

NEG_INF = float("-inf")


def _dot(a, b):
    return jnp.dot(a, b, preferred_element_type=F32)


def _dot_nt(a, b):
    return lax.dot_general(a, b, (((1,), (1,)), ((), ())), preferred_element_type=F32)


def _dot_tn(a, b):
    return lax.dot_general(a, b, (((0,), (0,)), ((), ())), preferred_element_type=F32)


def _split3(a):
    hi = a.astype(BF16)
    r = a - hi.astype(F32)
    mid = r.astype(BF16)
    lo = (r - mid.astype(F32)).astype(BF16)
    return hi, mid, lo


def _sel_left(m, a):
    hi, mid, lo = _split3(a)
    return _dot(m, hi) + _dot(m, mid) + _dot(m, lo)


def _sel_right(a, m):
    hi, mid, lo = _split3(a)
    return _dot(hi, m) + _dot(mid, m) + _dot(lo, m)


def _log_sigmoid(z):
    return jnp.minimum(z, 0.0) - jnp.log1p(jnp.exp(-jnp.abs(z)))


def _sigmoid(z):
    return 1.0 / (1.0 + jnp.exp(-z))


def _silu(z):
    return z * _sigmoid(z)


def _rms_scale(x):
    return lax.rsqrt(jnp.mean(x * x, axis=-1, keepdims=True) + RMS_EPS)


def _np_consts():
    C = CHUNK
    t = np.arange(C)
    T, U = np.meshgrid(t, t, indexing="ij")
    incl = [U <= T, U >= T]

    dec = np.zeros((2, N_DECAY_BLOCKS * C, C), np.float32)
    lev = np.zeros((2, len(LEVELS), C, 4 * C), np.float32)
    for d in range(2):
        blocks = [incl[d], (U > T) if d == 0 else (U < T)]
        for li, h in enumerate(LEVELS):
            m = (T // (2 * h)) * (2 * h) + h
            if d == 0:
                a = (T >= m) & (U >= m) & (U <= T)
                b = (T < m) & (U > T) & (U <= m - 1)
                pair = ((T // (2 * h)) == (U // (2 * h))) & (T % (2 * h) >= h) & (U % (2 * h) < h)
            else:
                a = (T < m) & (U >= T) & (U <= m - 1)
                b = (T >= m) & (U >= m) & (U < T)
                pair = ((T // (2 * h)) == (U // (2 * h))) & (T % (2 * h) < h) & (U % (2 * h) >= h)
            blocks += [a, b]
            lev[d, li] = np.tile(pair.astype(np.float32), (1, 4))
        dec[d] = np.concatenate([b_.astype(np.float32) for b_ in blocks], axis=0)

    incl_neg = np.stack([np.tile(np.where(incl[d], 0.0, NEG_INF).astype(np.float32), (1, 4))
                         for d in range(2)])
    tri_t = np.stack([np.tile(incl[d].T.astype(np.float32), (1, 4)) for d in range(2)])
    eye = np.tile(np.eye(C, dtype=np.float32), (1, 4))
    dist = np.tile(np.abs(T - U).astype(np.float32), (1, 4))

    j = np.arange(SUB)
    diag_neg = np.zeros((2, SUB, SUB, GROUP_W), np.float32)
    for d in range(2):
        ok = (j[None, :] >= j[:, None]) if d == 0 else (j[None, :] <= j[:, None])
        diag_neg[d] = np.where(ok, 0.0, NEG_INF)[:, :, None]

    r256 = np.arange(4 * C)
    head_of_row = r256 // C
    hm256 = (head_of_row[:, None] == (np.arange(GROUP_W) // HEAD_DIM)[None, :]).astype(np.float32)
    hm128 = (head_of_row[:, None] == (np.arange(GLA_W) // GLA_DK)[None, :]).astype(np.float32)
    bd256 = ((np.arange(GROUP_W) // HEAD_DIM)[:, None]
             == (np.arange(GROUP_W) // HEAD_DIM)[None, :]).astype(np.float32)
    bd_gl = ((np.arange(GROUP_W) // HEAD_DIM)[:, None]
             == (np.arange(GLA_W) // GLA_DK)[None, :]).astype(np.float32)

    sel_i = np.zeros((2, 128, GROUP_W), np.float32)
    sel_f = np.zeros((2, 128, GROUP_W), np.float32)
    for d in range(2):
        for h in range(HEADS):
            sel_i[d, d * 8 + h, h * HEAD_DIM:(h + 1) * HEAD_DIM] = 1.0
            sel_f[d, d * 8 + 4 + h, h * HEAD_DIM:(h + 1) * HEAD_DIM] = 1.0
    return dict(dec=dec, lev=lev, incl_neg=incl_neg, tri_t=tri_t, eye=eye, dist=dist,
                diag_neg=diag_neg, hm256=hm256, hm128=hm128, bd256=bd256, bd_gl=bd_gl,
                sel_i=sel_i, sel_f=sel_f)


_NP = _np_consts()


def _full_spec(shape):
    n = len(shape)
    return pl.BlockSpec(tuple(shape), lambda *_: (0,) * n)


def _ada_kernel(c_ref, w_ref, b_ref, o_ref):
    cv = c_ref[...]
    act = _silu(cv).astype(BF16)
    o_ref[...] = _dot(act, w_ref[...].astype(BF16)) + b_ref[...]


def _ada(cvecs, w, b):
    rows = cvecs.shape[0]
    n = w.shape[1]
    tn = 1024
    return pl.pallas_call(
        _ada_kernel,
        grid=(n // tn,),
        in_specs=[pl.BlockSpec((rows, D_MODEL), lambda i: (0, 0)),
                  pl.BlockSpec((D_MODEL, tn), lambda i: (0, i)),
                  pl.BlockSpec((1, tn), lambda i: (0, i))],
        out_specs=pl.BlockSpec((rows, tn), lambda i: (0, i)),
        out_shape=jax.ShapeDtypeStruct((rows, n), F32),
        compiler_params=pltpu.CompilerParams(dimension_semantics=("arbitrary",),
                                             vmem_limit_bytes=VMEM_LIMIT),
    )(cvecs, w, b.reshape(1, n))


def _rope(x, cos, sin):
    lane = lax.broadcasted_iota(jnp.int32, x.shape, 1)
    low = (lane % 32) < 16
    partner = jnp.where(low, pltpu.roll(x, GROUP_W - 16, 1), pltpu.roll(x, 16, 1))
    return x * cos + partner * sin


def _inp_kernel(x_ref, mod_ref, w_ref, cos_ref, sin_ref, pm_ref, gate_ref):
    x = x_ref[0]
    sh = mod_ref[0, 0, :, 0:D_MODEL]
    sc = mod_ref[0, 0, :, D_MODEL:2 * D_MODEL]
    xn = ((x * _rms_scale(x)) * (1.0 + sc) + sh).astype(BF16)

    def proj(a, b):
        return _dot(xn, w_ref[:, a:b])

    G = GROUP_W
    pm_ref[0, :, 0:G] = _silu(proj(0, G))
    pm_ref[0, :, G:4 * G] = proj(G, 4 * G)
    pm_ref[0, :, 4 * G:5 * G] = proj(4 * G, 5 * G)
    pm_ref[0, :, 5 * G:6 * G] = proj(5 * G, 6 * G) * (HEAD_DIM ** -0.5)
    pm_ref[0, :, 6 * G:7 * G] = proj(6 * G, 7 * G)
    cos = cos_ref[...]
    sin = sin_ref[...]
    pm_ref[0, :, 7 * G:8 * G] = _rope(proj(7 * G, 8 * G), cos, sin)
    pm_ref[0, :, 8 * G:9 * G] = _rope(proj(8 * G, 9 * G) * (HEAD_DIM ** -0.5), cos, sin)
    pm_ref[0, :, 9 * G:10 * G] = proj(9 * G, 10 * G)
    o = 10 * G
    pm_ref[0, :, o:o + GLA_W] = proj(o, o + GLA_W)
    pm_ref[0, :, o + GLA_W:o + 2 * GLA_W] = proj(o + GLA_W, o + 2 * GLA_W) * (GLA_DK ** -0.5)
    pm_ref[0, :, o + 2 * GLA_W:P_MIX] = proj(o + 2 * GLA_W, P_MIX)
    g = proj(P_MIX, IN_PAD)
    gate_ref[0, :, 0:2 * G] = _sigmoid(g[:, 0:2 * G])
    gate_ref[0, :, 2 * G:4 * G] = _silu(g[:, 2 * G:4 * G])


def _inp(xall, mods, w_in_p, cos_t, sin_t, n_ctx_tiles):
    B, NT, _ = xall.shape
    T = ROW_TILE
    return pl.pallas_call(
        _inp_kernel,
        grid=(B, NT // T),
        in_specs=[pl.BlockSpec((1, T, D_MODEL), lambda b, i: (b, i, 0)),
                  pl.BlockSpec((1, 1, 1, 6 * D_MODEL),
                               lambda b, i: (b, jnp.where(i < n_ctx_tiles, 0, 1), 0, 0)),
                  pl.BlockSpec((D_MODEL, IN_PAD), lambda b, i: (0, 0)),
                  pl.BlockSpec((T, GROUP_W), lambda b, i: (i, 0)),
                  pl.BlockSpec((T, GROUP_W), lambda b, i: (i, 0))],
        out_specs=[pl.BlockSpec((1, T, P_MIX), lambda b, i: (b, i, 0)),
                   pl.BlockSpec((1, T, D_MODEL), lambda b, i: (b, i, 0))],
        out_shape=[jax.ShapeDtypeStruct((B, NT, P_MIX), F32),
                   jax.ShapeDtypeStruct((B, NT, D_MODEL), F32)],
        compiler_params=pltpu.CompilerParams(dimension_semantics=("parallel", "parallel"),
                                             vmem_limit_bytes=VMEM_LIMIT),
    )(xall, mods, w_in_p, cos_t, sin_t)


def _tile4(a):
    return jnp.concatenate([a, a, a, a], axis=0)


def _last_row(a, d):
    return a[CHUNK - 1:CHUNK, :] if d == 0 else a[0:1, :]


def _vector_decay_group(d, q, k, v, log_a, st_ref, cst, e_mat, hm_k, bd_state, buf):
    C = CHUNK
    W = q.shape[1]
    b_buf, k_buf, v_buf = buf
    ex = _sel_left(cst["dec"][d], log_a)
    b = ex[0:C]
    q_state = (q * jnp.exp(b)).astype(BF16)
    k_end = (k * jnp.exp(ex[C:2 * C])).astype(BF16)
    v16 = v.astype(BF16)
    hm_v = cst["hm256"]
    v_bd = _tile4(v16) * hm_v

    attn = None
    for li in range(len(LEVELS)):
        qa = (q * jnp.exp(ex[(2 + 2 * li) * C:(3 + 2 * li) * C])).astype(BF16)
        kb = (k * jnp.exp(ex[(3 + 2 * li) * C:(4 + 2 * li) * C])).astype(BF16)
        part = _dot_nt(qa, _tile4(kb) * hm_k) * cst["lev"][d, li]
        attn = part if attn is None else attn + part

    st = st_ref[d]
    o = _dot(attn.astype(BF16), v_bd) + _dot_nt(q_state, st.astype(BF16))

    b_buf[:, 0:W] = b
    k_buf[:, 0:W] = k
    v_buf[...] = v
    terms = []
    for g in range(C // SUB):
        r0 = g * SUB
        b_sub = b[r0:r0 + SUB]
        q_sub = q[r0:r0 + SUB]
        for jj in range(SUB):
            s = r0 + jj
            bj = b_buf[s:s + 1, 0:W]
            kj = k_buf[s:s + 1, 0:W]
            e = jnp.exp(b_sub - bj + cst["diag_neg"][d, jj][:, 0:W])
            terms.append(q_sub * kj * e)
    summed = _dot(jnp.concatenate(terms, axis=0).astype(BF16), e_mat)
    rows = []
    for g in range(C // SUB):
        r0 = g * SUB
        acc = None
        for jj in range(SUB):
            s = r0 + jj
            piece = summed[(g * SUB + jj) * SUB:(g * SUB + jj + 1) * SUB] * v_buf[s:s + 1, :]
            acc = piece if acc is None else acc + piece
        rows.append(acc)
    o = o + jnp.concatenate(rows, axis=0)

    st_ref[d] = jnp.exp(_last_row(b, d)) * st + bd_state * _dot_tn(v16, k_end)
    return o


def _mix_kernel(pf_ref, pb_ref, hg_ref, mlb_ref, rtd_ref, wa_hi_ref, wa_lo_ref, ba_ref,
                dec_ref, lev_ref, incl_ref, trit_ref, eye_ref, dist_ref, diag_ref,
                hm256_ref, hm128_ref, bd256_ref, bdgl_ref, egl_ref, seli_ref, self_ref,
                yf_ref, yb_ref,
                s_hg, s_rt, s_gl, s_ml, m_ml, b_buf, k_buf, v_buf):
    C = CHUNK
    G = GROUP_W

    @pl.when(pl.program_id(1) == 0)
    def _():
        s_hg[...] = jnp.zeros_like(s_hg)
        s_rt[...] = jnp.zeros_like(s_rt)
        s_gl[...] = jnp.zeros_like(s_gl)
        s_ml[...] = jnp.zeros_like(s_ml)
        m_ml[...] = jnp.zeros_like(m_ml)

    cst = dict(dec=dec_ref, lev=lev_ref, diag_neg=diag_ref, hm256=hm256_ref[...])
    hm256 = cst["hm256"]
    hm128 = hm128_ref[...]
    bd256 = bd256_ref[...]
    bd256_16 = bd256.astype(BF16)
    bdgl = bdgl_ref[...]
    ones_cc = jnp.ones((SUB, C), BF16)
    lane_head = lax.broadcasted_iota(jnp.int32, (C, G), 1) // HEAD_DIM
    buf = (b_buf, k_buf, v_buf)

    for d, (p_ref, y_ref) in enumerate(((pf_ref, yf_ref), (pb_ref, yb_ref))):
        incl_neg = incl_ref[d]

        q = p_ref[0, :, 0:G]
        z = p_ref[0, :, (1 + d) * G:(2 + d) * G]
        v = p_ref[0, :, 3 * G:4 * G]
        log_lb = hg_ref[3 * d + 0:3 * d + 1, :]
        log_1mlb = hg_ref[3 * d + 1:3 * d + 2, :]
        one_m_lb = hg_ref[3 * d + 2:3 * d + 3, :]
        other = log_1mlb + _log_sigmoid(z)
        log_f = jnp.maximum(log_lb, other) + jnp.log1p(jnp.exp(-jnp.abs(log_lb - other)))
        k = one_m_lb * _sigmoid(-z)
        y_ref[0, :, 0:G] = _vector_decay_group(d, q, k, v, log_f, s_hg, cst, bd256_16,
                                               hm256, bd256, buf)

        q16 = p_ref[0, :, 4 * G:5 * G].astype(BF16)
        k = p_ref[0, :, 5 * G:6 * G]
        v16 = p_ref[0, :, 6 * G:7 * G].astype(BF16)
        pre = p_ref[0, :, P_SMALL:P_MIX] + mlb_ref[...]
        i_bc = _sel_right(pre, seli_ref[d])
        f_bc = _sel_right(_log_sigmoid(pre), self_ref[d])
        tri = dec_ref[d, 0:C, :]
        b_col = _sel_left(tri, f_bc)
        b_row = _sel_left(ones_cc, f_bc * trit_ref[d])[0:1]
        i_row = _sel_left(ones_cc, i_bc * eye_ref[...])[0:1]
        w = b_col - b_row + i_row + incl_neg
        m_prev = m_ml[d, 0:1, :]
        inter = b_col + m_prev
        m_t = None
        for h in range(HEADS):
            sl = slice(h * HEAD_DIM, (h + 1) * HEAD_DIM)
            mh = jnp.maximum(jnp.max(w[:, sl], axis=-1, keepdims=True), inter[:, sl][:, 0:1])
            mh = jnp.broadcast_to(mh, (C, G))
            m_t = mh if m_t is None else jnp.where(lane_head == h, mh, m_t)
        p_mat = jnp.exp(w - m_t)
        g_in = jnp.exp(inter - m_t)
        k16 = k.astype(BF16)
        s = (_dot_nt(q16, _tile4(k16) * hm256) * p_mat).astype(BF16)
        v_bd = _tile4(v16) * hm256
        c_st = s_ml[d]
        cst16 = c_st.astype(BF16)
        num = _dot(s, v_bd) + g_in * _dot(q16, cst16[:, 0:G])
        den = _dot(s, bd256_16) + g_in * _dot(q16, cst16[:, G:2 * G])
        y_ref[0, :, G:2 * G] = num / jnp.maximum(jnp.abs(den), jnp.exp(-m_t))
        m_new = _last_row(m_t, d)
        b_end = _last_row(b_col, d)
        a_state = jnp.exp(b_end + m_prev - m_new)
        kw = (k * jnp.exp(b_end - b_col + i_bc - m_new)).astype(BF16)
        upd_c = _dot_tn(kw, v16)
        upd_n = _dot_tn(kw, jnp.ones((C, G), BF16))
        s_ml[d, :, 0:G] = a_state * c_st[:, 0:G] + bd256 * upd_c
        s_ml[d, :, G:2 * G] = a_state * c_st[:, G:2 * G] + bd256 * upd_n
        m_ml[d] = jnp.broadcast_to(m_new, (SUB, G))

        q = p_ref[0, :, 7 * G:8 * G]
        k = p_ref[0, :, 8 * G:9 * G]
        v16 = p_ref[0, :, 9 * G:10 * G].astype(BF16)
        log_g = _log_sigmoid(rtd_ref[d:d + 1, :])
        decay = jnp.exp(log_g * dist_ref[...] + incl_neg)
        row = lax.broadcasted_iota(jnp.int32, (C, 1), 0).astype(F32)
        cnt = (row + 1.0) if d == 0 else (C - row)
        q_state = (q * jnp.exp(log_g * cnt)).astype(BF16)
        k_end = (k * jnp.exp(log_g * (C - cnt))).astype(BF16)
        attn = (_dot_nt(q.astype(BF16), _tile4(k.astype(BF16)) * hm256) * decay).astype(BF16)
        st = s_rt[d]
        y_ref[0, :, 2 * G:3 * G] = _dot(attn, _tile4(v16) * hm256) + _dot(q_state, st.astype(BF16))
        s_rt[d] = jnp.exp(log_g * float(C)) * st + bd256 * _dot_tn(k_end, v16)

        o = 10 * G
        q = p_ref[0, :, o:o + GLA_W]
        k = p_ref[0, :, o + GLA_W:o + 2 * GLA_W]
        v = p_ref[0, :, o + 2 * GLA_W:o + 2 * GLA_W + G]
        small = p_ref[0, :, P_SMALL:P_MIX]
        za = _sel_right(small, wa_hi_ref[d]) + _dot(small.astype(BF16), wa_lo_ref[d]) + ba_ref[d:d + 1, :]
        log_a = _log_sigmoid(za) * (1.0 / GLA_TAU)
        y_ref[0, :, 3 * G:4 * G] = _vector_decay_group(d, q, k, v, log_a, s_gl, cst, egl_ref[...],
                                                       hm128, bdgl, buf)


def _mix(p_mix, hg_par, ml_bias, rt_dec, wa_hi, wa_lo, ba, n_ctx_chunks):
    B, NT, _ = p_mix.shape
    C = CHUNK
    NC = NT // C

    def fwd_map(b, j):
        return (b, j, 0)

    def bwd_map(b, j):
        return (b, jnp.where(j < n_ctx_chunks, n_ctx_chunks - 1 - j, NC - 1 - (j - n_ctx_chunks)), 0)

    consts = [jnp.asarray(_NP["dec"], BF16), jnp.asarray(_NP["lev"]), jnp.asarray(_NP["incl_neg"]),
              jnp.asarray(_NP["tri_t"]), jnp.asarray(_NP["eye"]), jnp.asarray(_NP["dist"]),
              jnp.asarray(_NP["diag_neg"]), jnp.asarray(_NP["hm256"], BF16), jnp.asarray(_NP["hm128"], BF16),
              jnp.asarray(_NP["bd256"]), jnp.asarray(_NP["bd_gl"]), jnp.asarray(_NP["bd_gl"].T, BF16),
              jnp.asarray(_NP["sel_i"], BF16), jnp.asarray(_NP["sel_f"], BF16)]
    params = [hg_par, ml_bias, rt_dec, wa_hi, wa_lo, ba]
    G = GROUP_W
    return pl.pallas_call(
        _mix_kernel,
        grid=(B, NC),
        in_specs=[pl.BlockSpec((1, C, P_MIX), fwd_map), pl.BlockSpec((1, C, P_MIX), bwd_map)]
        + [_full_spec(a.shape) for a in params + consts],
        out_specs=[pl.BlockSpec((1, C, D_MODEL), fwd_map), pl.BlockSpec((1, C, D_MODEL), bwd_map)],
        out_shape=[jax.ShapeDtypeStruct((B, NT, D_MODEL), F32)] * 2,
        scratch_shapes=[pltpu.VMEM((2, G, G), F32), pltpu.VMEM((2, G, G), F32),
                        pltpu.VMEM((2, G, GLA_W), F32), pltpu.VMEM((2, G, 2 * G), F32),
                        pltpu.VMEM((2, SUB, G), F32),
                        pltpu.VMEM((C, G), F32), pltpu.VMEM((C, G), F32), pltpu.VMEM((C, G), F32)],
        compiler_params=pltpu.CompilerParams(dimension_semantics=("parallel", "arbitrary"),
                                             vmem_limit_bytes=VMEM_LIMIT),
    )(p_mix, p_mix, *params, *consts)


def _out_kernel(final, x_ref, yf_ref, yb_ref, gate_ref, mod_ref, gh_ref, bd_ref,
                wo_ref, w1_ref, w2_ref, gf_ref, o_ref):
    D = D_MODEL
    x = x_ref[0]
    bd = bd_ref[...]
    for g in range(D // GROUP_W):
        sl = slice(g * GROUP_W, (g + 1) * GROUP_W)
        y = yf_ref[0, :, sl] + yb_ref[0, :, sl]
        sq = y * y
        hi = sq.astype(BF16)
        lo = (sq - hi.astype(F32)).astype(BF16)
        msq = (_dot(hi, bd) + _dot(lo, bd)) * (1.0 / HEAD_DIM)
        yn = y * lax.rsqrt(msq + RMS_EPS) * gh_ref[:, sl] * gate_ref[0, :, sl]
        part = _dot(yn.astype(BF16), wo_ref[sl, :])
        mixed = part if g == 0 else mixed + part
    g1 = mod_ref[0, 0, :, 2 * D:3 * D]
    sh2 = mod_ref[0, 0, :, 3 * D:4 * D]
    sc2 = mod_ref[0, 0, :, 4 * D:5 * D]
    g2 = mod_ref[0, 0, :, 5 * D:6 * D]
    x1 = x + g1 * mixed
    hin = ((x1 * _rms_scale(x1)) * (1.0 + sc2) + sh2).astype(BF16)
    hid = jnp.maximum(_dot(hin, w1_ref[...]), 0.0)
    ff = _dot((hid * hid).astype(BF16), w2_ref[...])
    x2 = x1 + g2 * ff
    if final:
        x2 = (x2 * _rms_scale(x2)) * gf_ref[...]
    o_ref[0] = x2


def _out(xall, yf, yb, gate, mods, g_heads, w_out, w_ff1, w_ff2, g_final, n_ctx_tiles, final):
    B, NT, D = xall.shape
    T = ROW_TILE
    t0 = n_ctx_tiles if final else 0
    n_tiles = NT // T - t0
    tok = lambda b, i: (b, i + t0, 0)
    bd = jnp.asarray(_NP["bd256"], BF16)
    wspec = lambda shape: pl.BlockSpec(shape, lambda b, i: (0, 0), pipeline_mode=pl.Buffered(1))
    return pl.pallas_call(
        functools.partial(_out_kernel, final),
        grid=(B, n_tiles),
        in_specs=[pl.BlockSpec((1, T, D), tok), pl.BlockSpec((1, T, D), tok),
                  pl.BlockSpec((1, T, D), tok), pl.BlockSpec((1, T, D), tok),
                  pl.BlockSpec((1, 1, 1, 6 * D),
                               lambda b, i: (b, jnp.where(i + t0 < n_ctx_tiles, 0, 1), 0, 0)),
                  pl.BlockSpec((1, D), lambda b, i: (0, 0)),
                  pl.BlockSpec((GROUP_W, GROUP_W), lambda b, i: (0, 0)),
                  wspec((D, D)), wspec((D, D_FF)), wspec((D_FF, D)),
                  pl.BlockSpec((1, D), lambda b, i: (0, 0))],
        out_specs=pl.BlockSpec((1, T, D), lambda b, i: (b, i, 0)),
        out_shape=jax.ShapeDtypeStruct((B, n_tiles * T, D), F32),
        compiler_params=pltpu.CompilerParams(dimension_semantics=("parallel", "parallel"),
                                             vmem_limit_bytes=VMEM_LIMIT),
    )(xall, yf, yb, gate, mods, g_heads.reshape(1, D), bd, w_out, w_ff1, w_ff2,
      g_final.reshape(1, D))


_REF_LAYOUT = (('hg_q', 256), ('hg_f_fwd', 256), ('hg_f_bwd', 256), ('hg_i', 256), ('hg_g', 256),
               ('ml_q', 256), ('ml_k', 256), ('ml_v', 256), ('ml_if', 16), ('ml_o', 256),
               ('rt_q', 256), ('rt_k', 256), ('rt_v', 256), ('rt_g', 256),
               ('gl_q', 128), ('gl_k', 128), ('gl_v', 256), ('gl_a_fwd', 16), ('gl_a_bwd', 16),
               ('gl_g', 256))
_NEW_ORDER = ('hg_q', 'hg_f_fwd', 'hg_f_bwd', 'hg_i', 'ml_q', 'ml_k', 'ml_v', 'rt_q', 'rt_k', 'rt_v',
              'gl_q', 'gl_k', 'gl_v', 'ml_if', 'gl_a_fwd', 'gl_a_bwd', 'PAD', 'hg_g', 'ml_o', 'rt_g', 'gl_g')


def _column_index():
    starts, off = {}, 0
    for name, w in _REF_LAYOUT:
        starts[name] = (off, w)
        off += w
    idx = []
    for name in _NEW_ORDER:
        if name == 'PAD':
            idx += [-1] * (128 - 48)
        else:
            a, w = starts[name]
            idx += list(range(a, a + w))
    idx = np.asarray(idx)
    assert idx.shape[0] == IN_PAD
    return np.where(idx < 0, 0, idx), (idx >= 0)


_COL_IDX, _COL_VALID = _column_index()


def _rope_tables(n_ctx, n_lat):
    n = jnp.arange(n_lat)
    r = (n // GRID_W).astype(F32)
    col = (n % GRID_W).astype(F32)
    n_freq = HEAD_DIM // 4
    inv = ROPE_BASE ** (-jnp.arange(n_freq, dtype=F32) / n_freq)
    ar = r[:, None] * inv[None, :]
    ac = col[:, None] * inv[None, :]
    cos = jnp.concatenate([jnp.cos(ar), jnp.cos(ar), jnp.cos(ac), jnp.cos(ac)], axis=-1)
    sin = jnp.concatenate([-jnp.sin(ar), jnp.sin(ar), -jnp.sin(ac), jnp.sin(ac)], axis=-1)
    cos = jnp.concatenate([jnp.ones((n_ctx, HEAD_DIM), F32), cos], axis=0)
    sin = jnp.concatenate([jnp.zeros((n_ctx, HEAD_DIM), F32), sin], axis=0)
    return jnp.tile(cos, (1, HEADS)), jnp.tile(sin, (1, HEADS))


def kernel(x, c, ctx, c_ctx, w_ada, b_ada, w_in, g_heads, hgrn_lb_logits, ml_gate_bias,
           rt_decay_logit, gla_w_a, gla_b_a, w_out, w_ff1, w_ff2, g_final):
    B, L, D = x.shape
    Lc = ctx.shape[1]
    depth = w_in.shape[0]
    assert D == D_MODEL and L % ROW_TILE == 0 and Lc % ROW_TILE == 0 and L % GRID_W == 0
    n_ctx_tiles = Lc // ROW_TILE
    n_ctx_chunks = Lc // CHUNK

    xall = jnp.concatenate([ctx, x], axis=1)
    cos_t, sin_t = _rope_tables(Lc, L)

    n_c = -(-(B + 1) // 8) * 8
    cvecs = jnp.zeros((n_c, D), F32).at[:B].set(c).at[B].set(c_ctx)

    sm = jax.nn.softmax(hgrn_lb_logits.astype(F32), axis=0)
    lb_all = jnp.maximum(jnp.cumsum(sm, axis=0) - sm[:1], 0.0)

    col_valid = jnp.asarray(_COL_VALID, F32)[None, :]
    out = None
    for layer in range(depth):
        final = layer == depth - 1
        mod = _ada(cvecs, w_ada[layer], b_ada[layer])
        mods = jnp.stack([jnp.broadcast_to(mod[B], (B, 6 * D)), mod[:B]], axis=1)[:, :, None, :]

        w_in_p = (w_in[layer][:, _COL_IDX] * col_valid).astype(BF16)
        p_mix, gate = _inp(xall, mods, w_in_p, cos_t, sin_t, n_ctx_tiles)

        lb = lb_all[layer]
        hg_par = jnp.stack([jnp.log(lb[0]), jnp.log1p(-lb[0]), 1.0 - lb[0],
                            jnp.log(lb[1]), jnp.log1p(-lb[1]), 1.0 - lb[1],
                            jnp.zeros_like(lb[0]), jnp.zeros_like(lb[0])], axis=0)
        ml_bias = jnp.zeros((1, 128), F32).at[0, :16].set(ml_gate_bias[layer].astype(F32).reshape(16))
        rt_dec = jnp.zeros((8, GROUP_W), F32).at[:2].set(
            jnp.repeat(rt_decay_logit[layer].astype(F32), HEAD_DIM, axis=-1))
        wa = jnp.zeros((2, 128, GLA_W), F32)
        wa = wa.at[0, 16:32].set(gla_w_a[layer, 0].astype(F32)).at[1, 32:48].set(gla_w_a[layer, 1].astype(F32))
        wa_hi = wa.astype(BF16)
        wa_lo = (wa - wa_hi.astype(F32)).astype(BF16)
        ba = jnp.zeros((8, GLA_W), F32).at[:2].set(gla_b_a[layer].astype(F32))

        yf, yb = _mix(p_mix, hg_par, ml_bias, rt_dec, wa_hi, wa_lo, ba, n_ctx_chunks)

        res = _out(xall, yf, yb, gate, mods, g_heads[layer], w_out[layer].astype(BF16),
                   w_ff1[layer].astype(BF16), w_ff2[layer].astype(BF16), g_final,
                   n_ctx_tiles, final)
        if final:
            out = res
        else:
            xall = res
    return out
```

```python
import functools

import numpy as np
import jax
import jax.numpy as jnp
from jax import lax
from jax.experimental import pallas as pl
from jax.experimental.pallas import tpu as pltpu

F32 = jnp.float32
BF16 = jnp.bfloat16

D_MODEL = 1024
GROUP_W = 256
HEADS = 4
HEAD_DIM = 64
GLA_DK = 32
GLA_W = HEADS * GLA_DK
GLA_TAU = 16.0
D_FF = 4 * D_MODEL
GRID_W = 64
ROPE_BASE = 10000.0
RMS_EPS = 1e-6

CHUNK = 64
ROW_TILE = 256
LEVELS = (32, 16, 8, 4, 2, 1)

W_SMALL = 3072
W_GATES = 3200
IN_PAD = W_GATES + D_MODEL

O_HG_Q, O_HG_LF, O_HG_K, O_HG_V = 0, 256, 768, 1280
O_ML_Q, O_ML_K, O_ML_V = 1536, 1792, 2048
O_RT_Q, O_RT_K, O_RT_V = 2304, 2560, 2816
O_GL_Q, O_GL_K, O_GL_V, O_GL_LA = 3072, 3200, 3328, 3584
O_SMALL = 3840
P_MIX = 3968

VMEM_LIMIT = 56 * 1024 * 1024

NEG_INF = float("-inf")


def _dot(a, b):
    return jnp.dot(a, b, preferred_element_type=F32)


def _dot_nt(a, b):
    return lax.dot_general(a, b, (((1,), (1,)), ((), ())), preferred_element_type=F32)


def _dot_tn(a, b):
    return lax.dot_general(a, b, (((0,), (0,)), ((), ())), preferred_element_type=F32)


def _split2(a):
    hi = a.astype(BF16)
    return hi, (a - hi.astype(F32)).astype(BF16)


def _sel_left(m, a):
    hi, lo = _split2(a)
    return _dot(m, hi) + _dot(m, lo)


def _sel_right(a, m):
    hi, lo = _split2(a)
    return _dot(hi, m) + _dot(lo, m)


def _log_sigmoid(z):
    return jnp.minimum(z, 0.0) - jnp.log1p(jnp.exp(-jnp.abs(z)))


def _sigmoid(z):
    return 1.0 / (1.0 + jnp.exp(-z))


def _silu(z):
    return z * _sigmoid(z)


def _rms_scale(x):
    return lax.rsqrt(jnp.mean(x * x, axis=-1, keepdims=True) + RMS_EPS)


def _np_consts():
    C = CHUNK
    t = np.arange(C)
    T, U = np.meshgrid(t, t, indexing="ij")
    incl = [U <= T, U >= T]

    tri = np.stack([incl[d].astype(np.float32) for d in range(2)])
    lev = np.zeros((2, len(LEVELS), C, 4 * C), np.float32)
    for d in range(2):
        for li, h in enumerate(LEVELS):
            same = (T // (2 * h)) == (U // (2 * h))
            if d == 0:
                pair = same & (T % (2 * h) >= h) & (U % (2 * h) < h)
            else:
                pair = same & (T % (2 * h) < h) & (U % (2 * h) >= h)
            lev[d, li] = np.tile(pair.astype(np.float32), (1, 4))

    incl_neg = np.stack([np.tile(np.where(incl[d], 0.0, NEG_INF).astype(np.float32), (1, 4))
                         for d in range(2)])
    tri_t = np.stack([np.tile(incl[d].T.astype(np.float32), (1, 4)) for d in range(2)])
    eye = np.tile(np.eye(C, dtype=np.float32), (1, 4))
    dist = np.tile(np.abs(T - U).astype(np.float32), (1, 4))

    head_of_row = np.arange(4 * C) // C
    hm256 = (head_of_row[:, None] == (np.arange(GROUP_W) // HEAD_DIM)[None, :]).astype(np.float32)
    hm128 = (head_of_row[:, None] == (np.arange(GLA_W) // GLA_DK)[None, :]).astype(np.float32)
    bd256 = ((np.arange(GROUP_W) // HEAD_DIM)[:, None]
             == (np.arange(GROUP_W) // HEAD_DIM)[None, :]).astype(np.float32)
    bd_gl = ((np.arange(GROUP_W) // HEAD_DIM)[:, None]
             == (np.arange(GLA_W) // GLA_DK)[None, :]).astype(np.float32)

    sel_i = np.zeros((2, 128, GROUP_W), np.float32)
    sel_f = np.zeros((2, 128, GROUP_W), np.float32)
    for d in range(2):
        for h in range(HEADS):
            sel_i[d, d * 8 + h, h * HEAD_DIM:(h + 1) * HEAD_DIM] = 1.0
            sel_f[d, d * 8 + 4 + h, h * HEAD_DIM:(h + 1) * HEAD_DIM] = 1.0
    is_f = np.zeros((1, 128), np.float32)
    is_f[0, 4:8] = 1.0
    is_f[0, 12:16] = 1.0
    return dict(tri=tri, lev=lev, incl_neg=incl_neg, tri_t=tri_t, eye=eye, dist=dist,
                hm256=hm256, hm128=hm128, bd256=bd256, bd_gl=bd_gl,
                sel_i=sel_i, sel_f=sel_f, is_f=is_f)


_NP = _np_consts()


def _full_spec(shape):
    n = len(shape)
    return pl.BlockSpec(tuple(shape), lambda *_: (0,) * n)


def _ada_kernel(c_ref, w_ref, b_ref, o_ref):
    act = _silu(c_ref[...]).astype(BF16)
    o_ref[...] = _dot(act, w_ref[...].astype(BF16)) + b_ref[...]


def _ada(cvecs, w, b):
    rows = cvecs.shape[0]
    n = w.shape[1]
    tn = 1024
    return pl.pallas_call(
        _ada_kernel,
        grid=(n // tn,),
        in_specs=[pl.BlockSpec((rows, D_MODEL), lambda i: (0, 0)),
                  pl.BlockSpec((D_MODEL, tn), lambda i: (0, i)),
                  pl.BlockSpec((1, tn), lambda i: (0, i))],
        out_specs=pl.BlockSpec((rows, tn), lambda i: (0, i)),
        out_shape=jax.ShapeDtypeStruct((rows, n), F32),
        compiler_params=pltpu.CompilerParams(dimension_semantics=("arbitrary",),
                                             vmem_limit_bytes=VMEM_LIMIT),
        name="ada",
    )(cvecs, w, b.reshape(1, n))


def _rope(x, cos, sin):
    lane = lax.broadcasted_iota(jnp.int32, x.shape, 1)
    low = (lane % 32) < 16
    partner = jnp.where(low, pltpu.roll(x, GROUP_W - 16, 1), pltpu.roll(x, 16, 1))
    return x * cos + partner * sin


def _inp_kernel(x_ref, mod_ref, w_ref, cos_ref, sin_ref, hg_ref, mlb_ref, isf_ref,
                wa_hi_ref, wa_lo_ref, ba_ref, pm_ref, gate_ref):
    x = x_ref[0]
    sh = mod_ref[0, 0, :, 0:D_MODEL]
    sc = mod_ref[0, 0, :, D_MODEL:2 * D_MODEL]
    xn = ((x * _rms_scale(x)) * (1.0 + sc) + sh).astype(BF16)

    def proj(a, b):
        return _dot(xn, w_ref[:, a:b])

    G = GROUP_W
    pm_ref[0, :, O_HG_Q:O_HG_Q + G] = _silu(proj(0, G))
    for d in range(2):
        z = proj((1 + d) * G, (2 + d) * G)
        log_lb = hg_ref[3 * d + 0:3 * d + 1, :]
        log_1mlb = hg_ref[3 * d + 1:3 * d + 2, :]
        one_m_lb = hg_ref[3 * d + 2:3 * d + 3, :]
        ls = _log_sigmoid(z)
        other = log_1mlb + ls
        pm_ref[0, :, O_HG_LF + d * G:O_HG_LF + (d + 1) * G] = (
            jnp.maximum(log_lb, other) + jnp.log1p(jnp.exp(-jnp.abs(log_lb - other))))
        pm_ref[0, :, O_HG_K + d * G:O_HG_K + (d + 1) * G] = one_m_lb * jnp.exp(ls - z)
    pm_ref[0, :, O_HG_V:O_HG_V + G] = proj(3 * G, 4 * G)

    pm_ref[0, :, O_ML_Q:O_ML_Q + G] = proj(4 * G, 5 * G)
    pm_ref[0, :, O_ML_K:O_ML_K + G] = proj(5 * G, 6 * G) * (HEAD_DIM ** -0.5)
    pm_ref[0, :, O_ML_V:O_ML_V + G] = proj(6 * G, 7 * G)

    cos = cos_ref[...]
    sin = sin_ref[...]
    pm_ref[0, :, O_RT_Q:O_RT_Q + G] = _rope(proj(7 * G, 8 * G), cos, sin)
    pm_ref[0, :, O_RT_K:O_RT_K + G] = _rope(proj(8 * G, 9 * G) * (HEAD_DIM ** -0.5), cos, sin)
    pm_ref[0, :, O_RT_V:O_RT_V + G] = proj(9 * G, 10 * G)

    o = 10 * G
    pm_ref[0, :, O_GL_Q:O_GL_Q + GLA_W] = proj(o, o + GLA_W)
    pm_ref[0, :, O_GL_K:O_GL_K + GLA_W] = proj(o + GLA_W, o + 2 * GLA_W) * (GLA_DK ** -0.5)
    pm_ref[0, :, O_GL_V:O_GL_V + G] = proj(o + 2 * GLA_W, o + 2 * GLA_W + G)

    small = proj(W_SMALL, W_GATES)
    small_hi, small_lo = _split2(small)
    for d in range(2):
        za = (_dot(small_hi, wa_hi_ref[d]) + _dot(small_lo, wa_hi_ref[d])
              + _dot(small_hi, wa_lo_ref[d]) + ba_ref[d:d + 1, :])
        pm_ref[0, :, O_GL_LA + d * GLA_W:O_GL_LA + (d + 1) * GLA_W] = _log_sigmoid(za) * (1.0 / GLA_TAU)
    pre = small + mlb_ref[...]
    pm_ref[0, :, O_SMALL:P_MIX] = jnp.where(isf_ref[...] > 0.5, _log_sigmoid(pre), pre)

    g = proj(W_GATES, IN_PAD)
    gate_ref[0, :, 0:2 * G] = _sigmoid(g[:, 0:2 * G])
    gate_ref[0, :, 2 * G:4 * G] = _silu(g[:, 2 * G:4 * G])


def _inp(xall, mods, w_in_p, cos_t, sin_t, hg_par, ml_bias, wa_hi, wa_lo, ba, n_ctx_tiles):
    B, NT, _ = xall.shape
    T = ROW_TILE
    params = [hg_par, ml_bias, jnp.asarray(_NP["is_f"]), wa_hi, wa_lo, ba]
    return pl.pallas_call(
        _inp_kernel,
        grid=(B, NT // T),
        in_specs=[pl.BlockSpec((1, T, D_MODEL), lambda b, i: (b, i, 0)),
                  pl.BlockSpec((1, 1, 1, 6 * D_MODEL),
                               lambda b, i: (b, jnp.where(i < n_ctx_tiles, 0, 1), 0, 0)),
                  pl.BlockSpec((D_MODEL, IN_PAD), lambda b, i: (0, 0)),
                  pl.BlockSpec((T, GROUP_W), lambda b, i: (i, 0)),
                  pl.BlockSpec((T, GROUP_W), lambda b, i: (i, 0))]
        + [_full_spec(a.shape) for a in params],
        out_specs=[pl.BlockSpec((1, T, P_MIX), lambda b, i: (b, i, 0)),
                   pl.BlockSpec((1, T, D_MODEL), lambda b, i: (b, i, 0))],
        out_shape=[jax.ShapeDtypeStruct((B, NT, P_MIX), F32),
                   jax.ShapeDtypeStruct((B, NT, D_MODEL), F32)],
        compiler_params=pltpu.CompilerParams(dimension_semantics=("parallel", "parallel"),
                                             vmem_limit_bytes=VMEM_LIMIT),
        name="inp",
    )(xall, mods, w_in_p, cos_t, sin_t, *params)


def _tile4(a):
    return jnp.concatenate([a, a, a, a], axis=0)


def _last_row(a, d):
    return a[CHUNK - 1:CHUNK, :] if d == 0 else a[0:1, :]


def _level_exponents(d, h, log_a, b, row):
    C, W = log_a.shape
    if h >= 4:
        b3 = b.reshape(C // (2 * h), 2 * h, W)
        r = h - 1 if d == 0 else h
        ref = jnp.broadcast_to(b3[:, r:r + 1, :], b3.shape).reshape(C, W)
        return jnp.minimum(b - ref, 0.0), jnp.minimum(ref - b, 0.0)
    if h == 1:
        return log_a, None
    up = pltpu.roll(log_a, C - 1, 0)
    dn = pltpu.roll(log_a, 1, 0)
    if d == 0:
        return log_a + jnp.where(row % 4 == 3, dn, 0.0), jnp.where(row % 4 == 0, up, 0.0)
    return log_a + jnp.where(row % 4 == 0, up, 0.0), jnp.where(row % 4 == 3, dn, 0.0)


def _interleave(chains):
    live = list(chains)
    while live:
        for g in list(live):
            try:
                next(g)
            except StopIteration:
                live.remove(g)


def _vector_decay_chain(d, p_ref, o_q, o_k, o_v, o_la, W, y_ref, o_y, st_ref, tri, lev_ref,
                        e_mat, hm_k, hm_v, bd_state):
    C = CHUNK
    G = GROUP_W
    q = p_ref[0, :, o_q:o_q + W]
    k = p_ref[0, :, o_k:o_k + W]
    log_a = p_ref[0, :, o_la:o_la + W]
    b = _sel_left(tri, log_a)
    row = lax.broadcasted_iota(jnp.int32, (C, W), 0)
    order = sorted(range(len(LEVELS)), key=lambda li: LEVELS[li] >= 4)
    attn = None
    pending = None
    for li in order:
        ea, eb = _level_exponents(d, LEVELS[li], log_a, b, row)
        qa = (q * jnp.exp(ea)).astype(BF16)
        kb = (k if eb is None else k * jnp.exp(eb)).astype(BF16)
        part = _dot_nt(qa, _tile4(kb) * hm_k)
        if pending is not None:
            attn = pending if attn is None else attn + pending
        yield
        pending = part * lev_ref[d, li]
    attn = attn + pending

    v = p_ref[0, :, o_v:o_v + G]
    v16 = v.astype(BF16)
    st = st_ref[d]
    b_end = _last_row(b, d)
    q_state = (q * jnp.exp(b)).astype(BF16)
    k_end = (k * jnp.exp(b_end - b)).astype(BF16)
    o = _dot(attn.astype(BF16), _tile4(v16) * hm_v) + _dot_nt(q_state, st.astype(BF16))
    diag = _dot((q * k).astype(BF16), e_mat)
    upd = _dot_tn(v16, k_end)
    yield
    y_ref[0, :, o_y:o_y + G] = o + diag * v
    st_ref[d] = jnp.exp(b_end) * st + bd_state * upd


def _mlstm_chain(d, p_ref, y_ref, s_ml, n_ml, m_ml, tri, incl_neg, trit, eye, seli, self_,
                 hm256, bd256, bd256_16):
    C = CHUNK
    G = GROUP_W
    small = p_ref[0, :, O_SMALL:P_MIX]
    i_bc = _sel_right(small, seli)
    f_bc = _sel_right(small, self_)
    q16 = p_ref[0, :, O_ML_Q:O_ML_Q + G].astype(BF16)
    k = p_ref[0, :, O_ML_K:O_ML_K + G]
    qk = _dot_nt(q16, _tile4(k.astype(BF16)) * hm256)
    yield
    ones_row = jnp.ones((8, C), BF16)
    b_col = _sel_left(tri, f_bc)
    b_row = _sel_left(ones_row, f_bc * trit)[0:1]
    i_row = _sel_left(ones_row, i_bc * eye)[0:1]
    yield
    w = b_col - b_row + i_row + incl_neg
    m_prev = m_ml[d, 0:1, :]
    inter = b_col + m_prev
    row_max = [jnp.max(w[:, h * HEAD_DIM:(h + 1) * HEAD_DIM], axis=-1, keepdims=True)
               for h in range(HEADS)]
    yield
    lane_head = lax.broadcasted_iota(jnp.int32, (C, G), 1) // HEAD_DIM
    m_t = None
    for h in range(HEADS):
        mh = jnp.maximum(row_max[h], inter[:, h * HEAD_DIM:h * HEAD_DIM + 1])
        mh = jnp.broadcast_to(mh, (C, G))
        m_t = mh if m_t is None else jnp.where(lane_head == h, mh, m_t)
    g_in = jnp.exp(inter - m_t)
    s = (qk * jnp.exp(w - m_t)).astype(BF16)
    v16 = p_ref[0, :, O_ML_V:O_ML_V + G].astype(BF16)
    c_st = s_ml[d]
    n_row = n_ml[d, 0:1, :]
    qn = (q16.astype(F32) * n_row).astype(BF16)
    num_a = _dot(s, _tile4(v16) * hm256)
    num_b = _dot(q16, c_st.astype(BF16))
    den_a = _dot(s, bd256_16)
    den_b = _dot(qn, bd256_16)
    m_new = _last_row(m_t, d)
    b_end = _last_row(b_col, d)
    a_state = jnp.exp(b_end + m_prev - m_new)
    kw = k * jnp.exp(b_end - b_col + i_bc - m_new)
    upd = _dot_tn(kw.astype(BF16), v16)
    yield
    num = num_a + g_in * num_b
    den = den_a + g_in * den_b
    y_ref[0, :, G:2 * G] = num / jnp.maximum(jnp.abs(den), jnp.exp(-m_t))
    s_ml[d] = a_state * c_st + bd256 * upd
    n_ml[d] = jnp.broadcast_to(a_state * n_row + jnp.sum(kw, axis=0, keepdims=True), (8, G))
    m_ml[d] = jnp.broadcast_to(m_new, (8, G))


def _retention_chain(d, p_ref, y_ref, s_rt, log_g, dist, incl_neg, hm256, bd256):
    C = CHUNK
    G = GROUP_W
    q = p_ref[0, :, O_RT_Q:O_RT_Q + G]
    k = p_ref[0, :, O_RT_K:O_RT_K + G]
    qk = _dot_nt(q.astype(BF16), _tile4(k.astype(BF16)) * hm256)
    decay = jnp.exp(log_g * dist + incl_neg)
    yield
    v16 = p_ref[0, :, O_RT_V:O_RT_V + G].astype(BF16)
    rowc = lax.broadcasted_iota(jnp.int32, (C, 1), 0).astype(F32)
    cnt = (rowc + 1.0) if d == 0 else (C - rowc)
    q_state = (q * jnp.exp(log_g * cnt)).astype(BF16)
    k_end = (k * jnp.exp(log_g * (C - cnt))).astype(BF16)
    st = s_rt[d]
    o = _dot((qk * decay).astype(BF16), _tile4(v16) * hm256) + _dot(q_state, st.astype(BF16))
    upd = _dot_tn(k_end, v16)
    yield
    y_ref[0, :, 2 * G:3 * G] = o
    s_rt[d] = jnp.exp(log_g * float(C)) * st + bd256 * upd


def _mix_kernel(pf_ref, pb_ref, rtg_ref,
                tri_ref, lev_ref, incl_ref, trit_ref, eye_ref, dist_ref,
                hm256_ref, hm128_ref, bd256_ref, bdgl_ref, egl_ref, seli_ref, self_ref,
                yf_ref, yb_ref,
                s_hg, s_rt, s_gl, s_ml, n_ml, m_ml):
    C = CHUNK
    G = GROUP_W

    @pl.when(pl.program_id(1) == 0)
    def _():
        s_hg[...] = jnp.zeros_like(s_hg)
        s_rt[...] = jnp.zeros_like(s_rt)
        s_gl[...] = jnp.zeros_like(s_gl)
        s_ml[...] = jnp.zeros_like(s_ml)
        n_ml[...] = jnp.zeros_like(n_ml)
        m_ml[...] = jnp.zeros_like(m_ml)

    hm256 = hm256_ref[...]
    hm128 = hm128_ref[...]
    bd256 = bd256_ref[...]
    bd256_16 = bd256.astype(BF16)

    chains = []
    for d, (p_ref, y_ref) in enumerate(((pf_ref, yf_ref), (pb_ref, yb_ref))):
        incl_neg = incl_ref[d]
        tri = tri_ref[d]
        chains += [
            _vector_decay_chain(d, p_ref, O_HG_Q, O_HG_K + d * G, O_HG_V, O_HG_LF + d * G, G,
                                y_ref, 0, s_hg, tri, lev_ref, bd256_16, hm256, hm256, bd256),
            _mlstm_chain(d, p_ref, y_ref, s_ml, n_ml, m_ml, tri, incl_neg, trit_ref[d], eye_ref[...],
                         seli_ref[d], self_ref[d], hm256, bd256, bd256_16),
            _retention_chain(d, p_ref, y_ref, s_rt, rtg_ref[d:d + 1, :], dist_ref[...], incl_neg,
                             hm256, bd256),
            _vector_decay_chain(d, p_ref, O_GL_Q, O_GL_K, O_GL_V, O_GL_LA + d * GLA_W, GLA_W,
                                y_ref, 3 * G, s_gl, tri, lev_ref, egl_ref[...], hm128, hm256,
                                bdgl_ref[...]),
        ]
    _interleave(chains)


def _mix_call(B, NT, n_ctx_chunks):
    C = CHUNK
    NC = NT // C
    G = GROUP_W

    def fwd_map(b, j):
        return (b, j, 0)

    def bwd_map(b, j):
        return (b, jnp.where(j < n_ctx_chunks, n_ctx_chunks - 1 - j, NC - 1 - (j - n_ctx_chunks)), 0)

    consts = [jnp.asarray(_NP["tri"], BF16), jnp.asarray(_NP["lev"]), jnp.asarray(_NP["incl_neg"]),
              jnp.asarray(_NP["tri_t"]), jnp.asarray(_NP["eye"]), jnp.asarray(_NP["dist"]),
              jnp.asarray(_NP["hm256"], BF16), jnp.asarray(_NP["hm128"], BF16),
              jnp.asarray(_NP["bd256"]), jnp.asarray(_NP["bd_gl"]), jnp.asarray(_NP["bd_gl"].T, BF16),
              jnp.asarray(_NP["sel_i"], BF16), jnp.asarray(_NP["sel_f"], BF16)]
    call = pl.pallas_call(
        _mix_kernel,
        grid=(B, NC),
        in_specs=[pl.BlockSpec((1, C, P_MIX), fwd_map), pl.BlockSpec((1, C, P_MIX), bwd_map),
                  _full_spec((8, G))] + [_full_spec(a.shape) for a in consts],
        out_specs=[pl.BlockSpec((1, C, D_MODEL), fwd_map), pl.BlockSpec((1, C, D_MODEL), bwd_map)],
        out_shape=[jax.ShapeDtypeStruct((B, NT, D_MODEL), F32)] * 2,
        scratch_shapes=[pltpu.VMEM((2, G, G), F32), pltpu.VMEM((2, G, G), F32),
                        pltpu.VMEM((2, G, GLA_W), F32), pltpu.VMEM((2, G, G), F32),
                        pltpu.VMEM((2, 8, G), F32), pltpu.VMEM((2, 8, G), F32)],
        compiler_params=pltpu.CompilerParams(dimension_semantics=("parallel", "arbitrary"),
                                             vmem_limit_bytes=VMEM_LIMIT),
        name="mix",
    )
    return lambda p_mix, rt_par: call(p_mix, p_mix, rt_par, *consts)


def _out_kernel(final, x_ref, yf_ref, yb_ref, gate_ref, mod_ref, gh_ref, bd_ref,
                wo_ref, w1_ref, w2_ref, gf_ref, o_ref):
    D = D_MODEL
    x = x_ref[0]
    bd = bd_ref[...]
    for g in range(D // GROUP_W):
        sl = slice(g * GROUP_W, (g + 1) * GROUP_W)
        y = yf_ref[0, :, sl] + yb_ref[0, :, sl]
        msq = _sel_right(y * y, bd) * (1.0 / HEAD_DIM)
        yn = y * lax.rsqrt(msq + RMS_EPS) * gh_ref[:, sl] * gate_ref[0, :, sl]
        part = _dot(yn.astype(BF16), wo_ref[sl, :])
        mixed = part if g == 0 else mixed + part
    g1 = mod_ref[0, 0, :, 2 * D:3 * D]
    sh2 = mod_ref[0, 0, :, 3 * D:4 * D]
    sc2 = mod_ref[0, 0, :, 4 * D:5 * D]
    g2 = mod_ref[0, 0, :, 5 * D:6 * D]
    x1 = x + g1 * mixed
    hin = ((x1 * _rms_scale(x1)) * (1.0 + sc2) + sh2).astype(BF16)
    hid = jnp.maximum(_dot(hin, w1_ref[...]), 0.0)
    ff = _dot((hid * hid).astype(BF16), w2_ref[...])
    x2 = x1 + g2 * ff
    if final:
        x2 = (x2 * _rms_scale(x2)) * gf_ref[...]
    o_ref[0] = x2


def _out(xall, yf, yb, gate, mods, g_heads, w_out, w_ff1, w_ff2, g_final, n_ctx_tiles, final):
    B, NT, D = xall.shape
    T = ROW_TILE
    t0 = n_ctx_tiles if final else 0
    n_tiles = NT // T - t0
    tok = lambda b, i: (b, i + t0, 0)
    bd = jnp.asarray(_NP["bd256"], BF16)
    wspec = lambda shape: pl.BlockSpec(shape, lambda b, i: (0, 0), pipeline_mode=pl.Buffered(1))
    return pl.pallas_call(
        functools.partial(_out_kernel, final),
        grid=(B, n_tiles),
        in_specs=[pl.BlockSpec((1, T, D), tok), pl.BlockSpec((1, T, D), tok),
                  pl.BlockSpec((1, T, D), tok), pl.BlockSpec((1, T, D), tok),
                  pl.BlockSpec((1, 1, 1, 6 * D),
                               lambda b, i: (b, jnp.where(i + t0 < n_ctx_tiles, 0, 1), 0, 0)),
                  pl.BlockSpec((1, D), lambda b, i: (0, 0)),
                  pl.BlockSpec((GROUP_W, GROUP_W), lambda b, i: (0, 0)),
                  wspec((D, D)), wspec((D, D_FF)), wspec((D_FF, D)),
                  pl.BlockSpec((1, D), lambda b, i: (0, 0))],
        out_specs=pl.BlockSpec((1, T, D), lambda b, i: (b, i, 0)),
        out_shape=jax.ShapeDtypeStruct((B, n_tiles * T, D), F32),
        compiler_params=pltpu.CompilerParams(dimension_semantics=("parallel", "parallel"),
                                             vmem_limit_bytes=VMEM_LIMIT),
        name="out_final" if final else "out",
    )(xall, yf, yb, gate, mods, g_heads.reshape(1, D), bd, w_out, w_ff1, w_ff2,
      g_final.reshape(1, D))


_REF_LAYOUT = (('hg_q', 256), ('hg_f_fwd', 256), ('hg_f_bwd', 256), ('hg_i', 256), ('hg_g', 256),
               ('ml_q', 256), ('ml_k', 256), ('ml_v', 256), ('ml_if', 16), ('ml_o', 256),
               ('rt_q', 256), ('rt_k', 256), ('rt_v', 256), ('rt_g', 256),
               ('gl_q', 128), ('gl_k', 128), ('gl_v', 256), ('gl_a_fwd', 16), ('gl_a_bwd', 16),
               ('gl_g', 256))
_NEW_ORDER = ('hg_q', 'hg_f_fwd', 'hg_f_bwd', 'hg_i', 'ml_q', 'ml_k', 'ml_v', 'rt_q', 'rt_k', 'rt_v',
              'gl_q', 'gl_k', 'gl_v', 'ml_if', 'gl_a_fwd', 'gl_a_bwd', 'PAD', 'hg_g', 'ml_o', 'rt_g', 'gl_g')


def _column_index():
    starts, off = {}, 0
    for name, w in _REF_LAYOUT:
        starts[name] = (off, w)
        off += w
    idx = []
    for name in _NEW_ORDER:
        if name == 'PAD':
            idx += [-1] * (128 - 48)
        else:
            a, w = starts[name]
            idx += list(range(a, a + w))
    idx = np.asarray(idx)
    assert idx.shape[0] == IN_PAD
    return np.where(idx < 0, 0, idx), (idx >= 0)


_COL_IDX, _COL_VALID = _column_index()


def _rope_tables(n_ctx, n_lat):
    n = jnp.arange(n_lat)
    r = (n // GRID_W).astype(F32)
    col = (n % GRID_W).astype(F32)
    n_freq = HEAD_DIM // 4
    inv = ROPE_BASE ** (-jnp.arange(n_freq, dtype=F32) / n_freq)
    ar = r[:, None] * inv[None, :]
    ac = col[:, None] * inv[None, :]
    cos = jnp.concatenate([jnp.cos(ar), jnp.cos(ar), jnp.cos(ac), jnp.cos(ac)], axis=-1)
    sin = jnp.concatenate([-jnp.sin(ar), jnp.sin(ar), -jnp.sin(ac), jnp.sin(ac)], axis=-1)
    cos = jnp.concatenate([jnp.ones((n_ctx, HEAD_DIM), F32), cos], axis=0)
    sin = jnp.concatenate([jnp.zeros((n_ctx, HEAD_DIM), F32), sin], axis=0)
    return jnp.tile(cos, (1, HEADS)), jnp.tile(sin, (1, HEADS))


def kernel(x, c, ctx, c_ctx, w_ada, b_ada, w_in, g_heads, hgrn_lb_logits, ml_gate_bias,
           rt_decay_logit, gla_w_a, gla_b_a, w_out, w_ff1, w_ff2, g_final):
    B, L, D = x.shape
    Lc = ctx.shape[1]
    depth = w_in.shape[0]
    assert D == D_MODEL and L % ROW_TILE == 0 and Lc % ROW_TILE == 0 and L % GRID_W == 0
    n_ctx_tiles = Lc // ROW_TILE
    n_ctx_chunks = Lc // CHUNK

    xall = jnp.concatenate([ctx, x], axis=1)
    cos_t, sin_t = _rope_tables(Lc, L)
    mix = _mix_call(B, Lc + L, n_ctx_chunks)

    n_c = -(-(B + 1) // 8) * 8
    cvecs = jnp.zeros((n_c, D), F32).at[:B].set(c).at[B].set(c_ctx)

    sm = jax.nn.softmax(hgrn_lb_logits.astype(F32), axis=0)
    lb_all = jnp.maximum(jnp.cumsum(sm, axis=0) - sm[:1], 0.0)

    col_valid = jnp.asarray(_COL_VALID, F32)[None, :]
    out = None
    for layer in range(depth):
        final = layer == depth - 1
        mod = _ada(cvecs, w_ada[layer], b_ada[layer])
        mods = jnp.stack([jnp.broadcast_to(mod[B], (B, 6 * D)), mod[:B]], axis=1)[:, :, None, :]

        w_in_p = (w_in[layer][:, _COL_IDX] * col_valid).astype(BF16)
        lb = lb_all[layer]
        hg_par = jnp.stack([jnp.log(lb[0]), jnp.log1p(-lb[0]), 1.0 - lb[0],
                            jnp.log(lb[1]), jnp.log1p(-lb[1]), 1.0 - lb[1],
                            jnp.zeros_like(lb[0]), jnp.zeros_like(lb[0])], axis=0)
        ml_bias = jnp.zeros((1, 128), F32).at[0, :16].set(ml_gate_bias[layer].astype(F32).reshape(16))
        wa = jnp.zeros((2, 128, GLA_W), F32)
        wa = wa.at[0, 16:32].set(gla_w_a[layer, 0].astype(F32)).at[1, 32:48].set(gla_w_a[layer, 1].astype(F32))
        wa_hi = wa.astype(BF16)
        wa_lo = (wa - wa_hi.astype(F32)).astype(BF16)
        ba = jnp.zeros((8, GLA_W), F32).at[:2].set(gla_b_a[layer].astype(F32))
        p_mix, gate = _inp(xall, mods, w_in_p, cos_t, sin_t, hg_par, ml_bias, wa_hi, wa_lo, ba,
                           n_ctx_tiles)

        rt_par = jnp.zeros((8, GROUP_W), F32).at[:2].set(
            jnp.repeat(jax.nn.log_sigmoid(rt_decay_logit[layer].astype(F32)), HEAD_DIM, axis=-1))
        yf, yb = mix(p_mix, rt_par)

        res = _out(xall, yf, yb, gate, mods, g_heads[layer], w_out[layer].astype(BF16),
                   w_ff1[layer].astype(BF16), w_ff2[layer].astype(BF16), g_final,
                   n_ctx_tiles, final)
        if final:
            out = res
        else:
            xall = res
    return out
```

```python
import functools

import numpy as np
import jax
import jax.numpy as jnp
from jax import lax
from jax.experimental import pallas as pl
from jax.experimental.pallas import tpu as pltpu

F32 = jnp.float32
BF16 = jnp.bfloat16

D_MODEL = 1024
GROUP_W = 256
HEADS = 4
HEAD_DIM = 64
GLA_DK = 32
GLA_W = HEADS * GLA_DK
GLA_TAU = 16.0
D_FF = 4 * D_MODEL
GRID_W = 64
ROPE_BASE = 10000.0
RMS_EPS = 1e-6

CHUNK = 64
ROW_TILE = 256
LEVELS = (32, 16, 8, 4, 2, 1)

W_SMALL = 3072
W_GATES = 3200
IN_PAD = W_GATES + D_MODEL

O_HG_Q, O_HG_LF, O_HG_K, O_HG_V = 0, 256, 768, 1280
O_ML_Q, O_ML_K, O_ML_V = 1536, 1792, 2048
O_RT_Q, O_RT_K, O_RT_V = 2304, 2560, 2816
O_GL_Q, O_GL_K, O_GL_V, O_GL_LA = 3072, 3200, 3328, 3584
O_SMALL = 3840
P_MIX = 3968

VMEM_LIMIT = 56 * 1024 * 1024

NEG_INF = float("-inf")
LOG2E = 1.4426950408889634


def _dot(a, b):
    return jnp.dot(a, b, preferred_element_type=F32)


def _dot_nt(a, b):
    return lax.dot_general(a, b, (((1,), (1,)), ((), ())), preferred_element_type=F32)


def _dot_tn(a, b):
    return lax.dot_general(a, b, (((0,), (0,)), ((), ())), preferred_element_type=F32)


def _split2(a):
    hi = a.astype(BF16)
    return hi, (a - hi.astype(F32)).astype(BF16)


def _sel_left(m, a):
    hi, lo = _split2(a)
    return _dot(m, hi) + _dot(m, lo)


def _sel_right(a, m):
    hi, lo = _split2(a)
    return _dot(hi, m) + _dot(lo, m)


def _neg_abs(x):
    return pltpu.bitcast(pltpu.bitcast(x, jnp.uint32) | jnp.uint32(0x80000000), F32)


def _log2_sigmoid(z2):
    return jnp.minimum(z2, 0.0) - jnp.log2(1.0 + jnp.exp2(_neg_abs(z2)))


def _sigmoid(z):
    return 1.0 / (1.0 + jnp.exp(-z))


def _silu(z):
    return z * _sigmoid(z)


def _rms_scale(x):
    return lax.rsqrt(jnp.mean(x * x, axis=-1, keepdims=True) + RMS_EPS)


def _np_consts():
    C = CHUNK
    t = np.arange(C)
    T, U = np.meshgrid(t, t, indexing="ij")
    incl = [U <= T, U >= T]

    tri = np.stack([incl[d].astype(np.float32) for d in range(2)])
    lev = np.zeros((2, len(LEVELS), C, 4 * C), np.float32)
    for d in range(2):
        for li, h in enumerate(LEVELS):
            same = (T // (2 * h)) == (U // (2 * h))
            if d == 0:
                pair = same & (T % (2 * h) >= h) & (U % (2 * h) < h)
            else:
                pair = same & (T % (2 * h) < h) & (U % (2 * h) >= h)
            lev[d, li] = np.tile(pair.astype(np.float32), (1, 4))

    incl_neg = np.stack([np.tile(np.where(incl[d], 0.0, NEG_INF).astype(np.float32), (1, 4))
                         for d in range(2)])
    tri_t = np.stack([np.tile(incl[d].T.astype(np.float32), (1, 4)) for d in range(2)])
    eye = np.tile(np.eye(C, dtype=np.float32), (1, 4))
    dist = np.tile(np.abs(T - U).astype(np.float32), (1, 4))

    lane = np.arange(128)
    hmask = np.stack([np.broadcast_to(m, (C, 128)) for m in
                      [lane < 64, lane >= 64] + [lane // GLA_DK == h for h in range(HEADS)]]
                     ).astype(np.float32)
    bd256 = ((np.arange(GROUP_W) // HEAD_DIM)[:, None]
             == (np.arange(GROUP_W) // HEAD_DIM)[None, :]).astype(np.float32)
    bd_gl = ((np.arange(GROUP_W) // HEAD_DIM)[:, None]
             == (np.arange(GLA_W) // GLA_DK)[None, :]).astype(np.float32)

    sel_i = np.zeros((2, 128, GROUP_W), np.float32)
    sel_f = np.zeros((2, 128, GROUP_W), np.float32)
    for d in range(2):
        for h in range(HEADS):
            sel_i[d, d * 8 + h, h * HEAD_DIM:(h + 1) * HEAD_DIM] = 1.0
            sel_f[d, d * 8 + 4 + h, h * HEAD_DIM:(h + 1) * HEAD_DIM] = 1.0
    is_f = np.zeros((1, 128), np.float32)
    is_f[0, 4:8] = 1.0
    is_f[0, 12:16] = 1.0
    return dict(tri=tri, lev=lev, incl_neg=incl_neg, tri_t=tri_t, eye=eye, dist=dist,
                hmask=hmask, bd256=bd256, bd_gl=bd_gl,
                sel_i=sel_i, sel_f=sel_f, is_f=is_f)


_NP = _np_consts()


def _full_spec(shape):
    n = len(shape)
    return pl.BlockSpec(tuple(shape), lambda *_: (0,) * n)


def _ada_kernel(c_ref, w_ref, b_ref, o_ref):
    act = _silu(c_ref[...]).astype(BF16)
    o_ref[...] = _dot(act, w_ref[...].astype(BF16)) + b_ref[...]


def _ada(cvecs, w, b):
    rows = cvecs.shape[0]
    n = w.shape[1]
    tn = 1024
    return pl.pallas_call(
        _ada_kernel,
        grid=(n // tn,),
        in_specs=[pl.BlockSpec((rows, D_MODEL), lambda i: (0, 0)),
                  pl.BlockSpec((D_MODEL, tn), lambda i: (0, i)),
                  pl.BlockSpec((1, tn), lambda i: (0, i))],
        out_specs=pl.BlockSpec((rows, tn), lambda i: (0, i)),
        out_shape=jax.ShapeDtypeStruct((rows, n), F32),
        compiler_params=pltpu.CompilerParams(dimension_semantics=("arbitrary",),
                                             vmem_limit_bytes=VMEM_LIMIT),
        name="ada",
    )(cvecs, w, b.reshape(1, n))


def _rope(x, cos, sin):
    lane = lax.broadcasted_iota(jnp.int32, x.shape, 1)
    low = (lane % 32) < 16
    partner = jnp.where(low, pltpu.roll(x, GROUP_W - 16, 1), pltpu.roll(x, 16, 1))
    return x * cos + partner * sin


def _inp_kernel(x_ref, mod_ref, w_ref, cos_ref, sin_ref, hg_ref, mlb_ref, isf_ref,
                wa_hi_ref, wa_lo_ref, ba_ref, pm_ref, gate_ref):
    x = x_ref[0]
    sh = mod_ref[0, 0, :, 0:D_MODEL]
    sc = mod_ref[0, 0, :, D_MODEL:2 * D_MODEL]
    xn = ((x * _rms_scale(x)) * (1.0 + sc) + sh).astype(BF16)

    def proj(a, b):
        return _dot(xn, w_ref[:, a:b])

    G = GROUP_W

    def put(off, width, fn=None):
        def post(res):
            pm_ref[0, :, off:off + width] = res if fn is None else fn(res)
        return post

    def hgrn_decay(d):
        def post(z):
            log_lb = hg_ref[3 * d + 0:3 * d + 1, :]
            log_1mlb = hg_ref[3 * d + 1:3 * d + 2, :]
            one_m_lb = hg_ref[3 * d + 2:3 * d + 3, :]
            z2 = z * LOG2E
            ls = _log2_sigmoid(z2)
            other = log_1mlb + ls
            pm_ref[0, :, O_HG_LF + d * G:O_HG_LF + (d + 1) * G] = (
                jnp.maximum(log_lb, other) + jnp.log2(1.0 + jnp.exp2(_neg_abs(log_lb - other))))
            pm_ref[0, :, O_HG_K + d * G:O_HG_K + (d + 1) * G] = one_m_lb * jnp.exp2(ls - z2)
        return post

    def narrow(small):
        small_hi, small_lo = _split2(small)
        for d in range(2):
            za = (_dot(small_hi, wa_hi_ref[d]) + _dot(small_lo, wa_hi_ref[d])
                  + _dot(small_hi, wa_lo_ref[d]) + ba_ref[d:d + 1, :])
            pm_ref[0, :, O_GL_LA + d * GLA_W:O_GL_LA + (d + 1) * GLA_W] = (
                _log2_sigmoid(za * LOG2E) * (1.0 / GLA_TAU))
        pre2 = (small + mlb_ref[...]) * LOG2E
        pm_ref[0, :, O_SMALL:P_MIX] = jnp.where(isf_ref[...] > 0.5, _log2_sigmoid(pre2), pre2)

    def gates(g):
        gate_ref[0, :, 0:2 * G] = _sigmoid(g[:, 0:2 * G])
        gate_ref[0, :, 2 * G:4 * G] = _silu(g[:, 2 * G:4 * G])

    def rope(scale):
        return lambda r: _rope(r * scale if scale != 1.0 else r, cos_ref[...], sin_ref[...])

    o = 10 * G
    stages = [
        (0, G, put(O_HG_Q, G, _silu)),
        (G, 2 * G, hgrn_decay(0)),
        (2 * G, 3 * G, hgrn_decay(1)),
        (3 * G, 5 * G, put(O_HG_V, 2 * G)),
        (5 * G, 6 * G, put(O_ML_K, G, lambda r: r * (HEAD_DIM ** -0.5))),
        (6 * G, 7 * G, put(O_ML_V, G)),
        (7 * G, 8 * G, put(O_RT_Q, G, rope(1.0))),
        (8 * G, 9 * G, put(O_RT_K, G, rope(HEAD_DIM ** -0.5))),
        (9 * G, o + GLA_W, put(O_RT_V, G + GLA_W)),
        (o + GLA_W, o + 2 * GLA_W, put(O_GL_K, GLA_W, lambda r: r * (GLA_DK ** -0.5))),
        (o + 2 * GLA_W, W_SMALL, put(O_GL_V, G)),
        (W_SMALL, W_GATES, narrow),
        (W_GATES, IN_PAD, gates),
    ]
    pending = None
    for a, b, post in stages:
        res = proj(a, b)
        if pending is not None:
            pending[1](pending[0])
        pending = (res, post)
    pending[1](pending[0])


def _inp(xall, mods, w_in_p, cos_t, sin_t, hg_par, ml_bias, wa_hi, wa_lo, ba, n_ctx_tiles):
    B, NT, _ = xall.shape
    T = ROW_TILE
    params = [hg_par, ml_bias, jnp.asarray(_NP["is_f"]), wa_hi, wa_lo, ba]
    return pl.pallas_call(
        _inp_kernel,
        grid=(B, NT // T),
        in_specs=[pl.BlockSpec((1, T, D_MODEL), lambda b, i: (b, i, 0)),
                  pl.BlockSpec((1, 1, 1, 6 * D_MODEL),
                               lambda b, i: (b, jnp.where(i < n_ctx_tiles, 0, 1), 0, 0)),
                  pl.BlockSpec((D_MODEL, IN_PAD), lambda b, i: (0, 0)),
                  pl.BlockSpec((T, GROUP_W), lambda b, i: (i, 0)),
                  pl.BlockSpec((T, GROUP_W), lambda b, i: (i, 0))]
        + [_full_spec(a.shape) for a in params],
        out_specs=[pl.BlockSpec((1, T, P_MIX), lambda b, i: (b, i, 0)),
                   pl.BlockSpec((1, T, D_MODEL), lambda b, i: (b, i, 0))],
        out_shape=[jax.ShapeDtypeStruct((B, NT, P_MIX), F32),
                   jax.ShapeDtypeStruct((B, NT, D_MODEL), F32)],
        compiler_params=pltpu.CompilerParams(dimension_semantics=("parallel", "parallel"),
                                             vmem_limit_bytes=VMEM_LIMIT),
        name="inp",
    )(xall, mods, w_in_p, cos_t, sin_t, *params)


LANES = 128


def _head_blocks(a16, hmask_ref):
    C, W = a16.shape
    per_head = W // HEADS
    zero = jnp.zeros((C, LANES), a16.dtype)
    blocks = []
    for h in range(HEADS):
        j, off = divmod(h * per_head, LANES)
        mask = hmask_ref[off // 64] if per_head == 64 else hmask_ref[2 + h]
        part = a16[:, j * LANES:(j + 1) * LANES] * mask
        blocks.append(part if W == LANES else
                      jnp.concatenate([part if jj == j else zero for jj in range(W // LANES)], axis=1))
    return jnp.concatenate(blocks, axis=0)


def _last_row(a, d):
    return a[CHUNK - 1:CHUNK, :] if d == 0 else a[0:1, :]


def _level_exponent(d, h, log_a, b, row4):
    C, W = log_a.shape
    if h >= 4:
        b3 = b.reshape(C // (2 * h), 2 * h, W)
        r = h - 1 if d == 0 else h
        ref = jnp.broadcast_to(b3[:, r:r + 1, :], b3.shape).reshape(C, W)
        return _neg_abs(b - ref)
    up = pltpu.roll(log_a, C - 1, 0)
    dn = pltpu.roll(log_a, 1, 0)
    if d == 0:
        return jnp.where(row4 == 0, up, jnp.where(row4 == 1, 0.0,
                                                  log_a + jnp.where(row4 == 3, dn, 0.0)))
    return jnp.where(row4 == 0, log_a + up,
                     jnp.where(row4 == 1, log_a, jnp.where(row4 == 2, 0.0, dn)))


def _interleave(chains):
    live = list(chains)
    while live:
        for g in list(live):
            try:
                next(g)
            except StopIteration:
                live.remove(g)


def _update_state(st_ref, d, st, decay_row, bd_state, upd):
    rows = st.shape[0] // HEADS
    per_head = st.shape[1] // HEADS
    for h in range(HEADS):
        j = (h * per_head) // LANES
        rs = slice(h * rows, (h + 1) * rows)
        cs = slice(j * LANES, (j + 1) * LANES)
        st_ref[d, rs, cs] = decay_row[:, cs] * st[rs, cs] + bd_state[rs, cs] * upd[rs, cs]


def _vector_decay_chain(d, p_ref, o_q, o_k, o_v, o_la, W, y_ref, o_y, st_ref, tri, lev_ref,
                        e_mat, hmask_ref, bd_state):
    C = CHUNK
    G = GROUP_W
    q16 = p_ref[0, :, o_q:o_q + W].astype(BF16)
    k16 = p_ref[0, :, o_k:o_k + W].astype(BF16)
    log_a = p_ref[0, :, o_la:o_la + W]
    b = _sel_left(tri, log_a)
    row4 = lax.broadcasted_iota(jnp.int32, (C, W), 0) & 3
    order = sorted(range(len(LEVELS)), key=lambda li: LEVELS[li] >= 4)
    attn = None
    pending = None
    for li in order:
        h = LEVELS[li]
        if h == 1:
            qa = q16 * jnp.exp2(log_a).astype(BF16)
            kb = k16
        else:
            f16 = jnp.exp2(_level_exponent(d, h, log_a, b, row4)).astype(BF16)
            qa = q16 * f16
            kb = k16 * f16
        part = _dot_nt(qa, _head_blocks(kb, hmask_ref))
        if pending is not None:
            attn = pending if attn is None else attn + pending
        yield
        pending = part * lev_ref[d, li]
    attn = attn + pending

    v = p_ref[0, :, o_v:o_v + G]
    v16 = v.astype(BF16)
    st = st_ref[d]
    b_end = _last_row(b, d)
    q_state = q16 * jnp.exp2(b).astype(BF16)
    k_end = k16 * jnp.exp2(b_end - b).astype(BF16)
    o = _dot(attn.astype(BF16), _head_blocks(v16, hmask_ref)) + _dot_nt(q_state, st.astype(BF16))
    diag = _dot(q16 * k16, e_mat)
    upd = _dot_tn(v16, k_end)
    yield
    y_ref[0, :, o_y:o_y + G] = o + diag * v
    _update_state(st_ref, d, st, jnp.exp2(b_end), bd_state, upd)


def _mlstm_chain(d, p_ref, y_ref, s_ml, n_ml, m_ml, tri, incl_neg, trit, eye, seli, self_,
                 hmask_ref, bd256, bd256_16):
    C = CHUNK
    G = GROUP_W
    small = p_ref[0, :, O_SMALL:P_MIX]
    i_bc = _sel_right(small, seli)
    f_bc = _sel_right(small, self_)
    q16 = p_ref[0, :, O_ML_Q:O_ML_Q + G].astype(BF16)
    k = p_ref[0, :, O_ML_K:O_ML_K + G]
    qk = _dot_nt(q16, _head_blocks(k.astype(BF16), hmask_ref))
    yield
    ones_row = jnp.ones((8, C), BF16)
    b_col = _sel_left(tri, f_bc)
    b_row = _sel_left(ones_row, f_bc * trit)[0:1]
    i_row = _sel_left(ones_row, i_bc * eye)[0:1]
    yield
    w = b_col - b_row + i_row + incl_neg
    m_prev = m_ml[d, 0:1, :]
    inter = b_col + m_prev
    row_max = [jnp.max(w[:, h * HEAD_DIM:(h + 1) * HEAD_DIM], axis=-1, keepdims=True)
               for h in range(HEADS)]
    yield
    lane_head = lax.broadcasted_iota(jnp.int32, (C, G), 1) // HEAD_DIM
    m_t = None
    for h in range(HEADS):
        mh = jnp.maximum(row_max[h], inter[:, h * HEAD_DIM:h * HEAD_DIM + 1])
        mh = jnp.broadcast_to(mh, (C, G))
        m_t = mh if m_t is None else jnp.where(lane_head == h, mh, m_t)
    g_in = jnp.exp2(inter - m_t)
    s = (qk * jnp.exp2(w - m_t)).astype(BF16)
    v16 = p_ref[0, :, O_ML_V:O_ML_V + G].astype(BF16)
    c_st = s_ml[d]
    n_row = n_ml[d, 0:1, :]
    qn = (q16.astype(F32) * n_row).astype(BF16)
    num_a = _dot(s, _head_blocks(v16, hmask_ref))
    num_b = _dot(q16, c_st.astype(BF16))
    den_a = _dot(s, bd256_16)
    den_b = _dot(qn, bd256_16)
    m_new = _last_row(m_t, d)
    b_end = _last_row(b_col, d)
    a_state = jnp.exp2(b_end + m_prev - m_new)
    kw = k * jnp.exp2(b_end - b_col + i_bc - m_new)
    upd = _dot_tn(kw.astype(BF16), v16)
    yield
    num = num_a + g_in * num_b
    den = den_a + g_in * den_b
    y_ref[0, :, G:2 * G] = num / jnp.maximum(jnp.abs(den), jnp.exp2(-m_t))
    _update_state(s_ml, d, c_st, a_state, bd256, upd)
    n_ml[d] = jnp.broadcast_to(a_state * n_row + jnp.sum(kw, axis=0, keepdims=True), (8, G))
    m_ml[d] = jnp.broadcast_to(m_new, (8, G))


def _retention_chain(d, p_ref, y_ref, s_rt, log_g, dist, incl_neg, hmask_ref, bd256):
    C = CHUNK
    G = GROUP_W
    q16 = p_ref[0, :, O_RT_Q:O_RT_Q + G].astype(BF16)
    k16 = p_ref[0, :, O_RT_K:O_RT_K + G].astype(BF16)
    qk = _dot_nt(q16, _head_blocks(k16, hmask_ref))
    decay = jnp.exp2(log_g * dist + incl_neg)
    yield
    v16 = p_ref[0, :, O_RT_V:O_RT_V + G].astype(BF16)
    rowc = lax.broadcasted_iota(jnp.int32, (C, 1), 0).astype(F32)
    cnt = (rowc + 1.0) if d == 0 else (C - rowc)
    q_state = q16 * jnp.exp2(log_g * cnt).astype(BF16)
    k_end = k16 * jnp.exp2(log_g * (C - cnt)).astype(BF16)
    st = s_rt[d]
    o = _dot((qk * decay).astype(BF16), _head_blocks(v16, hmask_ref)) + _dot(q_state, st.astype(BF16))
    upd = _dot_tn(k_end, v16)
    yield
    y_ref[0, :, 2 * G:3 * G] = o
    _update_state(s_rt, d, st, jnp.exp2(log_g * float(C)), bd256, upd)


def _mix_kernel(pf_ref, pb_ref, rtg_ref,
                tri_ref, lev_ref, incl_ref, trit_ref, eye_ref, dist_ref,
                hmask_ref, bd256_ref, bdgl_ref, egl_ref, seli_ref, self_ref,
                yf_ref, yb_ref,
                s_hg, s_rt, s_gl, s_ml, n_ml, m_ml):
    C = CHUNK
    G = GROUP_W

    @pl.when(pl.program_id(1) == 0)
    def _():
        s_hg[...] = jnp.zeros_like(s_hg)
        s_rt[...] = jnp.zeros_like(s_rt)
        s_gl[...] = jnp.zeros_like(s_gl)
        s_ml[...] = jnp.zeros_like(s_ml)
        n_ml[...] = jnp.zeros_like(n_ml)
        m_ml[...] = jnp.zeros_like(m_ml)

    bd256 = bd256_ref[...]
    bd256_16 = bd256.astype(BF16)

    chains = []
    for d, (p_ref, y_ref) in enumerate(((pf_ref, yf_ref), (pb_ref, yb_ref))):
        incl_neg = incl_ref[d]
        tri = tri_ref[d]
        chains += [
            _vector_decay_chain(d, p_ref, O_HG_Q, O_HG_K + d * G, O_HG_V, O_HG_LF + d * G, G,
                                y_ref, 0, s_hg, tri, lev_ref, bd256_16, hmask_ref, bd256),
            _mlstm_chain(d, p_ref, y_ref, s_ml, n_ml, m_ml, tri, incl_neg, trit_ref[d], eye_ref[...],
                         seli_ref[d], self_ref[d], hmask_ref, bd256, bd256_16),
            _retention_chain(d, p_ref, y_ref, s_rt, rtg_ref[d:d + 1, :], dist_ref[...], incl_neg,
                             hmask_ref, bd256),
            _vector_decay_chain(d, p_ref, O_GL_Q, O_GL_K, O_GL_V, O_GL_LA + d * GLA_W, GLA_W,
                                y_ref, 3 * G, s_gl, tri, lev_ref, egl_ref[...], hmask_ref,
                                bdgl_ref[...]),
        ]
    _interleave(chains)


def _mix_call(B, NT, n_ctx_chunks):
    C = CHUNK
    NC = NT // C
    G = GROUP_W

    def fwd_map(b, j):
        return (b, j, 0)

    def bwd_map(b, j):
        return (b, jnp.where(j < n_ctx_chunks, n_ctx_chunks - 1 - j, NC - 1 - (j - n_ctx_chunks)), 0)

    consts = [jnp.asarray(_NP["tri"], BF16), jnp.asarray(_NP["lev"]), jnp.asarray(_NP["incl_neg"]),
              jnp.asarray(_NP["tri_t"]), jnp.asarray(_NP["eye"]), jnp.asarray(_NP["dist"]),
              jnp.asarray(_NP["hmask"], BF16),
              jnp.asarray(_NP["bd256"]), jnp.asarray(_NP["bd_gl"]), jnp.asarray(_NP["bd_gl"].T, BF16),
              jnp.asarray(_NP["sel_i"], BF16), jnp.asarray(_NP["sel_f"], BF16)]
    call = pl.pallas_call(
        _mix_kernel,
        grid=(B, NC),
        in_specs=[pl.BlockSpec((1, C, P_MIX), fwd_map), pl.BlockSpec((1, C, P_MIX), bwd_map),
                  _full_spec((8, G))] + [_full_spec(a.shape) for a in consts],
        out_specs=[pl.BlockSpec((1, C, D_MODEL), fwd_map), pl.BlockSpec((1, C, D_MODEL), bwd_map)],
        out_shape=[jax.ShapeDtypeStruct((B, NT, D_MODEL), F32)] * 2,
        scratch_shapes=[pltpu.VMEM((2, G, G), F32), pltpu.VMEM((2, G, G), F32),
                        pltpu.VMEM((2, G, GLA_W), F32), pltpu.VMEM((2, G, G), F32),
                        pltpu.VMEM((2, 8, G), F32), pltpu.VMEM((2, 8, G), F32)],
        compiler_params=pltpu.CompilerParams(dimension_semantics=("parallel", "arbitrary"),
                                             vmem_limit_bytes=VMEM_LIMIT),
        name="mix",
    )
    return lambda p_mix, rt_par: call(p_mix, p_mix, rt_par, *consts)


def _out_kernel(final, x_ref, yf_ref, yb_ref, gate_ref, mod_ref, gh_ref, bd_ref,
                wo_ref, w1_ref, w2_ref, gf_ref, o_ref):
    D = D_MODEL
    x = x_ref[0]
    bd = bd_ref[...]
    for g in range(D // GROUP_W):
        sl = slice(g * GROUP_W, (g + 1) * GROUP_W)
        y = yf_ref[0, :, sl] + yb_ref[0, :, sl]
        msq = _sel_right(y * y, bd) * (1.0 / HEAD_DIM)
        yn = y * lax.rsqrt(msq + RMS_EPS) * gh_ref[:, sl] * gate_ref[0, :, sl]
        part = _dot(yn.astype(BF16), wo_ref[sl, :])
        mixed = part if g == 0 else mixed + part
    g1 = mod_ref[0, 0, :, 2 * D:3 * D]
    sh2 = mod_ref[0, 0, :, 3 * D:4 * D]
    sc2 = mod_ref[0, 0, :, 4 * D:5 * D]
    g2 = mod_ref[0, 0, :, 5 * D:6 * D]
    x1 = x + g1 * mixed
    hin = ((x1 * _rms_scale(x1)) * (1.0 + sc2) + sh2).astype(BF16)
    hid = jnp.maximum(_dot(hin, w1_ref[...]), 0.0)
    ff = _dot((hid * hid).astype(BF16), w2_ref[...])
    x2 = x1 + g2 * ff
    if final:
        x2 = (x2 * _rms_scale(x2)) * gf_ref[...]
    o_ref[0] = x2


def _out(xall, yf, yb, gate, mods, g_heads, w_out, w_ff1, w_ff2, g_final, n_ctx_tiles, final):
    B, NT, D = xall.shape
    T = ROW_TILE
    t0 = n_ctx_tiles if final else 0
    n_tiles = NT // T - t0
    tok = lambda b, i: (b, i + t0, 0)
    bd = jnp.asarray(_NP["bd256"], BF16)
    wspec = lambda shape: pl.BlockSpec(shape, lambda b, i: (0, 0), pipeline_mode=pl.Buffered(1))
    return pl.pallas_call(
        functools.partial(_out_kernel, final),
        grid=(B, n_tiles),
        in_specs=[pl.BlockSpec((1, T, D), tok), pl.BlockSpec((1, T, D), tok),
                  pl.BlockSpec((1, T, D), tok), pl.BlockSpec((1, T, D), tok),
                  pl.BlockSpec((1, 1, 1, 6 * D),
                               lambda b, i: (b, jnp.where(i + t0 < n_ctx_tiles, 0, 1), 0, 0)),
                  pl.BlockSpec((1, D), lambda b, i: (0, 0)),
                  pl.BlockSpec((GROUP_W, GROUP_W), lambda b, i: (0, 0)),
                  wspec((D, D)), wspec((D, D_FF)), wspec((D_FF, D)),
                  pl.BlockSpec((1, D), lambda b, i: (0, 0))],
        out_specs=pl.BlockSpec((1, T, D), lambda b, i: (b, i, 0)),
        out_shape=jax.ShapeDtypeStruct((B, n_tiles * T, D), F32),
        compiler_params=pltpu.CompilerParams(dimension_semantics=("parallel", "parallel"),
                                             vmem_limit_bytes=VMEM_LIMIT),
        name="out_final" if final else "out",
    )(xall, yf, yb, gate, mods, g_heads.reshape(1, D), bd, w_out, w_ff1, w_ff2,
      g_final.reshape(1, D))


_REF_LAYOUT = (('hg_q', 256), ('hg_f_fwd', 256), ('hg_f_bwd', 256), ('hg_i', 256), ('hg_g', 256),
               ('ml_q', 256), ('ml_k', 256), ('ml_v', 256), ('ml_if', 16), ('ml_o', 256),
               ('rt_q', 256), ('rt_k', 256), ('rt_v', 256), ('rt_g', 256),
               ('gl_q', 128), ('gl_k', 128), ('gl_v', 256), ('gl_a_fwd', 16), ('gl_a_bwd', 16),
               ('gl_g', 256))
_NEW_ORDER = ('hg_q', 'hg_f_fwd', 'hg_f_bwd', 'hg_i', 'ml_q', 'ml_k', 'ml_v', 'rt_q', 'rt_k', 'rt_v',
              'gl_q', 'gl_k', 'gl_v', 'ml_if', 'gl_a_fwd', 'gl_a_bwd', 'PAD', 'hg_g', 'ml_o', 'rt_g', 'gl_g')


def _column_index():
    starts, off = {}, 0
    for name, w in _REF_LAYOUT:
        starts[name] = (off, w)
        off += w
    idx = []
    for name in _NEW_ORDER:
        if name == 'PAD':
            idx += [-1] * (128 - 48)
        else:
            a, w = starts[name]
            idx += list(range(a, a + w))
    idx = np.asarray(idx)
    assert idx.shape[0] == IN_PAD
    return np.where(idx < 0, 0, idx), (idx >= 0)


_COL_IDX, _COL_VALID = _column_index()


def _rope_tables(n_ctx, n_lat):
    n = jnp.arange(n_lat)
    r = (n // GRID_W).astype(F32)
    col = (n % GRID_W).astype(F32)
    n_freq = HEAD_DIM // 4
    inv = ROPE_BASE ** (-jnp.arange(n_freq, dtype=F32) / n_freq)
    ar = r[:, None] * inv[None, :]
    ac = col[:, None] * inv[None, :]
    cos = jnp.concatenate([jnp.cos(ar), jnp.cos(ar), jnp.cos(ac), jnp.cos(ac)], axis=-1)
    sin = jnp.concatenate([-jnp.sin(ar), jnp.sin(ar), -jnp.sin(ac), jnp.sin(ac)], axis=-1)
    cos = jnp.concatenate([jnp.ones((n_ctx, HEAD_DIM), F32), cos], axis=0)
    sin = jnp.concatenate([jnp.zeros((n_ctx, HEAD_DIM), F32), sin], axis=0)
    return jnp.tile(cos, (1, HEADS)), jnp.tile(sin, (1, HEADS))


def kernel(x, c, ctx, c_ctx, w_ada, b_ada, w_in, g_heads, hgrn_lb_logits, ml_gate_bias,
           rt_decay_logit, gla_w_a, gla_b_a, w_out, w_ff1, w_ff2, g_final):
    B, L, D = x.shape
    Lc = ctx.shape[1]
    depth = w_in.shape[0]
    assert D == D_MODEL and L % ROW_TILE == 0 and Lc % ROW_TILE == 0 and L % GRID_W == 0
    n_ctx_tiles = Lc // ROW_TILE
    n_ctx_chunks = Lc // CHUNK

    xall = jnp.concatenate([ctx, x], axis=1)
    cos_t, sin_t = _rope_tables(Lc, L)
    mix = _mix_call(B, Lc + L, n_ctx_chunks)

    n_c = -(-(B + 1) // 8) * 8
    cvecs = jnp.zeros((n_c, D), F32).at[:B].set(c).at[B].set(c_ctx)

    sm = jax.nn.softmax(hgrn_lb_logits.astype(F32), axis=0)
    lb_all = jnp.maximum(jnp.cumsum(sm, axis=0) - sm[:1], 0.0)

    col_valid = jnp.asarray(_COL_VALID, F32)[None, :]
    out = None
    for layer in range(depth):
        final = layer == depth - 1
        mod = _ada(cvecs, w_ada[layer], b_ada[layer])
        mods = jnp.stack([jnp.broadcast_to(mod[B], (B, 6 * D)), mod[:B]], axis=1)[:, :, None, :]

        w_in_p = (w_in[layer][:, _COL_IDX] * col_valid).astype(BF16)
        lb = lb_all[layer]
        hg_par = jnp.stack([jnp.log(lb[0]) * LOG2E, jnp.log1p(-lb[0]) * LOG2E, 1.0 - lb[0],
                            jnp.log(lb[1]) * LOG2E, jnp.log1p(-lb[1]) * LOG2E, 1.0 - lb[1],
                            jnp.zeros_like(lb[0]), jnp.zeros_like(lb[0])], axis=0)
        ml_bias = jnp.zeros((1, 128), F32).at[0, :16].set(ml_gate_bias[layer].astype(F32).reshape(16))
        wa = jnp.zeros((2, 128, GLA_W), F32)
        wa = wa.at[0, 16:32].set(gla_w_a[layer, 0].astype(F32)).at[1, 32:48].set(gla_w_a[layer, 1].astype(F32))
        wa_hi = wa.astype(BF16)
        wa_lo = (wa - wa_hi.astype(F32)).astype(BF16)
        ba = jnp.zeros((8, GLA_W), F32).at[:2].set(gla_b_a[layer].astype(F32))
        p_mix, gate = _inp(xall, mods, w_in_p, cos_t, sin_t, hg_par, ml_bias, wa_hi, wa_lo, ba,
                           n_ctx_tiles)

        rt_par = jnp.zeros((8, GROUP_W), F32).at[:2].set(
            jnp.repeat(jax.nn.log_sigmoid(rt_decay_logit[layer].astype(F32)) * LOG2E, HEAD_DIM, axis=-1))
        yf, yb = mix(p_mix, rt_par)

        res = _out(xall, yf, yb, gate, mods, g_heads[layer], w_out[layer].astype(BF16),
                   w_ff1[layer].astype(BF16), w_ff2[layer].astype(BF16), g_final,
                   n_ctx_tiles, final)
        if final:
            out = res
        else:
            xall = res
    return out
```

```python
import functools

import numpy as np
import jax
import jax.numpy as jnp
from jax import lax
from jax.experimental import pallas as pl
from jax.experimental.pallas import tpu as pltpu

F32 = jnp.float32
BF16 = jnp.bfloat16

D_MODEL = 1024
GROUP_W = 256
HEADS = 4
HEAD_DIM = 64
GLA_DK = 32
GLA_W = HEADS * GLA_DK
GLA_TAU = 16.0
D_FF = 4 * D_MODEL
GRID_W = 64
ROPE_BASE = 10000.0
RMS_EPS = 1e-6

CHUNK = 64
ROW_TILE = 256
LEVELS = (32, 16, 8, 4, 2, 1)

W_SMALL = 3072
W_GATES = 3200
IN_PAD = W_GATES + D_MODEL

O_HG_Q, O_HG_V = 0, 256
O_ML_Q, O_ML_K, O_ML_V = 512, 768, 1024
O_RT_Q, O_RT_K, O_RT_V = 1280, 1536, 1792
O_GL_Q, O_GL_K, O_GL_V = 2048, 2176, 2304
PS_W = 2560
G_HG_LF, G_GL_LA, G_SMALL = 0, 256, 384
PG_W = 512

VMEM_LIMIT = 56 * 1024 * 1024

NEG_INF = float("-inf")
LOG2E = 1.4426950408889634


def _dot(a, b):
    return jnp.dot(a, b, preferred_element_type=F32)


def _dot_nt(a, b):
    return lax.dot_general(a, b, (((1,), (1,)), ((), ())), preferred_element_type=F32)


def _dot_tn(a, b):
    return lax.dot_general(a, b, (((0,), (0,)), ((), ())), preferred_element_type=F32)


def _split2(a):
    hi = a.astype(BF16)
    return hi, (a - hi.astype(F32)).astype(BF16)


def _sel_left(m, a):
    hi, lo = _split2(a)
    return _dot(m, hi) + _dot(m, lo)


def _sel_right(a, m):
    hi, lo = _split2(a)
    return _dot(hi, m) + _dot(lo, m)


def _neg_abs(x):
    return pltpu.bitcast(pltpu.bitcast(x, jnp.uint32) | jnp.uint32(0x80000000), F32)


def _log2_sigmoid(z2):
    return jnp.minimum(z2, 0.0) - jnp.log2(1.0 + jnp.exp2(_neg_abs(z2)))


def _sigmoid(z):
    return 1.0 / (1.0 + jnp.exp(-z))


def _silu(z):
    return z * _sigmoid(z)


def _rms_scale(x):
    return lax.rsqrt(jnp.mean(x * x, axis=-1, keepdims=True) + RMS_EPS)


def _np_consts():
    C = CHUNK
    t = np.arange(C)
    T, U = np.meshgrid(t, t, indexing="ij")
    incl = [U <= T, U >= T]

    tri = np.stack([incl[d].astype(np.float32) for d in range(2)])
    lev = np.zeros((2, len(LEVELS), C, 4 * C), np.float32)
    for d in range(2):
        for li, h in enumerate(LEVELS):
            same = (T // (2 * h)) == (U // (2 * h))
            if d == 0:
                pair = same & (T % (2 * h) >= h) & (U % (2 * h) < h)
            else:
                pair = same & (T % (2 * h) < h) & (U % (2 * h) >= h)
            lev[d, li] = np.tile(pair.astype(np.float32), (1, 4))

    incl_neg = np.stack([np.tile(np.where(incl[d], 0.0, NEG_INF).astype(np.float32), (1, 4))
                         for d in range(2)])
    tri_t = np.stack([np.tile(incl[d].T.astype(np.float32), (1, 4)) for d in range(2)])
    eye = np.tile(np.eye(C, dtype=np.float32), (1, 4))
    dist = np.tile(np.abs(T - U).astype(np.float32), (1, 4))

    head_of_row = np.arange(4 * C) // C
    head_of_lane = np.arange(GROUP_W) // HEAD_DIM
    parity = np.where(head_of_row < 2, head_of_row, -1)
    hmask = np.concatenate(
        [(head_of_lane % 2)[None, :] == parity[:, None],
         head_of_row[:, None] == (np.arange(GLA_W) // GLA_DK)[None, :]], axis=1).astype(np.float32)
    bd256 = ((np.arange(GROUP_W) // HEAD_DIM)[:, None]
             == (np.arange(GROUP_W) // HEAD_DIM)[None, :]).astype(np.float32)
    bd_gl = ((np.arange(GROUP_W) // HEAD_DIM)[:, None]
             == (np.arange(GLA_W) // GLA_DK)[None, :]).astype(np.float32)

    sel_i = np.zeros((2, 128, GROUP_W), np.float32)
    sel_f = np.zeros((2, 128, GROUP_W), np.float32)
    for d in range(2):
        for h in range(HEADS):
            sel_i[d, d * 8 + h, h * HEAD_DIM:(h + 1) * HEAD_DIM] = 1.0
            sel_f[d, d * 8 + 4 + h, h * HEAD_DIM:(h + 1) * HEAD_DIM] = 1.0
    is_f = np.zeros((1, 128), np.float32)
    is_f[0, 4:8] = 1.0
    is_f[0, 12:16] = 1.0
    return dict(tri=tri, lev=lev, incl_neg=incl_neg, tri_t=tri_t, eye=eye, dist=dist,
                hmask=hmask, bd256=bd256, bd_gl=bd_gl,
                sel_i=sel_i, sel_f=sel_f, is_f=is_f)


_NP = _np_consts()


def _full_spec(shape):
    n = len(shape)
    return pl.BlockSpec(tuple(shape), lambda *_: (0,) * n)


def _ada_kernel(c_ref, w_ref, b_ref, o_ref):
    act = _silu(c_ref[...]).astype(BF16)
    o_ref[...] = _dot(act, w_ref[...].astype(BF16)) + b_ref[...]


def _ada(cvecs, w, b):
    rows = cvecs.shape[0]
    n = w.shape[1]
    tn = 1024
    return pl.pallas_call(
        _ada_kernel,
        grid=(n // tn,),
        in_specs=[pl.BlockSpec((rows, D_MODEL), lambda i: (0, 0)),
                  pl.BlockSpec((D_MODEL, tn), lambda i: (0, i)),
                  pl.BlockSpec((1, tn), lambda i: (0, i))],
        out_specs=pl.BlockSpec((rows, tn), lambda i: (0, i)),
        out_shape=jax.ShapeDtypeStruct((rows, n), F32),
        compiler_params=pltpu.CompilerParams(dimension_semantics=("arbitrary",),
                                             vmem_limit_bytes=VMEM_LIMIT),
        name="ada",
    )(cvecs, w, b.reshape(1, n))


def _rope(x, cos, sin):
    lane = lax.broadcasted_iota(jnp.int32, x.shape, 1)
    low = (lane % 32) < 16
    partner = jnp.where(low, pltpu.roll(x, GROUP_W - 16, 1), pltpu.roll(x, 16, 1))
    return x * cos + partner * sin


def _inp_kernel(x_ref, mod_ref, w_ref, cos_ref, sin_ref, hg_ref, mlb_ref, isf_ref,
                wa_hi_ref, wa_lo_ref, ba_ref, ps_ref, pk_ref, pg_ref, gate_ref):
    x = x_ref[0]
    sh = mod_ref[0, 0, :, 0:D_MODEL]
    sc = mod_ref[0, 0, :, D_MODEL:2 * D_MODEL]
    xn = ((x * _rms_scale(x)) * (1.0 + sc) + sh).astype(BF16)

    def proj(a, b):
        return _dot(xn, w_ref[:, a:b])

    G = GROUP_W

    def put(off, width, fn=None):
        def post(res):
            ps_ref[0, :, off:off + width] = (res if fn is None else fn(res)).astype(BF16)
        return post

    def hgrn_decay(d):
        def post(z):
            log_lb = hg_ref[3 * d + 0:3 * d + 1, :]
            log_1mlb = hg_ref[3 * d + 1:3 * d + 2, :]
            one_m_lb = hg_ref[3 * d + 2:3 * d + 3, :]
            z2 = z * LOG2E
            ls = _log2_sigmoid(z2)
            other = log_1mlb + ls
            pg_ref[d, 0, :, G_HG_LF:G_HG_LF + G] = (
                jnp.maximum(log_lb, other) + jnp.log2(1.0 + jnp.exp2(_neg_abs(log_lb - other))))
            pk_ref[d, 0] = (one_m_lb * jnp.exp2(ls - z2)).astype(BF16)
        return post

    def narrow(small):
        small_hi, small_lo = _split2(small)
        pre2 = (small + mlb_ref[...]) * LOG2E
        gate_logs = jnp.where(isf_ref[...] > 0.5, _log2_sigmoid(pre2), pre2)
        for d in range(2):
            za = (_dot(small_hi, wa_hi_ref[d]) + _dot(small_lo, wa_hi_ref[d])
                  + _dot(small_hi, wa_lo_ref[d]) + ba_ref[d:d + 1, :])
            pg_ref[d, 0, :, G_GL_LA:G_GL_LA + GLA_W] = _log2_sigmoid(za * LOG2E) * (1.0 / GLA_TAU)
            pg_ref[d, 0, :, G_SMALL:PG_W] = gate_logs

    def gates(g):
        gate_ref[0, :, 0:2 * G] = _sigmoid(g[:, 0:2 * G]).astype(BF16)
        gate_ref[0, :, 2 * G:4 * G] = _silu(g[:, 2 * G:4 * G]).astype(BF16)

    def rope(scale):
        return lambda r: _rope(r * scale if scale != 1.0 else r, cos_ref[...], sin_ref[...])

    o = 10 * G
    stages = [
        (0, G, put(O_HG_Q, G, _silu)),
        (G, 2 * G, hgrn_decay(0)),
        (2 * G, 3 * G, hgrn_decay(1)),
        (3 * G, 5 * G, put(O_HG_V, 2 * G)),
        (5 * G, 6 * G, put(O_ML_K, G, lambda r: r * (HEAD_DIM ** -0.5))),
        (6 * G, 7 * G, put(O_ML_V, G)),
        (7 * G, 8 * G, put(O_RT_Q, G, rope(1.0))),
        (8 * G, 9 * G, put(O_RT_K, G, rope(HEAD_DIM ** -0.5))),
        (9 * G, o + GLA_W, put(O_RT_V, G + GLA_W)),
        (o + GLA_W, o + 2 * GLA_W, put(O_GL_K, GLA_W, lambda r: r * (GLA_DK ** -0.5))),
        (o + 2 * GLA_W, W_SMALL, put(O_GL_V, G)),
        (W_SMALL, W_GATES, narrow),
        (W_GATES, IN_PAD, gates),
    ]
    pending = None
    for a, b, post in stages:
        res = proj(a, b)
        if pending is not None:
            pending[1](pending[0])
        pending = (res, post)
    pending[1](pending[0])


def _inp(xall, mods, w_in_p, cos_t, sin_t, hg_par, ml_bias, wa_hi, wa_lo, ba, n_ctx_tiles):
    B, NT, _ = xall.shape
    T = ROW_TILE
    params = [hg_par, ml_bias, jnp.asarray(_NP["is_f"]), wa_hi, wa_lo, ba]
    return pl.pallas_call(
        _inp_kernel,
        grid=(B, NT // T),
        in_specs=[pl.BlockSpec((1, T, D_MODEL), lambda b, i: (b, i, 0)),
                  pl.BlockSpec((1, 1, 1, 6 * D_MODEL),
                               lambda b, i: (b, jnp.where(i < n_ctx_tiles, 0, 1), 0, 0)),
                  pl.BlockSpec((D_MODEL, IN_PAD), lambda b, i: (0, 0)),
                  pl.BlockSpec((T, GROUP_W), lambda b, i: (i, 0)),
                  pl.BlockSpec((T, GROUP_W), lambda b, i: (i, 0))]
        + [_full_spec(a.shape) for a in params],
        out_specs=[pl.BlockSpec((1, T, PS_W), lambda b, i: (b, i, 0)),
                   pl.BlockSpec((2, 1, T, GROUP_W), lambda b, i: (0, b, i, 0)),
                   pl.BlockSpec((2, 1, T, PG_W), lambda b, i: (0, b, i, 0)),
                   pl.BlockSpec((1, T, D_MODEL), lambda b, i: (b, i, 0))],
        out_shape=[jax.ShapeDtypeStruct((B, NT, PS_W), BF16),
                   jax.ShapeDtypeStruct((2, B, NT, GROUP_W), BF16),
                   jax.ShapeDtypeStruct((2, B, NT, PG_W), F32),
                   jax.ShapeDtypeStruct((B, NT, D_MODEL), BF16)],
        compiler_params=pltpu.CompilerParams(dimension_semantics=("parallel", "parallel"),
                                             vmem_limit_bytes=VMEM_LIMIT),
        name="inp",
    )(xall, mods, w_in_p, cos_t, sin_t, *params)


LANES = 128


def _head_blocks(a16, hmask_ref):
    C, W = a16.shape
    if W == LANES:
        return jnp.concatenate([a16, a16, a16, a16], axis=0) * hmask_ref[:, GROUP_W:GROUP_W + W]
    even = a16 * hmask_ref[0:C, 0:W]
    odd = a16 * hmask_ref[C:2 * C, 0:W]
    zero = jnp.zeros((C, LANES), a16.dtype)
    return jnp.concatenate([
        jnp.concatenate([even[:, 0:LANES], zero], axis=1),
        jnp.concatenate([odd[:, 0:LANES], zero], axis=1),
        jnp.concatenate([zero, even[:, LANES:W]], axis=1),
        jnp.concatenate([zero, odd[:, LANES:W]], axis=1)], axis=0)


def _last_row(a, d):
    return a[CHUNK - 1:CHUNK, :] if d == 0 else a[0:1, :]


def _level_exponent(d, h, log_a, b, row4):
    C, W = log_a.shape
    if h >= 4:
        b3 = b.reshape(C // (2 * h), 2 * h, W)
        r = h - 1 if d == 0 else h
        ref = jnp.broadcast_to(b3[:, r:r + 1, :], b3.shape).reshape(C, W)
        return _neg_abs(b - ref)
    up = pltpu.roll(log_a, C - 1, 0)
    dn = pltpu.roll(log_a, 1, 0)
    if d == 0:
        return jnp.where(row4 == 0, up, jnp.where(row4 == 1, 0.0,
                                                  log_a + jnp.where(row4 == 3, dn, 0.0)))
    return jnp.where(row4 == 0, log_a + up,
                     jnp.where(row4 == 1, log_a, jnp.where(row4 == 2, 0.0, dn)))


def _interleave(chains):
    live = list(chains)
    while live:
        for g in list(live):
            try:
                next(g)
            except StopIteration:
                live.remove(g)


def _update_state(st_ref, d, st, decay_row, bd_state, upd):
    rows = st.shape[0] // HEADS
    per_head = st.shape[1] // HEADS
    for h in range(HEADS):
        j = (h * per_head) // LANES
        rs = slice(h * rows, (h + 1) * rows)
        cs = slice(j * LANES, (j + 1) * LANES)
        st_ref[d, rs, cs] = decay_row[:, cs] * st[rs, cs] + bd_state[rs, cs] * upd[rs, cs]


def _vector_decay_chain(d, q16, k16, v16, log_a, y_ref, o_y, st_ref, tri, lev_ref, eye,
                        hmask_ref, bd_state):
    C, W = log_a.shape
    G = GROUP_W
    b = _sel_left(tri, log_a)
    row4 = lax.broadcasted_iota(jnp.int32, (C, W), 0) & 3
    order = sorted(range(len(LEVELS)), key=lambda li: LEVELS[li] >= 4)
    attn = None
    pending = None
    for li in order:
        h = LEVELS[li]
        if h == 1:
            qa = jnp.concatenate([q16 * jnp.exp2(log_a).astype(BF16), q16], axis=0)
            kb = k16
        else:
            f16 = jnp.exp2(_level_exponent(d, h, log_a, b, row4)).astype(BF16)
            qa = q16 * f16
            kb = k16 * f16
        part = _dot_nt(qa, _head_blocks(kb, hmask_ref))
        if pending is not None:
            attn = pending if attn is None else attn + pending
        yield
        pending = part * lev_ref[d, li] if h > 1 else part[0:C] * lev_ref[d, li] + part[C:2 * C] * eye
    attn = attn + pending

    st = st_ref[d]
    b_end = _last_row(b, d)
    q_state = q16 * jnp.exp2(b).astype(BF16)
    k_end = k16 * jnp.exp2(b_end - b).astype(BF16)
    o = _dot(attn.astype(BF16), _head_blocks(v16, hmask_ref)) + _dot_nt(q_state, st.astype(BF16))
    upd = _dot_tn(v16, k_end)
    yield
    y_ref[0, :, o_y:o_y + G] = o
    _update_state(st_ref, d, st, jnp.exp2(b_end), bd_state, upd)


def _mlstm_chain(d, q16, k16, v16, small, y_ref, s_ml, n_ml, m_ml, tri, incl_neg, trit, eye,
                 seli, self_, hmask_ref, bd256, bd256_16):
    C = CHUNK
    G = GROUP_W
    i_bc = _sel_right(small, seli)
    f_bc = _sel_right(small, self_)
    qk = _dot_nt(q16, _head_blocks(k16, hmask_ref))
    yield
    ones_row = jnp.ones((8, C), BF16)
    b_col = _sel_left(tri, f_bc)
    b_row = _sel_left(ones_row, f_bc * trit)[0:1]
    i_row = _sel_left(ones_row, i_bc * eye)[0:1]
    yield
    w = b_col - b_row + i_row + incl_neg
    m_prev = m_ml[d, 0:1, :]
    inter = b_col + m_prev
    row_max = [jnp.max(w[:, h * HEAD_DIM:(h + 1) * HEAD_DIM], axis=-1, keepdims=True)
               for h in range(HEADS)]
    yield
    lane_head = lax.broadcasted_iota(jnp.int32, (C, G), 1) // HEAD_DIM
    m_t = None
    for h in range(HEADS):
        mh = jnp.maximum(row_max[h], inter[:, h * HEAD_DIM:h * HEAD_DIM + 1])
        mh = jnp.broadcast_to(mh, (C, G))
        m_t = mh if m_t is None else jnp.where(lane_head == h, mh, m_t)
    g_in = jnp.exp2(inter - m_t)
    s = (qk * jnp.exp2(w - m_t)).astype(BF16)
    c_st = s_ml[d]
    n_row = n_ml[d, 0:1, :]
    qn = (q16.astype(F32) * n_row).astype(BF16)
    num_a = _dot(s, _head_blocks(v16, hmask_ref))
    num_b = _dot(q16, c_st.astype(BF16))
    den_ab = _dot(jnp.concatenate([s, qn], axis=0), bd256_16)
    m_new = _last_row(m_t, d)
    b_end = _last_row(b_col, d)
    a_state = jnp.exp2(b_end + m_prev - m_new)
    kw = k16.astype(F32) * jnp.exp2(b_end - b_col + i_bc - m_new)
    upd = _dot_tn(kw.astype(BF16), v16)
    yield
    num = num_a + g_in * num_b
    den = den_ab[0:C] + g_in * den_ab[C:2 * C]
    y_ref[0, :, G:2 * G] = num / jnp.maximum(jnp.abs(den), jnp.exp2(-m_t))
    _update_state(s_ml, d, c_st, a_state, bd256, upd)
    n_ml[d] = jnp.broadcast_to(a_state * n_row + jnp.sum(kw, axis=0, keepdims=True), (8, G))
    m_ml[d] = jnp.broadcast_to(m_new, (8, G))


def _retention_chain(d, q16, k16, v16, y_ref, s_rt, log_g, dist, incl_neg, hmask_ref, bd256):
    C = CHUNK
    G = GROUP_W
    qk = _dot_nt(q16, _head_blocks(k16, hmask_ref))
    decay = jnp.exp2(log_g * dist + incl_neg)
    yield
    rowc = lax.broadcasted_iota(jnp.int32, (C, 1), 0).astype(F32)
    cnt = (rowc + 1.0) if d == 0 else (C - rowc)
    q_state = q16 * jnp.exp2(log_g * cnt).astype(BF16)
    k_end = k16 * jnp.exp2(log_g * (C - cnt)).astype(BF16)
    st = s_rt[d]
    o = _dot((qk * decay).astype(BF16), _head_blocks(v16, hmask_ref)) + _dot(q_state, st.astype(BF16))
    upd = _dot_tn(k_end, v16)
    yield
    y_ref[0, :, 2 * G:3 * G] = o
    _update_state(s_rt, d, st, jnp.exp2(log_g * float(C)), bd256, upd)


def _mix_kernel(psf_ref, psb_ref, pkf_ref, pkb_ref, pgf_ref, pgb_ref, rtg_ref,
                tri_ref, lev_ref, incl_ref, trit_ref, eye_ref, dist_ref,
                hmask_ref, bd256_ref, bdgl_ref, seli_ref, self_ref,
                yf_ref, yb_ref,
                s_hg, s_rt, s_gl, s_ml, n_ml, m_ml):
    C = CHUNK
    G = GROUP_W

    @pl.when(pl.program_id(1) == 0)
    def _():
        s_hg[...] = jnp.zeros_like(s_hg)
        s_rt[...] = jnp.zeros_like(s_rt)
        s_gl[...] = jnp.zeros_like(s_gl)
        s_ml[...] = jnp.zeros_like(s_ml)
        n_ml[...] = jnp.zeros_like(n_ml)
        m_ml[...] = jnp.zeros_like(m_ml)

    bd256 = bd256_ref[...]
    bd256_16 = bd256.astype(BF16)

    chains = []
    for d, (ps_ref, pk_ref, pg_ref, y_ref) in enumerate(((psf_ref, pkf_ref, pgf_ref, yf_ref),
                                                         (psb_ref, pkb_ref, pgb_ref, yb_ref))):
        incl_neg = incl_ref[d]
        tri = tri_ref[d]
        eye = eye_ref[...]

        def ps(off, width, ps_ref=ps_ref):
            return ps_ref[0, :, off:off + width]

        def pg(off, width, pg_ref=pg_ref):
            return pg_ref[0, 0, :, off:off + width]

        chains += [
            _vector_decay_chain(d, ps(O_HG_Q, G), pk_ref[0, 0], ps(O_HG_V, G), pg(G_HG_LF, G),
                                y_ref, 0, s_hg, tri, lev_ref, eye, hmask_ref, bd256),
            _mlstm_chain(d, ps(O_ML_Q, G), ps(O_ML_K, G), ps(O_ML_V, G), pg(G_SMALL, PG_W - G_SMALL),
                         y_ref, s_ml, n_ml, m_ml, tri, incl_neg, trit_ref[d], eye,
                         seli_ref[d], self_ref[d], hmask_ref, bd256, bd256_16),
            _retention_chain(d, ps(O_RT_Q, G), ps(O_RT_K, G), ps(O_RT_V, G), y_ref, s_rt,
                             rtg_ref[d:d + 1, :], dist_ref[...], incl_neg, hmask_ref, bd256),
            _vector_decay_chain(d, ps(O_GL_Q, GLA_W), ps(O_GL_K, GLA_W), ps(O_GL_V, G),
                                pg(G_GL_LA, GLA_W), y_ref, 3 * G, s_gl, tri, lev_ref, eye,
                                hmask_ref, bdgl_ref[...]),
        ]
    _interleave(chains)


def _mix_call(B, NT, n_ctx_chunks):
    C = CHUNK
    NC = NT // C
    G = GROUP_W

    def bwd_chunk(j):
        return jnp.where(j < n_ctx_chunks, n_ctx_chunks - 1 - j, NC - 1 - (j - n_ctx_chunks))

    def fwd_map(b, j):
        return (b, j, 0)

    def bwd_map(b, j):
        return (b, bwd_chunk(j), 0)

    consts = [jnp.asarray(_NP["tri"], BF16), jnp.asarray(_NP["lev"]), jnp.asarray(_NP["incl_neg"]),
              jnp.asarray(_NP["tri_t"]), jnp.asarray(_NP["eye"]), jnp.asarray(_NP["dist"]),
              jnp.asarray(_NP["hmask"], BF16),
              jnp.asarray(_NP["bd256"]), jnp.asarray(_NP["bd_gl"]),
              jnp.asarray(_NP["sel_i"], BF16), jnp.asarray(_NP["sel_f"], BF16)]
    call = pl.pallas_call(
        _mix_kernel,
        grid=(B, NC),
        in_specs=[pl.BlockSpec((1, C, PS_W), fwd_map), pl.BlockSpec((1, C, PS_W), bwd_map),
                  pl.BlockSpec((1, 1, C, G), lambda b, j: (0, b, j, 0)),
                  pl.BlockSpec((1, 1, C, G), lambda b, j: (1, b, bwd_chunk(j), 0)),
                  pl.BlockSpec((1, 1, C, PG_W), lambda b, j: (0, b, j, 0)),
                  pl.BlockSpec((1, 1, C, PG_W), lambda b, j: (1, b, bwd_chunk(j), 0)),
                  _full_spec((8, G))] + [_full_spec(a.shape) for a in consts],
        out_specs=[pl.BlockSpec((1, C, D_MODEL), fwd_map), pl.BlockSpec((1, C, D_MODEL), bwd_map)],
        out_shape=[jax.ShapeDtypeStruct((B, NT, D_MODEL), F32)] * 2,
        scratch_shapes=[pltpu.VMEM((2, G, G), F32), pltpu.VMEM((2, G, G), F32),
                        pltpu.VMEM((2, G, GLA_W), F32), pltpu.VMEM((2, G, G), F32),
                        pltpu.VMEM((2, 8, G), F32), pltpu.VMEM((2, 8, G), F32)],
        compiler_params=pltpu.CompilerParams(dimension_semantics=("parallel", "arbitrary"),
                                             vmem_limit_bytes=VMEM_LIMIT),
        name="mix",
    )
    return lambda ps, pk, pg, rt_par: call(ps, ps, pk, pk, pg, pg, rt_par, *consts)


def _out_kernel(final, x_ref, yf_ref, yb_ref, gate_ref, mod_ref, gh_ref, bd_ref,
                wo_ref, w1_ref, w2_ref, gf_ref, o_ref):
    D = D_MODEL
    x = x_ref[0]
    bd = bd_ref[...]
    for g in range(D // GROUP_W):
        sl = slice(g * GROUP_W, (g + 1) * GROUP_W)
        y = yf_ref[0, :, sl] + yb_ref[0, :, sl]
        msq = _sel_right(y * y, bd) * (1.0 / HEAD_DIM)
        yn = y * lax.rsqrt(msq + RMS_EPS) * gh_ref[:, sl] * gate_ref[0, :, sl].astype(F32)
        part = _dot(yn.astype(BF16), wo_ref[sl, :])
        mixed = part if g == 0 else mixed + part
    g1 = mod_ref[0, 0, :, 2 * D:3 * D]
    sh2 = mod_ref[0, 0, :, 3 * D:4 * D]
    sc2 = mod_ref[0, 0, :, 4 * D:5 * D]
    g2 = mod_ref[0, 0, :, 5 * D:6 * D]
    x1 = x + g1 * mixed
    hin = ((x1 * _rms_scale(x1)) * (1.0 + sc2) + sh2).astype(BF16)
    hid = jnp.maximum(_dot(hin, w1_ref[...]), 0.0)
    ff = _dot((hid * hid).astype(BF16), w2_ref[...])
    x2 = x1 + g2 * ff
    if final:
        x2 = (x2 * _rms_scale(x2)) * gf_ref[...]
    o_ref[0] = x2


def _out(xall, yf, yb, gate, mods, g_heads, w_out, w_ff1, w_ff2, g_final, n_ctx_tiles, final):
    B, NT, D = xall.shape
    T = ROW_TILE
    t0 = n_ctx_tiles if final else 0
    n_tiles = NT // T - t0
    tok = lambda b, i: (b, i + t0, 0)
    bd = jnp.asarray(_NP["bd256"], BF16)
    wspec = lambda shape: pl.BlockSpec(shape, lambda b, i: (0, 0), pipeline_mode=pl.Buffered(1))
    return pl.pallas_call(
        functools.partial(_out_kernel, final),
        grid=(B, n_tiles),
        in_specs=[pl.BlockSpec((1, T, D), tok), pl.BlockSpec((1, T, D), tok),
                  pl.BlockSpec((1, T, D), tok), pl.BlockSpec((1, T, D), tok),
                  pl.BlockSpec((1, 1, 1, 6 * D),
                               lambda b, i: (b, jnp.where(i + t0 < n_ctx_tiles, 0, 1), 0, 0)),
                  pl.BlockSpec((1, D), lambda b, i: (0, 0)),
                  pl.BlockSpec((GROUP_W, GROUP_W), lambda b, i: (0, 0)),
                  wspec((D, D)), wspec((D, D_FF)), wspec((D_FF, D)),
                  pl.BlockSpec((1, D), lambda b, i: (0, 0))],
        out_specs=pl.BlockSpec((1, T, D), lambda b, i: (b, i, 0)),
        out_shape=jax.ShapeDtypeStruct((B, n_tiles * T, D), F32),
        compiler_params=pltpu.CompilerParams(dimension_semantics=("parallel", "parallel"),
                                             vmem_limit_bytes=VMEM_LIMIT),
        name="out_final" if final else "out",
    )(xall, yf, yb, gate, mods, g_heads.reshape(1, D), bd, w_out, w_ff1, w_ff2,
      g_final.reshape(1, D))


_REF_LAYOUT = (('hg_q', 256), ('hg_f_fwd', 256), ('hg_f_bwd', 256), ('hg_i', 256), ('hg_g', 256),
               ('ml_q', 256), ('ml_k', 256), ('ml_v', 256), ('ml_if', 16), ('ml_o', 256),
               ('rt_q', 256), ('rt_k', 256), ('rt_v', 256), ('rt_g', 256),
               ('gl_q', 128), ('gl_k', 128), ('gl_v', 256), ('gl_a_fwd', 16), ('gl_a_bwd', 16),
               ('gl_g', 256))
_NEW_ORDER = ('hg_q', 'hg_f_fwd', 'hg_f_bwd', 'hg_i', 'ml_q', 'ml_k', 'ml_v', 'rt_q', 'rt_k', 'rt_v',
              'gl_q', 'gl_k', 'gl_v', 'ml_if', 'gl_a_fwd', 'gl_a_bwd', 'PAD', 'hg_g', 'ml_o', 'rt_g', 'gl_g')


def _column_index():
    starts, off = {}, 0
    for name, w in _REF_LAYOUT:
        starts[name] = (off, w)
        off += w
    idx = []
    for name in _NEW_ORDER:
        if name == 'PAD':
            idx += [-1] * (128 - 48)
        else:
            a, w = starts[name]
            idx += list(range(a, a + w))
    idx = np.asarray(idx)
    assert idx.shape[0] == IN_PAD
    return np.where(idx < 0, 0, idx), (idx >= 0)


_COL_IDX, _COL_VALID = _column_index()


def _rope_tables(n_ctx, n_lat):
    n = jnp.arange(n_lat)
    r = (n // GRID_W).astype(F32)
    col = (n % GRID_W).astype(F32)
    n_freq = HEAD_DIM // 4
    inv = ROPE_BASE ** (-jnp.arange(n_freq, dtype=F32) / n_freq)
    ar = r[:, None] * inv[None, :]
    ac = col[:, None] * inv[None, :]
    cos = jnp.concatenate([jnp.cos(ar), jnp.cos(ar), jnp.cos(ac), jnp.cos(ac)], axis=-1)
    sin = jnp.concatenate([-jnp.sin(ar), jnp.sin(ar), -jnp.sin(ac), jnp.sin(ac)], axis=-1)
    cos = jnp.concatenate([jnp.ones((n_ctx, HEAD_DIM), F32), cos], axis=0)
    sin = jnp.concatenate([jnp.zeros((n_ctx, HEAD_DIM), F32), sin], axis=0)
    return jnp.tile(cos, (1, HEADS)), jnp.tile(sin, (1, HEADS))


def kernel(x, c, ctx, c_ctx, w_ada, b_ada, w_in, g_heads, hgrn_lb_logits, ml_gate_bias,
           rt_decay_logit, gla_w_a, gla_b_a, w_out, w_ff1, w_ff2, g_final):
    B, L, D = x.shape
    Lc = ctx.shape[1]
    depth = w_in.shape[0]
    assert D == D_MODEL and L % ROW_TILE == 0 and Lc % ROW_TILE == 0 and L % GRID_W == 0
    n_ctx_tiles = Lc // ROW_TILE
    n_ctx_chunks = Lc // CHUNK

    xall = jnp.concatenate([ctx, x], axis=1)
    cos_t, sin_t = _rope_tables(Lc, L)
    mix = _mix_call(B, Lc + L, n_ctx_chunks)

    n_c = -(-(B + 1) // 8) * 8
    cvecs = jnp.zeros((n_c, D), F32).at[:B].set(c).at[B].set(c_ctx)

    sm = jax.nn.softmax(hgrn_lb_logits.astype(F32), axis=0)
    lb_all = jnp.maximum(jnp.cumsum(sm, axis=0) - sm[:1], 0.0)

    col_valid = jnp.asarray(_COL_VALID, F32)[None, :]
    out = None
    for layer in range(depth):
        final = layer == depth - 1
        mod = _ada(cvecs, w_ada[layer], b_ada[layer])
        mods = jnp.stack([jnp.broadcast_to(mod[B], (B, 6 * D)), mod[:B]], axis=1)[:, :, None, :]

        w_in_p = (w_in[layer][:, _COL_IDX] * col_valid).astype(BF16)
        lb = lb_all[layer]
        hg_par = jnp.stack([jnp.log(lb[0]) * LOG2E, jnp.log1p(-lb[0]) * LOG2E, 1.0 - lb[0],
                            jnp.log(lb[1]) * LOG2E, jnp.log1p(-lb[1]) * LOG2E, 1.0 - lb[1],
                            jnp.zeros_like(lb[0]), jnp.zeros_like(lb[0])], axis=0)
        ml_bias = jnp.zeros((1, 128), F32).at[0, :16].set(ml_gate_bias[layer].astype(F32).reshape(16))
        wa = jnp.zeros((2, 128, GLA_W), F32)
        wa = wa.at[0, 16:32].set(gla_w_a[layer, 0].astype(F32)).at[1, 32:48].set(gla_w_a[layer, 1].astype(F32))
        wa_hi = wa.astype(BF16)
        wa_lo = (wa - wa_hi.astype(F32)).astype(BF16)
        ba = jnp.zeros((8, GLA_W), F32).at[:2].set(gla_b_a[layer].astype(F32))
        ps, pk, pg, gate = _inp(xall, mods, w_in_p, cos_t, sin_t, hg_par, ml_bias, wa_hi, wa_lo, ba,
                                n_ctx_tiles)

        rt_par = jnp.zeros((8, GROUP_W), F32).at[:2].set(
            jnp.repeat(jax.nn.log_sigmoid(rt_decay_logit[layer].astype(F32)) * LOG2E, HEAD_DIM, axis=-1))
        yf, yb = mix(ps, pk, pg, rt_par)

        res = _out(xall, yf, yb, gate, mods, g_heads[layer], w_out[layer].astype(BF16),
                   w_ff1[layer].astype(BF16), w_ff2[layer].astype(BF16), g_final,
                   n_ctx_tiles, final)
        if final:
            out = res
        else:
            xall = res
    return out
```

```python
import functools

import numpy as np
import jax
import jax.numpy as jnp
from jax import lax
from jax.experimental import pallas as pl
from jax.experimental.pallas import tpu as pltpu

F32 = jnp.float32
BF16 = jnp.bfloat16

D_MODEL = 1024
GROUP_W = 256
HEADS = 4
HEAD_DIM = 64
GLA_DK = 32
GLA_W = HEADS * GLA_DK
GLA_TAU = 16.0
D_FF = 4 * D_MODEL
GRID_W = 64
ROPE_BASE = 10000.0
RMS_EPS = 1e-6

CHUNK = 64
STEP_CHUNKS = 4
ROW_TILE = 256
LEVELS = (32, 16, 8, 4, 2, 1)

W_SMALL = 3072
W_GATES = 3200
IN_PAD = W_GATES + D_MODEL

O_HG_Q, O_HG_V = 0, 256
O_ML_Q, O_ML_K, O_ML_V = 512, 768, 1024
O_RT_Q, O_RT_K, O_RT_V = 1280, 1536, 1792
O_GL_Q, O_GL_K, O_GL_V = 2048, 2176, 2304
PS_W = 2560
G_HG_LF, G_GL_LA, G_SMALL = 0, 256, 384
PG_W = 512

VMEM_LIMIT = 56 * 1024 * 1024

NEG_INF = float("-inf")
LOG2E = 1.4426950408889634


def _dot(a, b):
    return jnp.dot(a, b, preferred_element_type=F32)


def _dot_nt(a, b):
    return lax.dot_general(a, b, (((1,), (1,)), ((), ())), preferred_element_type=F32)


def _dot_tn(a, b):
    return lax.dot_general(a, b, (((0,), (0,)), ((), ())), preferred_element_type=F32)


def _split2(a):
    hi = a.astype(BF16)
    return hi, (a - hi.astype(F32)).astype(BF16)


def _sel_left(m, a):
    hi, lo = _split2(a)
    return _dot(m, hi) + _dot(m, lo)


def _sel_right(a, m):
    hi, lo = _split2(a)
    return _dot(hi, m) + _dot(lo, m)


def _neg_abs(x):
    return pltpu.bitcast(pltpu.bitcast(x, jnp.uint32) | jnp.uint32(0x80000000), F32)


def _log2_sigmoid(z2):
    return jnp.minimum(z2, 0.0) - jnp.log2(1.0 + jnp.exp2(_neg_abs(z2)))


def _sigmoid(z):
    return 1.0 / (1.0 + jnp.exp(-z))


def _silu(z):
    return z * _sigmoid(z)


def _rms_scale(x):
    return lax.rsqrt(jnp.mean(x * x, axis=-1, keepdims=True) + RMS_EPS)


def _np_consts():
    C = CHUNK
    t = np.arange(C)
    T, U = np.meshgrid(t, t, indexing="ij")
    incl = [U <= T, U >= T]

    tri = np.stack([incl[d].astype(np.float32) for d in range(2)])
    lev = np.zeros((2, len(LEVELS), C, 4 * C), np.float32)
    for d in range(2):
        for li, h in enumerate(LEVELS):
            same = (T // (2 * h)) == (U // (2 * h))
            if d == 0:
                pair = same & (T % (2 * h) >= h) & (U % (2 * h) < h)
            else:
                pair = same & (T % (2 * h) < h) & (U % (2 * h) >= h)
            lev[d, li] = np.tile(pair.astype(np.float32), (1, 4))

    incl_neg = np.stack([np.tile(np.where(incl[d], 0.0, NEG_INF).astype(np.float32), (1, 4))
                         for d in range(2)])
    tri_t = np.stack([np.tile(incl[d].T.astype(np.float32), (1, 4)) for d in range(2)])
    eye = np.tile(np.eye(C, dtype=np.float32), (1, 4))
    dist = np.tile(np.abs(T - U).astype(np.float32), (1, 4))

    head_of_row = np.arange(4 * C) // C
    head_of_lane = np.arange(GROUP_W) // HEAD_DIM
    parity = np.where(head_of_row < 2, head_of_row, -1)
    hmask = np.concatenate(
        [(head_of_lane % 2)[None, :] == parity[:, None],
         head_of_row[:, None] == (np.arange(GLA_W) // GLA_DK)[None, :]], axis=1).astype(np.float32)
    bd256 = ((np.arange(GROUP_W) // HEAD_DIM)[:, None]
             == (np.arange(GROUP_W) // HEAD_DIM)[None, :]).astype(np.float32)
    bd_gl = ((np.arange(GROUP_W) // HEAD_DIM)[:, None]
             == (np.arange(GLA_W) // GLA_DK)[None, :]).astype(np.float32)

    sel_i = np.zeros((2, 128, GROUP_W), np.float32)
    sel_f = np.zeros((2, 128, GROUP_W), np.float32)
    for d in range(2):
        for h in range(HEADS):
            sel_i[d, d * 8 + h, h * HEAD_DIM:(h + 1) * HEAD_DIM] = 1.0
            sel_f[d, d * 8 + 4 + h, h * HEAD_DIM:(h + 1) * HEAD_DIM] = 1.0
    is_f = np.zeros((1, 128), np.float32)
    is_f[0, 4:8] = 1.0
    is_f[0, 12:16] = 1.0
    return dict(tri=tri, lev=lev, incl_neg=incl_neg, tri_t=tri_t, eye=eye, dist=dist,
                hmask=hmask, bd256=bd256, bd_gl=bd_gl,
                sel_i=sel_i, sel_f=sel_f, is_f=is_f)


_NP = _np_consts()


def _full_spec(shape):
    n = len(shape)
    return pl.BlockSpec(tuple(shape), lambda *_: (0,) * n)


def _ada_kernel(c_ref, w_ref, b_ref, o_ref):
    act = _silu(c_ref[...]).astype(BF16)
    o_ref[...] = _dot(act, w_ref[...].astype(BF16)) + b_ref[...]


def _ada(cvecs, w, b):
    rows = cvecs.shape[0]
    n = w.shape[1]
    tn = 1024
    return pl.pallas_call(
        _ada_kernel,
        grid=(n // tn,),
        in_specs=[pl.BlockSpec((rows, D_MODEL), lambda i: (0, 0)),
                  pl.BlockSpec((D_MODEL, tn), lambda i: (0, i)),
                  pl.BlockSpec((1, tn), lambda i: (0, i))],
        out_specs=pl.BlockSpec((rows, tn), lambda i: (0, i)),
        out_shape=jax.ShapeDtypeStruct((rows, n), F32),
        compiler_params=pltpu.CompilerParams(dimension_semantics=("arbitrary",),
                                             vmem_limit_bytes=VMEM_LIMIT),
        name="ada",
    )(cvecs, w, b.reshape(1, n))


def _rope(x, cos, sin):
    lane = lax.broadcasted_iota(jnp.int32, x.shape, 1)
    low = (lane % 32) < 16
    partner = jnp.where(low, pltpu.roll(x, GROUP_W - 16, 1), pltpu.roll(x, 16, 1))
    return x * cos + partner * sin


def _token_specs(xs, n_ctx_tiles, t0=0):
    T = ROW_TILE
    if len(xs) == 1:
        return [pl.BlockSpec((1, T, D_MODEL), lambda b, i: (b, i + t0, 0))]
    return [pl.BlockSpec((1, T, D_MODEL), lambda b, i: (b, jnp.minimum(i + t0, n_ctx_tiles - 1), 0)),
            pl.BlockSpec((1, T, D_MODEL), lambda b, i: (b, jnp.maximum(i + t0 - n_ctx_tiles, 0), 0))]


def _load_tokens(x_refs, n_ctx_tiles, t0=0):
    if len(x_refs) == 1:
        return x_refs[0][0]
    return jnp.where(pl.program_id(1) + t0 < n_ctx_tiles, x_refs[0][0], x_refs[1][0])


def _inp_kernel(n_src, n_ctx_tiles, *refs):
    x_refs = refs[:n_src]
    (mod_ref, w_ref, cos_ref, sin_ref, hg_ref, mlb_ref, isf_ref,
     wa_hi_ref, wa_lo_ref, ba_ref, ps_ref, pk_ref, pg_ref, gate_ref) = refs[n_src:]
    x = _load_tokens(x_refs, n_ctx_tiles)
    sh = mod_ref[0, 0, :, 0:D_MODEL]
    sc = mod_ref[0, 0, :, D_MODEL:2 * D_MODEL]
    xn = ((x * _rms_scale(x)) * (1.0 + sc) + sh).astype(BF16)

    def proj(a, b):
        return _dot(xn, w_ref[:, a:b])

    G = GROUP_W

    def put(off, width, fn=None):
        def post(res):
            ps_ref[0, :, off:off + width] = (res if fn is None else fn(res)).astype(BF16)
        return post

    def hgrn_decay(d):
        def post(z):
            log_lb = hg_ref[3 * d + 0:3 * d + 1, :]
            log_1mlb = hg_ref[3 * d + 1:3 * d + 2, :]
            one_m_lb = hg_ref[3 * d + 2:3 * d + 3, :]
            z2 = z * LOG2E
            ls = _log2_sigmoid(z2)
            other = log_1mlb + ls
            pg_ref[d, 0, :, G_HG_LF:G_HG_LF + G] = (
                jnp.maximum(log_lb, other) + jnp.log2(1.0 + jnp.exp2(_neg_abs(log_lb - other))))
            pk_ref[d, 0] = (one_m_lb * jnp.exp2(ls - z2)).astype(BF16)
        return post

    def narrow(small):
        small_hi, small_lo = _split2(small)
        pre2 = (small + mlb_ref[...]) * LOG2E
        gate_logs = jnp.where(isf_ref[...] > 0.5, _log2_sigmoid(pre2), pre2)
        for d in range(2):
            za = (_dot(small_hi, wa_hi_ref[d]) + _dot(small_lo, wa_hi_ref[d])
                  + _dot(small_hi, wa_lo_ref[d]) + ba_ref[d:d + 1, :])
            pg_ref[d, 0, :, G_GL_LA:G_GL_LA + GLA_W] = _log2_sigmoid(za * LOG2E) * (1.0 / GLA_TAU)
            pg_ref[d, 0, :, G_SMALL:PG_W] = gate_logs

    def gates(g):
        gate_ref[0, :, 0:2 * G] = _sigmoid(g[:, 0:2 * G]).astype(BF16)
        gate_ref[0, :, 2 * G:4 * G] = _silu(g[:, 2 * G:4 * G]).astype(BF16)

    def rope(scale):
        return lambda r: _rope(r * scale if scale != 1.0 else r, cos_ref[...], sin_ref[...])

    o = 10 * G
    stages = [
        (0, G, put(O_HG_Q, G, _silu)),
        (G, 2 * G, hgrn_decay(0)),
        (2 * G, 3 * G, hgrn_decay(1)),
        (3 * G, 5 * G, put(O_HG_V, 2 * G)),
        (5 * G, 6 * G, put(O_ML_K, G, lambda r: r * (HEAD_DIM ** -0.5))),
        (6 * G, 7 * G, put(O_ML_V, G)),
        (7 * G, 8 * G, put(O_RT_Q, G, rope(1.0))),
        (8 * G, 9 * G, put(O_RT_K, G, rope(HEAD_DIM ** -0.5))),
        (9 * G, o + GLA_W, put(O_RT_V, G + GLA_W)),
        (o + GLA_W, o + 2 * GLA_W, put(O_GL_K, GLA_W, lambda r: r * (GLA_DK ** -0.5))),
        (o + 2 * GLA_W, W_SMALL, put(O_GL_V, G)),
        (W_SMALL, W_GATES, narrow),
        (W_GATES, IN_PAD, gates),
    ]
    pending = None
    for a, b, post in stages:
        res = proj(a, b)
        if pending is not None:
            pending[1](pending[0])
        pending = (res, post)
    pending[1](pending[0])


def _inp(xs, mods, w_in_p, cos_t, sin_t, hg_par, ml_bias, wa_hi, wa_lo, ba, n_ctx_tiles):
    B = xs[0].shape[0]
    NT = sum(a.shape[1] for a in xs)
    T = ROW_TILE
    params = [hg_par, ml_bias, jnp.asarray(_NP["is_f"]), wa_hi, wa_lo, ba]
    return pl.pallas_call(
        functools.partial(_inp_kernel, len(xs), n_ctx_tiles),
        grid=(B, NT // T),
        in_specs=_token_specs(xs, n_ctx_tiles)
        + [pl.BlockSpec((1, 1, 1, 6 * D_MODEL),
                        lambda b, i: (b, jnp.where(i < n_ctx_tiles, 0, 1), 0, 0)),
                  pl.BlockSpec((D_MODEL, IN_PAD), lambda b, i: (0, 0)),
                  pl.BlockSpec((T, GROUP_W), lambda b, i: (i, 0)),
                  pl.BlockSpec((T, GROUP_W), lambda b, i: (i, 0))]
        + [_full_spec(a.shape) for a in params],
        out_specs=[pl.BlockSpec((1, T, PS_W), lambda b, i: (b, i, 0)),
                   pl.BlockSpec((2, 1, T, GROUP_W), lambda b, i: (0, b, i, 0)),
                   pl.BlockSpec((2, 1, T, PG_W), lambda b, i: (0, b, i, 0)),
                   pl.BlockSpec((1, T, D_MODEL), lambda b, i: (b, i, 0))],
        out_shape=[jax.ShapeDtypeStruct((B, NT, PS_W), BF16),
                   jax.ShapeDtypeStruct((2, B, NT, GROUP_W), BF16),
                   jax.ShapeDtypeStruct((2, B, NT, PG_W), F32),
                   jax.ShapeDtypeStruct((B, NT, D_MODEL), BF16)],
        compiler_params=pltpu.CompilerParams(dimension_semantics=("parallel", "parallel"),
                                             vmem_limit_bytes=VMEM_LIMIT),
        name="inp",
    )(*xs, mods, w_in_p, cos_t, sin_t, *params)


LANES = 128


def _head_blocks(a16, hmask_ref):
    C, W = a16.shape
    if W == LANES:
        return jnp.concatenate([a16, a16, a16, a16], axis=0) * hmask_ref[:, GROUP_W:GROUP_W + W]
    even = a16 * hmask_ref[0:C, 0:W]
    odd = a16 * hmask_ref[C:2 * C, 0:W]
    zero = jnp.zeros((C, LANES), a16.dtype)
    return jnp.concatenate([
        jnp.concatenate([even[:, 0:LANES], zero], axis=1),
        jnp.concatenate([odd[:, 0:LANES], zero], axis=1),
        jnp.concatenate([zero, even[:, LANES:W]], axis=1),
        jnp.concatenate([zero, odd[:, LANES:W]], axis=1)], axis=0)


def _last_row(a, d):
    return a[CHUNK - 1:CHUNK, :] if d == 0 else a[0:1, :]


def _level_exponent(d, h, log_a, b, row4):
    C, W = log_a.shape
    if h >= 4:
        b3 = b.reshape(C // (2 * h), 2 * h, W)
        r = h - 1 if d == 0 else h
        ref = jnp.broadcast_to(b3[:, r:r + 1, :], b3.shape).reshape(C, W)
        return _neg_abs(b - ref)
    up = pltpu.roll(log_a, C - 1, 0)
    dn = pltpu.roll(log_a, 1, 0)
    if d == 0:
        return jnp.where(row4 == 0, up, jnp.where(row4 == 1, 0.0,
                                                  log_a + jnp.where(row4 == 3, dn, 0.0)))
    return jnp.where(row4 == 0, log_a + up,
                     jnp.where(row4 == 1, log_a, jnp.where(row4 == 2, 0.0, dn)))


def _interleave(chains):
    live = list(chains)
    while live:
        for g in list(live):
            try:
                next(g)
            except StopIteration:
                live.remove(g)


def _update_state(st_ref, d, st, decay_row, bd_state, upd):
    rows = st.shape[0] // HEADS
    per_head = st.shape[1] // HEADS
    for h in range(HEADS):
        j = (h * per_head) // LANES
        rs = slice(h * rows, (h + 1) * rows)
        cs = slice(j * LANES, (j + 1) * LANES)
        st_ref[d, rs, cs] = decay_row[:, cs] * st[rs, cs] + bd_state[rs, cs] * upd[rs, cs]


def _vector_decay_chain(d, q16, k16, v16, log_a, y_ref, o_y, st_ref, tri, lev_ref, eye,
                        hmask_ref, bd_state):
    C, W = log_a.shape
    G = GROUP_W
    b = _sel_left(tri, log_a)
    row4 = lax.broadcasted_iota(jnp.int32, (C, W), 0) & 3
    order = sorted(range(len(LEVELS)), key=lambda li: LEVELS[li] >= 4)
    attn = None
    pending = None
    for li in order:
        h = LEVELS[li]
        if h == 1:
            qa = jnp.concatenate([q16 * jnp.exp2(log_a).astype(BF16), q16], axis=0)
            kb = k16
        else:
            f16 = jnp.exp2(_level_exponent(d, h, log_a, b, row4)).astype(BF16)
            qa = q16 * f16
            kb = k16 * f16
        part = _dot_nt(qa, _head_blocks(kb, hmask_ref))
        if pending is not None:
            attn = pending if attn is None else attn + pending
        yield
        pending = part * lev_ref[d, li] if h > 1 else part[0:C] * lev_ref[d, li] + part[C:2 * C] * eye
    attn = attn + pending

    st = st_ref[d]
    b_end = _last_row(b, d)
    q_state = q16 * jnp.exp2(b).astype(BF16)
    k_end = k16 * jnp.exp2(b_end - b).astype(BF16)
    o = _dot(attn.astype(BF16), _head_blocks(v16, hmask_ref)) + _dot_nt(q_state, st.astype(BF16))
    upd = _dot_tn(v16, k_end)
    yield
    y_ref(o_y, o)
    _update_state(st_ref, d, st, jnp.exp2(b_end), bd_state, upd)


def _mlstm_chain(d, q16, k16, v16, small, y_ref, s_ml, n_ml, m_ml, tri, incl_neg, trit, eye,
                 seli, self_, hmask_ref, bd256, bd256_16):
    C = CHUNK
    G = GROUP_W
    i_bc = _sel_right(small, seli)
    f_bc = _sel_right(small, self_)
    qk = _dot_nt(q16, _head_blocks(k16, hmask_ref))
    yield
    ones_row = jnp.ones((8, C), BF16)
    b_col = _sel_left(tri, f_bc)
    b_row = _sel_left(ones_row, f_bc * trit)[0:1]
    i_row = _sel_left(ones_row, i_bc * eye)[0:1]
    yield
    w = b_col - b_row + i_row + incl_neg
    m_prev = m_ml[d, 0:1, :]
    inter = b_col + m_prev
    row_max = [jnp.max(w[:, h * HEAD_DIM:(h + 1) * HEAD_DIM], axis=-1, keepdims=True)
               for h in range(HEADS)]
    yield
    lane_head = lax.broadcasted_iota(jnp.int32, (C, G), 1) // HEAD_DIM
    m_t = None
    for h in range(HEADS):
        mh = jnp.maximum(row_max[h], inter[:, h * HEAD_DIM:h * HEAD_DIM + 1])
        mh = jnp.broadcast_to(mh, (C, G))
        m_t = mh if m_t is None else jnp.where(lane_head == h, mh, m_t)
    g_in = jnp.exp2(inter - m_t)
    s = (qk * jnp.exp2(w - m_t)).astype(BF16)
    c_st = s_ml[d]
    n_row = n_ml[d, 0:1, :]
    qn = (q16.astype(F32) * n_row).astype(BF16)
    num_a = _dot(s, _head_blocks(v16, hmask_ref))
    num_b = _dot(q16, c_st.astype(BF16))
    den_ab = _dot(jnp.concatenate([s, qn], axis=0), bd256_16)
    m_new = _last_row(m_t, d)
    b_end = _last_row(b_col, d)
    a_state = jnp.exp2(b_end + m_prev - m_new)
    kw = k16.astype(F32) * jnp.exp2(b_end - b_col + i_bc - m_new)
    upd = _dot_tn(kw.astype(BF16), v16)
    yield
    num = num_a + g_in * num_b
    den = den_ab[0:C] + g_in * den_ab[C:2 * C]
    y_ref(G, num / jnp.maximum(jnp.abs(den), jnp.exp2(-m_t)))
    _update_state(s_ml, d, c_st, a_state, bd256, upd)
    n_ml[d] = jnp.broadcast_to(a_state * n_row + jnp.sum(kw, axis=0, keepdims=True), (8, G))
    m_ml[d] = jnp.broadcast_to(m_new, (8, G))


def _retention_chain(d, q16, k16, v16, y_ref, s_rt, log_g, dist, incl_neg, hmask_ref, bd256):
    C = CHUNK
    G = GROUP_W
    qk = _dot_nt(q16, _head_blocks(k16, hmask_ref))
    decay = jnp.exp2(log_g * dist + incl_neg)
    yield
    rowc = lax.broadcasted_iota(jnp.int32, (C, 1), 0).astype(F32)
    cnt = (rowc + 1.0) if d == 0 else (C - rowc)
    q_state = q16 * jnp.exp2(log_g * cnt).astype(BF16)
    k_end = k16 * jnp.exp2(log_g * (C - cnt)).astype(BF16)
    st = s_rt[d]
    o = _dot((qk * decay).astype(BF16), _head_blocks(v16, hmask_ref)) + _dot(q_state, st.astype(BF16))
    upd = _dot_tn(k_end, v16)
    yield
    y_ref(2 * G, o)
    _update_state(s_rt, d, st, jnp.exp2(log_g * float(C)), bd256, upd)


def _mix_kernel(psf_ref, psb_ref, pkf_ref, pkb_ref, pgf_ref, pgb_ref, rtg_ref,
                tri_ref, lev_ref, incl_ref, trit_ref, eye_ref, dist_ref,
                hmask_ref, bd256_ref, bdgl_ref, seli_ref, self_ref,
                yf_ref, yb_ref,
                s_hg, s_rt, s_gl, s_ml, n_ml, m_ml):
    C = CHUNK
    G = GROUP_W

    @pl.when(pl.program_id(1) == 0)
    def _():
        s_hg[...] = jnp.zeros_like(s_hg)
        s_rt[...] = jnp.zeros_like(s_rt)
        s_gl[...] = jnp.zeros_like(s_gl)
        s_ml[...] = jnp.zeros_like(s_ml)
        n_ml[...] = jnp.zeros_like(n_ml)
        m_ml[...] = jnp.zeros_like(m_ml)

    bd256 = bd256_ref[...]
    bd256_16 = bd256.astype(BF16)

    def one_chunk(i, carry):
        chains = []
        for d, (ps_ref, pk_ref, pg_ref, y_ref) in enumerate(((psf_ref, pkf_ref, pgf_ref, yf_ref),
                                                             (psb_ref, pkb_ref, pgb_ref, yb_ref))):
            incl_neg = incl_ref[d]
            tri = tri_ref[d]
            eye = eye_ref[...]
            sub = i if d == 0 else STEP_CHUNKS - 1 - i
            rows = pl.ds(pl.multiple_of(sub * C, C), C)

            def ps(off, width, ps_ref=ps_ref, rows=rows):
                return ps_ref[0, rows, off:off + width]

            def pg(off, width, pg_ref=pg_ref, rows=rows):
                return pg_ref[0, 0, rows, off:off + width]

            def put(off, val, y_ref=y_ref, rows=rows):
                y_ref[0, rows, off:off + G] = val

            chains += [
                _vector_decay_chain(d, ps(O_HG_Q, G), pk_ref[0, 0, rows, :], ps(O_HG_V, G),
                                    pg(G_HG_LF, G), put, 0, s_hg, tri, lev_ref, eye, hmask_ref, bd256),
                _mlstm_chain(d, ps(O_ML_Q, G), ps(O_ML_K, G), ps(O_ML_V, G),
                             pg(G_SMALL, PG_W - G_SMALL), put, s_ml, n_ml, m_ml, tri, incl_neg,
                             trit_ref[d], eye, seli_ref[d], self_ref[d], hmask_ref, bd256, bd256_16),
                _retention_chain(d, ps(O_RT_Q, G), ps(O_RT_K, G), ps(O_RT_V, G), put, s_rt,
                                 rtg_ref[d:d + 1, :], dist_ref[...], incl_neg, hmask_ref, bd256),
                _vector_decay_chain(d, ps(O_GL_Q, GLA_W), ps(O_GL_K, GLA_W), ps(O_GL_V, G),
                                    pg(G_GL_LA, GLA_W), put, 3 * G, s_gl, tri, lev_ref, eye,
                                    hmask_ref, bdgl_ref[...]),
            ]
        _interleave(chains)
        return carry

    lax.fori_loop(0, STEP_CHUNKS, one_chunk, 0)


def _mix_call(B, NT, n_ctx_chunks):
    C = CHUNK * STEP_CHUNKS
    assert NT % C == 0 and n_ctx_chunks % STEP_CHUNKS == 0
    NC = NT // C
    n_ctx_blocks = n_ctx_chunks // STEP_CHUNKS
    G = GROUP_W

    def bwd_chunk(j):
        return jnp.where(j < n_ctx_blocks, n_ctx_blocks - 1 - j, NC - 1 - (j - n_ctx_blocks))

    def fwd_map(b, j):
        return (b, j, 0)

    def bwd_map(b, j):
        return (b, bwd_chunk(j), 0)

    consts = [jnp.asarray(_NP["tri"], BF16), jnp.asarray(_NP["lev"]), jnp.asarray(_NP["incl_neg"]),
              jnp.asarray(_NP["tri_t"]), jnp.asarray(_NP["eye"]), jnp.asarray(_NP["dist"]),
              jnp.asarray(_NP["hmask"], BF16),
              jnp.asarray(_NP["bd256"]), jnp.asarray(_NP["bd_gl"]),
              jnp.asarray(_NP["sel_i"], BF16), jnp.asarray(_NP["sel_f"], BF16)]
    call = pl.pallas_call(
        _mix_kernel,
        grid=(B, NC),
        in_specs=[pl.BlockSpec((1, C, PS_W), fwd_map), pl.BlockSpec((1, C, PS_W), bwd_map),
                  pl.BlockSpec((1, 1, C, G), lambda b, j: (0, b, j, 0)),
                  pl.BlockSpec((1, 1, C, G), lambda b, j: (1, b, bwd_chunk(j), 0)),
                  pl.BlockSpec((1, 1, C, PG_W), lambda b, j: (0, b, j, 0)),
                  pl.BlockSpec((1, 1, C, PG_W), lambda b, j: (1, b, bwd_chunk(j), 0)),
                  _full_spec((8, G))] + [_full_spec(a.shape) for a in consts],
        out_specs=[pl.BlockSpec((1, C, D_MODEL), fwd_map), pl.BlockSpec((1, C, D_MODEL), bwd_map)],
        out_shape=[jax.ShapeDtypeStruct((B, NT, D_MODEL), F32)] * 2,
        scratch_shapes=[pltpu.VMEM((2, G, G), F32), pltpu.VMEM((2, G, G), F32),
                        pltpu.VMEM((2, G, GLA_W), F32), pltpu.VMEM((2, G, G), F32),
                        pltpu.VMEM((2, 8, G), F32), pltpu.VMEM((2, 8, G), F32)],
        compiler_params=pltpu.CompilerParams(dimension_semantics=("parallel", "arbitrary"),
                                             vmem_limit_bytes=VMEM_LIMIT),
        name="mix",
    )
    return lambda ps, pk, pg, rt_par: call(ps, ps, pk, pk, pg, pg, rt_par, *consts)


def _out_kernel(final, n_src, n_ctx_tiles, t0, *refs):
    x_refs = refs[:n_src]
    (yf_ref, yb_ref, gate_ref, mod_ref, gh_ref, bd_ref,
     wo_ref, w1_ref, w2_ref, gf_ref, o_ref) = refs[n_src:]
    D = D_MODEL
    x = _load_tokens(x_refs, n_ctx_tiles, t0)
    bd = bd_ref[...]
    for g in range(D // GROUP_W):
        sl = slice(g * GROUP_W, (g + 1) * GROUP_W)
        y = yf_ref[0, :, sl] + yb_ref[0, :, sl]
        msq = _sel_right(y * y, bd) * (1.0 / HEAD_DIM)
        yn = y * lax.rsqrt(msq + RMS_EPS) * gh_ref[:, sl] * gate_ref[0, :, sl].astype(F32)
        part = _dot(yn.astype(BF16), wo_ref[sl, :])
        mixed = part if g == 0 else mixed + part
    g1 = mod_ref[0, 0, :, 2 * D:3 * D]
    sh2 = mod_ref[0, 0, :, 3 * D:4 * D]
    sc2 = mod_ref[0, 0, :, 4 * D:5 * D]
    g2 = mod_ref[0, 0, :, 5 * D:6 * D]
    x1 = x + g1 * mixed
    hin = ((x1 * _rms_scale(x1)) * (1.0 + sc2) + sh2).astype(BF16)
    hid = jnp.maximum(_dot(hin, w1_ref[...]), 0.0)
    ff = _dot((hid * hid).astype(BF16), w2_ref[...])
    x2 = x1 + g2 * ff
    if final:
        x2 = (x2 * _rms_scale(x2)) * gf_ref[...]
    o_ref[0] = x2


def _out(xs, yf, yb, gate, mods, g_heads, w_out, w_ff1, w_ff2, g_final, n_ctx_tiles, final):
    B, NT, D = yf.shape
    T = ROW_TILE
    t0 = n_ctx_tiles if final else 0
    n_tiles = NT // T - t0
    tok = lambda b, i: (b, i + t0, 0)
    bd = jnp.asarray(_NP["bd256"], BF16)
    wspec = lambda shape: pl.BlockSpec(shape, lambda b, i: (0, 0), pipeline_mode=pl.Buffered(1))
    return pl.pallas_call(
        functools.partial(_out_kernel, final, len(xs), n_ctx_tiles, t0),
        grid=(B, n_tiles),
        in_specs=_token_specs(xs, n_ctx_tiles, t0)
        + [pl.BlockSpec((1, T, D), tok),
                  pl.BlockSpec((1, T, D), tok), pl.BlockSpec((1, T, D), tok),
                  pl.BlockSpec((1, 1, 1, 6 * D),
                               lambda b, i: (b, jnp.where(i + t0 < n_ctx_tiles, 0, 1), 0, 0)),
                  pl.BlockSpec((1, D), lambda b, i: (0, 0)),
                  pl.BlockSpec((GROUP_W, GROUP_W), lambda b, i: (0, 0)),
                  wspec((D, D)), wspec((D, D_FF)), wspec((D_FF, D)),
                  pl.BlockSpec((1, D), lambda b, i: (0, 0))],
        out_specs=pl.BlockSpec((1, T, D), lambda b, i: (b, i, 0)),
        out_shape=jax.ShapeDtypeStruct((B, n_tiles * T, D), F32),
        compiler_params=pltpu.CompilerParams(dimension_semantics=("parallel", "parallel"),
                                             vmem_limit_bytes=VMEM_LIMIT),
        name="out_final" if final else "out",
    )(*xs, yf, yb, gate, mods, g_heads.reshape(1, D), bd, w_out, w_ff1, w_ff2,
      g_final.reshape(1, D))


_REF_LAYOUT = (('hg_q', 256), ('hg_f_fwd', 256), ('hg_f_bwd', 256), ('hg_i', 256), ('hg_g', 256),
               ('ml_q', 256), ('ml_k', 256), ('ml_v', 256), ('ml_if', 16), ('ml_o', 256),
               ('rt_q', 256), ('rt_k', 256), ('rt_v', 256), ('rt_g', 256),
               ('gl_q', 128), ('gl_k', 128), ('gl_v', 256), ('gl_a_fwd', 16), ('gl_a_bwd', 16),
               ('gl_g', 256))
_NEW_ORDER = ('hg_q', 'hg_f_fwd', 'hg_f_bwd', 'hg_i', 'ml_q', 'ml_k', 'ml_v', 'rt_q', 'rt_k', 'rt_v',
              'gl_q', 'gl_k', 'gl_v', 'ml_if', 'gl_a_fwd', 'gl_a_bwd', 'PAD', 'hg_g', 'ml_o', 'rt_g', 'gl_g')


def _column_index():
    starts, off = {}, 0
    for name, w in _REF_LAYOUT:
        starts[name] = (off, w)
        off += w
    idx = []
    for name in _NEW_ORDER:
        if name == 'PAD':
            idx += [-1] * (128 - 48)
        else:
            a, w = starts[name]
            idx += list(range(a, a + w))
    idx = np.asarray(idx)
    assert idx.shape[0] == IN_PAD
    return np.where(idx < 0, 0, idx), (idx >= 0)


_COL_IDX, _COL_VALID = _column_index()


def _rope_tables(n_ctx, n_lat):
    n = jnp.arange(n_lat)
    r = (n // GRID_W).astype(F32)
    col = (n % GRID_W).astype(F32)
    n_freq = HEAD_DIM // 4
    inv = ROPE_BASE ** (-jnp.arange(n_freq, dtype=F32) / n_freq)
    ar = r[:, None] * inv[None, :]
    ac = col[:, None] * inv[None, :]
    cos = jnp.concatenate([jnp.cos(ar), jnp.cos(ar), jnp.cos(ac), jnp.cos(ac)], axis=-1)
    sin = jnp.concatenate([-jnp.sin(ar), jnp.sin(ar), -jnp.sin(ac), jnp.sin(ac)], axis=-1)
    cos = jnp.concatenate([jnp.ones((n_ctx, HEAD_DIM), F32), cos], axis=0)
    sin = jnp.concatenate([jnp.zeros((n_ctx, HEAD_DIM), F32), sin], axis=0)
    return jnp.tile(cos, (1, HEADS)), jnp.tile(sin, (1, HEADS))


def kernel(x, c, ctx, c_ctx, w_ada, b_ada, w_in, g_heads, hgrn_lb_logits, ml_gate_bias,
           rt_decay_logit, gla_w_a, gla_b_a, w_out, w_ff1, w_ff2, g_final):
    B, L, D = x.shape
    Lc = ctx.shape[1]
    depth = w_in.shape[0]
    assert D == D_MODEL and L % ROW_TILE == 0 and Lc % ROW_TILE == 0 and L % GRID_W == 0
    n_ctx_tiles = Lc // ROW_TILE
    n_ctx_chunks = Lc // CHUNK

    xs = (ctx, x)
    cos_t, sin_t = _rope_tables(Lc, L)
    mix = _mix_call(B, Lc + L, n_ctx_chunks)

    n_c = -(-(B + 1) // 8) * 8
    cvecs = jnp.zeros((n_c, D), F32).at[:B].set(c).at[B].set(c_ctx)

    sm = jax.nn.softmax(hgrn_lb_logits.astype(F32), axis=0)
    lb_all = jnp.maximum(jnp.cumsum(sm, axis=0) - sm[:1], 0.0)

    col_valid = jnp.asarray(_COL_VALID, F32)[None, :]
    out = None
    for layer in range(depth):
        final = layer == depth - 1
        mod = _ada(cvecs, w_ada[layer], b_ada[layer])
        mods = jnp.stack([jnp.broadcast_to(mod[B], (B, 6 * D)), mod[:B]], axis=1)[:, :, None, :]

        w_in_p = (w_in[layer][:, _COL_IDX] * col_valid).astype(BF16)
        lb = lb_all[layer]
        hg_par = jnp.stack([jnp.log(lb[0]) * LOG2E, jnp.log1p(-lb[0]) * LOG2E, 1.0 - lb[0],
                            jnp.log(lb[1]) * LOG2E, jnp.log1p(-lb[1]) * LOG2E, 1.0 - lb[1],
                            jnp.zeros_like(lb[0]), jnp.zeros_like(lb[0])], axis=0)
        ml_bias = jnp.zeros((1, 128), F32).at[0, :16].set(ml_gate_bias[layer].astype(F32).reshape(16))
        wa = jnp.zeros((2, 128, GLA_W), F32)
        wa = wa.at[0, 16:32].set(gla_w_a[layer, 0].astype(F32)).at[1, 32:48].set(gla_w_a[layer, 1].astype(F32))
        wa_hi = wa.astype(BF16)
        wa_lo = (wa - wa_hi.astype(F32)).astype(BF16)
        ba = jnp.zeros((8, GLA_W), F32).at[:2].set(gla_b_a[layer].astype(F32))
        ps, pk, pg, gate = _inp(xs, mods, w_in_p, cos_t, sin_t, hg_par, ml_bias, wa_hi, wa_lo, ba,
                                n_ctx_tiles)

        rt_par = jnp.zeros((8, GROUP_W), F32).at[:2].set(
            jnp.repeat(jax.nn.log_sigmoid(rt_decay_logit[layer].astype(F32)) * LOG2E, HEAD_DIM, axis=-1))
        yf, yb = mix(ps, pk, pg, rt_par)

        res = _out(xs, yf, yb, gate, mods, g_heads[layer], w_out[layer].astype(BF16),
                   w_ff1[layer].astype(BF16), w_ff2[layer].astype(BF16), g_final,
                   n_ctx_tiles, final)
        if final:
            out = res
        else:
            xs = (res,)
    return out
```

```python
import functools

import numpy as np
import jax
import jax.numpy as jnp
from jax import lax
from jax.experimental import pallas as pl
from jax.experimental.pallas import tpu as pltpu

F32 = jnp.float32
BF16 = jnp.bfloat16

D_MODEL = 1024
GROUP_W = 256
HEADS = 4
HEAD_DIM = 64
GLA_DK = 32
GLA_W = HEADS * GLA_DK
GLA_TAU = 16.0
D_FF = 4 * D_MODEL
GRID_W = 64
ROPE_BASE = 10000.0
RMS_EPS = 1e-6

CHUNK = 64
STEP_CHUNKS = 4
ROW_TILE = 256
LEVELS = (32, 16, 8, 4, 2, 1)

W_SMALL = 3072
W_GATES = 3200
IN_PAD = W_GATES + D_MODEL

O_HG_Q, O_HG_V = 0, 256
O_ML_Q, O_ML_K, O_ML_V = 512, 768, 1024
O_RT_Q, O_RT_K, O_RT_V = 1280, 1536, 1792
O_GL_Q, O_GL_K, O_GL_V = 2048, 2176, 2304
PS_W = 2560
G_HG_LF, G_GL_LA, G_SMALL = 0, 256, 384
PG_W = 512

VMEM_LIMIT = 56 * 1024 * 1024

NEG_INF = float("-inf")
LOG2E = 1.4426950408889634


def _dot(a, b):
    return jnp.dot(a, b, preferred_element_type=F32)


def _dot_nt(a, b):
    return lax.dot_general(a, b, (((1,), (1,)), ((), ())), preferred_element_type=F32)


def _dot_tn(a, b):
    return lax.dot_general(a, b, (((0,), (0,)), ((), ())), preferred_element_type=F32)


def _split2(a):
    hi = a.astype(BF16)
    return hi, (a - hi.astype(F32)).astype(BF16)


def _cumsum_rows(x, d):
    n = x.shape[0]
    row = lax.broadcasted_iota(jnp.int32, x.shape, 0)
    s = 1
    while s < 8:
        if d == 0:
            x = x + jnp.where(row >= s, pltpu.roll(x, s, 0), 0.0)
        else:
            x = x + jnp.where(row < n - s, pltpu.roll(x, n - s, 0), 0.0)
        s *= 2
    while s < n:
        if d == 0:
            x = jnp.concatenate([x[:s], x[s:] + x[:n - s]], axis=0)
        else:
            x = jnp.concatenate([x[:n - s] + x[s:], x[n - s:]], axis=0)
        s *= 2
    return x


def _sel_right(a, m):
    hi, lo = _split2(a)
    return _dot(hi, m) + _dot(lo, m)


def _neg_abs(x):
    return pltpu.bitcast(pltpu.bitcast(x, jnp.uint32) | jnp.uint32(0x80000000), F32)


def _log2_sigmoid(z2):
    return jnp.minimum(z2, 0.0) - jnp.log2(1.0 + jnp.exp2(_neg_abs(z2)))


def _sigmoid(z):
    return 1.0 / (1.0 + jnp.exp(-z))


def _silu(z):
    return z * _sigmoid(z)


def _rms_scale(x):
    return lax.rsqrt(jnp.mean(x * x, axis=-1, keepdims=True) + RMS_EPS)


def _np_consts():
    C = CHUNK
    t = np.arange(C)
    T, U = np.meshgrid(t, t, indexing="ij")
    incl = [U <= T, U >= T]

    lev = np.zeros((2, len(LEVELS), C, 4 * C), np.float32)
    for d in range(2):
        for li, h in enumerate(LEVELS):
            same = (T // (2 * h)) == (U // (2 * h))
            if d == 0:
                pair = same & (T % (2 * h) >= h) & (U % (2 * h) < h)
            else:
                pair = same & (T % (2 * h) < h) & (U % (2 * h) >= h)
            lev[d, li] = np.tile(pair.astype(np.float32), (1, 4))

    incl_neg = np.stack([np.tile(np.where(incl[d], 0.0, NEG_INF).astype(np.float32), (1, 4))
                         for d in range(2)])
    eye = np.tile(np.eye(C, dtype=np.float32), (1, 4))
    dist = np.tile(np.abs(T - U).astype(np.float32), (1, 4))

    head_of_row = np.arange(4 * C) // C
    head_of_lane = np.arange(GROUP_W) // HEAD_DIM
    parity = np.where(head_of_row < 2, head_of_row, -1)
    hmask = np.concatenate(
        [(head_of_lane % 2)[None, :] == parity[:, None],
         head_of_row[:, None] == (np.arange(GLA_W) // GLA_DK)[None, :]], axis=1).astype(np.float32)
    bd256 = ((np.arange(GROUP_W) // HEAD_DIM)[:, None]
             == (np.arange(GROUP_W) // HEAD_DIM)[None, :]).astype(np.float32)
    bd_gl = ((np.arange(GROUP_W) // HEAD_DIM)[:, None]
             == (np.arange(GLA_W) // GLA_DK)[None, :]).astype(np.float32)

    sel_i = np.zeros((2, 128, GROUP_W), np.float32)
    sel_f = np.zeros((2, 128, GROUP_W), np.float32)
    for d in range(2):
        for h in range(HEADS):
            sel_i[d, d * 8 + h, h * HEAD_DIM:(h + 1) * HEAD_DIM] = 1.0
            sel_f[d, d * 8 + 4 + h, h * HEAD_DIM:(h + 1) * HEAD_DIM] = 1.0
    is_f = np.zeros((1, 128), np.float32)
    is_f[0, 4:8] = 1.0
    is_f[0, 12:16] = 1.0
    return dict(lev=lev, incl_neg=incl_neg, eye=eye, dist=dist,
                hmask=hmask, bd256=bd256, bd_gl=bd_gl,
                sel_i=sel_i, sel_f=sel_f, is_f=is_f)


_NP = _np_consts()


def _full_spec(shape):
    n = len(shape)
    return pl.BlockSpec(tuple(shape), lambda *_: (0,) * n)


def _ada_kernel(c_ref, w_ref, b_ref, o_ref):
    act = _silu(c_ref[...]).astype(BF16)
    o_ref[...] = _dot(act, w_ref[...].astype(BF16)) + b_ref[...]


def _ada(cvecs, w, b):
    rows = cvecs.shape[0]
    n = w.shape[1]
    tn = 1024
    return pl.pallas_call(
        _ada_kernel,
        grid=(n // tn,),
        in_specs=[pl.BlockSpec((rows, D_MODEL), lambda i: (0, 0)),
                  pl.BlockSpec((D_MODEL, tn), lambda i: (0, i)),
                  pl.BlockSpec((1, tn), lambda i: (0, i))],
        out_specs=pl.BlockSpec((rows, tn), lambda i: (0, i)),
        out_shape=jax.ShapeDtypeStruct((rows, n), F32),
        compiler_params=pltpu.CompilerParams(dimension_semantics=("arbitrary",),
                                             vmem_limit_bytes=VMEM_LIMIT),
        name="ada",
    )(cvecs, w, b.reshape(1, n))


def _rope(x, cos, sin):
    lane = lax.broadcasted_iota(jnp.int32, x.shape, 1)
    low = (lane % 32) < 16
    partner = jnp.where(low, pltpu.roll(x, GROUP_W - 16, 1), pltpu.roll(x, 16, 1))
    return x * cos + partner * sin


def _token_specs(xs, n_ctx_tiles, t0=0):
    T = ROW_TILE
    if len(xs) == 1:
        return [pl.BlockSpec((1, T, D_MODEL), lambda b, i: (b, i + t0, 0))]
    return [pl.BlockSpec((1, T, D_MODEL), lambda b, i: (b, jnp.minimum(i + t0, n_ctx_tiles - 1), 0)),
            pl.BlockSpec((1, T, D_MODEL), lambda b, i: (b, jnp.maximum(i + t0 - n_ctx_tiles, 0), 0))]


def _load_tokens(x_refs, n_ctx_tiles, t0=0):
    if len(x_refs) == 1:
        return x_refs[0][0]
    return jnp.where(pl.program_id(1) + t0 < n_ctx_tiles, x_refs[0][0], x_refs[1][0])


def _inp_kernel(n_src, n_ctx_tiles, *refs):
    x_refs = refs[:n_src]
    (mod_ref, w_ref, cos_ref, sin_ref, hg_ref, mlb_ref, isf_ref,
     wa_hi_ref, wa_lo_ref, ba_ref, ps_ref, pk_ref, pg_ref, gate_ref) = refs[n_src:]
    x = _load_tokens(x_refs, n_ctx_tiles)
    sh = mod_ref[0, 0, :, 0:D_MODEL]
    sc = mod_ref[0, 0, :, D_MODEL:2 * D_MODEL]
    xn = ((x * _rms_scale(x)) * (1.0 + sc) + sh).astype(BF16)

    def proj(a, b):
        return _dot(xn, w_ref[:, a:b])

    G = GROUP_W

    def put(off, width, fn=None):
        def post(res):
            ps_ref[0, :, off:off + width] = (res if fn is None else fn(res)).astype(BF16)
        return post

    def hgrn_decay(d):
        def post(z):
            log_lb = hg_ref[3 * d + 0:3 * d + 1, :]
            log_1mlb = hg_ref[3 * d + 1:3 * d + 2, :]
            one_m_lb = hg_ref[3 * d + 2:3 * d + 3, :]
            z2 = z * LOG2E
            ls = _log2_sigmoid(z2)
            other = log_1mlb + ls
            pg_ref[d, 0, :, G_HG_LF:G_HG_LF + G] = (
                jnp.maximum(log_lb, other) + jnp.log2(1.0 + jnp.exp2(_neg_abs(log_lb - other))))
            pk_ref[d, 0] = (one_m_lb * jnp.exp2(ls - z2)).astype(BF16)
        return post

    def narrow(small):
        small_hi, small_lo = _split2(small)
        pre2 = (small + mlb_ref[...]) * LOG2E
        gate_logs = jnp.where(isf_ref[...] > 0.5, _log2_sigmoid(pre2), pre2)
        for d in range(2):
            za = (_dot(small_hi, wa_hi_ref[d]) + _dot(small_lo, wa_hi_ref[d])
                  + _dot(small_hi, wa_lo_ref[d]) + ba_ref[d:d + 1, :])
            pg_ref[d, 0, :, G_GL_LA:G_GL_LA + GLA_W] = _log2_sigmoid(za * LOG2E) * (1.0 / GLA_TAU)
            pg_ref[d, 0, :, G_SMALL:PG_W] = gate_logs

    def gates(half, fn):
        def post(g):
            gate_ref[0, :, half * 2 * G:(half + 1) * 2 * G] = fn(g).astype(BF16)
        return post

    def rope(scale):
        return lambda r: _rope(r * scale if scale != 1.0 else r, cos_ref[...], sin_ref[...])

    o = 10 * G
    stages = [
        (W_GATES, W_GATES + 2 * G, gates(0, _sigmoid)),
        (0, G, put(O_HG_Q, G, _silu)),
        (G, 2 * G, hgrn_decay(0)),
        (3 * G, 5 * G, put(O_HG_V, 2 * G)),
        (2 * G, 3 * G, hgrn_decay(1)),
        (5 * G, 6 * G, put(O_ML_K, G, lambda r: r * (HEAD_DIM ** -0.5))),
        (W_GATES + 2 * G, IN_PAD, gates(1, _silu)),
        (6 * G, 7 * G, put(O_ML_V, G)),
        (7 * G, 8 * G, put(O_RT_Q, G, rope(1.0))),
        (9 * G, o + GLA_W, put(O_RT_V, G + GLA_W)),
        (8 * G, 9 * G, put(O_RT_K, G, rope(HEAD_DIM ** -0.5))),
        (o + GLA_W, o + 2 * GLA_W, put(O_GL_K, GLA_W, lambda r: r * (GLA_DK ** -0.5))),
        (W_SMALL, W_GATES, narrow),
        (o + 2 * GLA_W, W_SMALL, put(O_GL_V, G)),
    ]
    pending = None
    for a, b, post in stages:
        res = proj(a, b)
        if pending is not None:
            pending[1](pending[0])
        pending = (res, post)
    pending[1](pending[0])


def _inp(xs, mods, w_in_p, cos_t, sin_t, hg_par, ml_bias, wa_hi, wa_lo, ba, n_ctx_tiles):
    B = xs[0].shape[0]
    NT = sum(a.shape[1] for a in xs)
    T = ROW_TILE
    params = [hg_par, ml_bias, jnp.asarray(_NP["is_f"]), wa_hi, wa_lo, ba]
    return pl.pallas_call(
        functools.partial(_inp_kernel, len(xs), n_ctx_tiles),
        grid=(B, NT // T),
        in_specs=_token_specs(xs, n_ctx_tiles)
        + [pl.BlockSpec((1, 1, 1, 6 * D_MODEL),
                        lambda b, i: (b, jnp.where(i < n_ctx_tiles, 0, 1), 0, 0)),
                  pl.BlockSpec((D_MODEL, IN_PAD), lambda b, i: (0, 0)),
                  pl.BlockSpec((T, GROUP_W), lambda b, i: (i, 0)),
                  pl.BlockSpec((T, GROUP_W), lambda b, i: (i, 0))]
        + [_full_spec(a.shape) for a in params],
        out_specs=[pl.BlockSpec((1, T, PS_W), lambda b, i: (b, i, 0)),
                   pl.BlockSpec((2, 1, T, GROUP_W), lambda b, i: (0, b, i, 0)),
                   pl.BlockSpec((2, 1, T, PG_W), lambda b, i: (0, b, i, 0)),
                   pl.BlockSpec((1, T, D_MODEL), lambda b, i: (b, i, 0))],
        out_shape=[jax.ShapeDtypeStruct((B, NT, PS_W), BF16),
                   jax.ShapeDtypeStruct((2, B, NT, GROUP_W), BF16),
                   jax.ShapeDtypeStruct((2, B, NT, PG_W), F32),
                   jax.ShapeDtypeStruct((B, NT, D_MODEL), BF16)],
        compiler_params=pltpu.CompilerParams(dimension_semantics=("parallel", "parallel"),
                                             vmem_limit_bytes=VMEM_LIMIT),
        name="inp",
    )(*xs, mods, w_in_p, cos_t, sin_t, *params)


LANES = 128


def _head_blocks(a16, hmask_ref):
    C, W = a16.shape
    if W == LANES:
        return jnp.concatenate([a16, a16, a16, a16], axis=0) * hmask_ref[:, GROUP_W:GROUP_W + W]
    even = a16 * hmask_ref[0:C, 0:W]
    odd = a16 * hmask_ref[C:2 * C, 0:W]
    zero = jnp.zeros((C, LANES), a16.dtype)
    return jnp.concatenate([
        jnp.concatenate([even[:, 0:LANES], zero], axis=1),
        jnp.concatenate([odd[:, 0:LANES], zero], axis=1),
        jnp.concatenate([zero, even[:, LANES:W]], axis=1),
        jnp.concatenate([zero, odd[:, LANES:W]], axis=1)], axis=0)


def _last_row(a, d):
    return a[CHUNK - 1:CHUNK, :] if d == 0 else a[0:1, :]


def _level_exponent(d, h, log_a, b, row4):
    C, W = log_a.shape
    if h >= 4:
        b3 = b.reshape(C // (2 * h), 2 * h, W)
        r = h - 1 if d == 0 else h
        ref = jnp.broadcast_to(b3[:, r:r + 1, :], b3.shape).reshape(C, W)
        return _neg_abs(b - ref)
    up = pltpu.roll(log_a, C - 1, 0)
    dn = pltpu.roll(log_a, 1, 0)
    if d == 0:
        return jnp.where(row4 == 0, up, jnp.where(row4 == 1, 0.0,
                                                  log_a + jnp.where(row4 == 3, dn, 0.0)))
    return jnp.where(row4 == 0, log_a + up,
                     jnp.where(row4 == 1, log_a, jnp.where(row4 == 2, 0.0, dn)))


def _interleave(chains):
    live = list(chains)
    while live:
        for g in list(live):
            try:
                next(g)
            except StopIteration:
                live.remove(g)


def _update_state(st_ref, d, st, decay_row, bd_state, upd):
    rows = st.shape[0] // HEADS
    per_head = st.shape[1] // HEADS
    for h in range(HEADS):
        j = (h * per_head) // LANES
        rs = slice(h * rows, (h + 1) * rows)
        cs = slice(j * LANES, (j + 1) * LANES)
        st_ref[d, rs, cs] = decay_row[:, cs] * st[rs, cs] + bd_state[rs, cs] * upd[rs, cs]


def _vector_decay_chain(d, q16, k16, v16, log_a, y_ref, o_y, st_ref, lev_ref, eye,
                        hmask_ref, bd_state):
    C, W = log_a.shape
    G = GROUP_W
    b = _cumsum_rows(log_a, d)
    row4 = lax.broadcasted_iota(jnp.int32, (C, W), 0) & 3
    order = sorted(range(len(LEVELS)), key=lambda li: LEVELS[li] >= 4)
    attn = None
    pending = None
    for li in order:
        h = LEVELS[li]
        if h == 1:
            qa = jnp.concatenate([q16 * jnp.exp2(log_a).astype(BF16), q16], axis=0)
            kb = k16
        else:
            f16 = jnp.exp2(_level_exponent(d, h, log_a, b, row4)).astype(BF16)
            qa = q16 * f16
            kb = k16 * f16
        part = _dot_nt(qa, _head_blocks(kb, hmask_ref))
        if pending is not None:
            attn = pending if attn is None else attn + pending
        yield
        pending = part * lev_ref[d, li] if h > 1 else part[0:C] * lev_ref[d, li] + part[C:2 * C] * eye
    attn = attn + pending

    st = st_ref[d]
    b_end = _last_row(b, d)
    q_state = q16 * jnp.exp2(b).astype(BF16)
    k_end = k16 * jnp.exp2(b_end - b).astype(BF16)
    o = _dot(attn.astype(BF16), _head_blocks(v16, hmask_ref)) + _dot_nt(q_state, st.astype(BF16))
    upd = _dot_tn(v16, k_end)
    yield
    y_ref(o_y, o)
    _update_state(st_ref, d, st, jnp.exp2(b_end), bd_state, upd)


def _mlstm_chain(d, q16, k16, v16, small, y_ref, s_ml, n_ml, m_ml, incl_neg, eye,
                 seli, self_, hmask_ref, bd256, bd256_16):
    C = CHUNK
    G = GROUP_W
    i_bc = _sel_right(small, seli)
    b_col = _sel_right(_cumsum_rows(small, d), self_)
    qk = _dot_nt(q16, _head_blocks(k16, hmask_ref))
    yield
    b_row = jnp.sum(b_col * eye, axis=0, keepdims=True)
    i_row = jnp.sum(i_bc * eye, axis=0, keepdims=True)
    w = b_col - b_row + i_row + incl_neg
    m_prev = m_ml[d, 0:1, :]
    inter = b_col + m_prev
    row_max = [jnp.max(w[:, h * HEAD_DIM:(h + 1) * HEAD_DIM], axis=-1, keepdims=True)
               for h in range(HEADS)]
    yield
    lane_head = lax.broadcasted_iota(jnp.int32, (C, G), 1) // HEAD_DIM
    m_t = None
    for h in range(HEADS):
        mh = jnp.maximum(row_max[h], inter[:, h * HEAD_DIM:h * HEAD_DIM + 1])
        mh = jnp.broadcast_to(mh, (C, G))
        m_t = mh if m_t is None else jnp.where(lane_head == h, mh, m_t)
    g_in = jnp.exp2(inter - m_t)
    s = (qk * jnp.exp2(w - m_t)).astype(BF16)
    c_st = s_ml[d]
    n_row = n_ml[d, 0:1, :]
    qn = (q16.astype(F32) * n_row).astype(BF16)
    num_a = _dot(s, _head_blocks(v16, hmask_ref))
    num_b = _dot(q16, c_st.astype(BF16))
    den_ab = _dot(jnp.concatenate([s, qn], axis=0), bd256_16)
    m_new = _last_row(m_t, d)
    b_end = _last_row(b_col, d)
    a_state = jnp.exp2(b_end + m_prev - m_new)
    kw = k16.astype(F32) * jnp.exp2(b_end - b_col + i_bc - m_new)
    upd = _dot_tn(kw.astype(BF16), v16)
    yield
    num = num_a + g_in * num_b
    den = den_ab[0:C] + g_in * den_ab[C:2 * C]
    y_ref(G, num / jnp.maximum(jnp.abs(den), jnp.exp2(-m_t)))
    _update_state(s_ml, d, c_st, a_state, bd256, upd)
    n_ml[d] = jnp.broadcast_to(a_state * n_row + jnp.sum(kw, axis=0, keepdims=True), (8, G))
    m_ml[d] = jnp.broadcast_to(m_new, (8, G))


def _retention_chain(d, q16, k16, v16, y_ref, s_rt, log_g, dist, incl_neg, hmask_ref, bd256):
    C = CHUNK
    G = GROUP_W
    qk = _dot_nt(q16, _head_blocks(k16, hmask_ref))
    decay = jnp.exp2(log_g * dist + incl_neg)
    yield
    rowc = lax.broadcasted_iota(jnp.int32, (C, 1), 0).astype(F32)
    cnt = (rowc + 1.0) if d == 0 else (C - rowc)
    q_state = q16 * jnp.exp2(log_g * cnt).astype(BF16)
    k_end = k16 * jnp.exp2(log_g * (C - cnt)).astype(BF16)
    st = s_rt[d]
    o = _dot((qk * decay).astype(BF16), _head_blocks(v16, hmask_ref)) + _dot(q_state, st.astype(BF16))
    upd = _dot_tn(k_end, v16)
    yield
    y_ref(2 * G, o)
    _update_state(s_rt, d, st, jnp.exp2(log_g * float(C)), bd256, upd)


def _mix_kernel(psf_ref, psb_ref, pkf_ref, pkb_ref, pgf_ref, pgb_ref, rtg_ref,
                lev_ref, incl_ref, eye_ref, dist_ref,
                hmask_ref, bd256_ref, bdgl_ref, seli_ref, self_ref,
                yf_ref, yb_ref,
                s_hg, s_rt, s_gl, s_ml, n_ml, m_ml):
    C = CHUNK
    G = GROUP_W

    @pl.when(pl.program_id(1) == 0)
    def _():
        s_hg[...] = jnp.zeros_like(s_hg)
        s_rt[...] = jnp.zeros_like(s_rt)
        s_gl[...] = jnp.zeros_like(s_gl)
        s_ml[...] = jnp.zeros_like(s_ml)
        n_ml[...] = jnp.zeros_like(n_ml)
        m_ml[...] = jnp.zeros_like(m_ml)

    bd256 = bd256_ref[...]
    bd256_16 = bd256.astype(BF16)

    def one_chunk(i, carry):
        chains = []
        for d, (ps_ref, pk_ref, pg_ref, y_ref) in enumerate(((psf_ref, pkf_ref, pgf_ref, yf_ref),
                                                             (psb_ref, pkb_ref, pgb_ref, yb_ref))):
            incl_neg = incl_ref[d]
            eye = eye_ref[...]
            sub = i if d == 0 else STEP_CHUNKS - 1 - i
            rows = pl.ds(pl.multiple_of(sub * C, C), C)

            def ps(off, width, ps_ref=ps_ref, rows=rows):
                return ps_ref[0, rows, off:off + width]

            def pg(off, width, pg_ref=pg_ref, rows=rows):
                return pg_ref[0, 0, rows, off:off + width]

            def put(off, val, y_ref=y_ref, rows=rows):
                y_ref[0, rows, off:off + G] = val

            chains += [
                _vector_decay_chain(d, ps(O_HG_Q, G), pk_ref[0, 0, rows, :], ps(O_HG_V, G),
                                    pg(G_HG_LF, G), put, 0, s_hg, lev_ref, eye, hmask_ref, bd256),
                _mlstm_chain(d, ps(O_ML_Q, G), ps(O_ML_K, G), ps(O_ML_V, G),
                             pg(G_SMALL, PG_W - G_SMALL), put, s_ml, n_ml, m_ml, incl_neg,
                             eye, seli_ref[d], self_ref[d], hmask_ref, bd256, bd256_16),
                _retention_chain(d, ps(O_RT_Q, G), ps(O_RT_K, G), ps(O_RT_V, G), put, s_rt,
                                 rtg_ref[d:d + 1, :], dist_ref[...], incl_neg, hmask_ref, bd256),
                _vector_decay_chain(d, ps(O_GL_Q, GLA_W), ps(O_GL_K, GLA_W), ps(O_GL_V, G),
                                    pg(G_GL_LA, GLA_W), put, 3 * G, s_gl, lev_ref, eye,
                                    hmask_ref, bdgl_ref[...]),
            ]
        _interleave(chains)
        return carry

    lax.fori_loop(0, STEP_CHUNKS, one_chunk, 0)


def _mix_call(B, NT, n_ctx_chunks):
    C = CHUNK * STEP_CHUNKS
    assert NT % C == 0 and n_ctx_chunks % STEP_CHUNKS == 0
    NC = NT // C
    n_ctx_blocks = n_ctx_chunks // STEP_CHUNKS
    G = GROUP_W

    def bwd_chunk(j):
        return jnp.where(j < n_ctx_blocks, n_ctx_blocks - 1 - j, NC - 1 - (j - n_ctx_blocks))

    def fwd_map(b, j):
        return (b, j, 0)

    def bwd_map(b, j):
        return (b, bwd_chunk(j), 0)

    consts = [jnp.asarray(_NP["lev"]), jnp.asarray(_NP["incl_neg"]),
              jnp.asarray(_NP["eye"]), jnp.asarray(_NP["dist"]),
              jnp.asarray(_NP["hmask"], BF16),
              jnp.asarray(_NP["bd256"]), jnp.asarray(_NP["bd_gl"]),
              jnp.asarray(_NP["sel_i"], BF16), jnp.asarray(_NP["sel_f"], BF16)]
    call = pl.pallas_call(
        _mix_kernel,
        grid=(B, NC),
        in_specs=[pl.BlockSpec((1, C, PS_W), fwd_map), pl.BlockSpec((1, C, PS_W), bwd_map),
                  pl.BlockSpec((1, 1, C, G), lambda b, j: (0, b, j, 0)),
                  pl.BlockSpec((1, 1, C, G), lambda b, j: (1, b, bwd_chunk(j), 0)),
                  pl.BlockSpec((1, 1, C, PG_W), lambda b, j: (0, b, j, 0)),
                  pl.BlockSpec((1, 1, C, PG_W), lambda b, j: (1, b, bwd_chunk(j), 0)),
                  _full_spec((8, G))] + [_full_spec(a.shape) for a in consts],
        out_specs=[pl.BlockSpec((1, C, D_MODEL), fwd_map), pl.BlockSpec((1, C, D_MODEL), bwd_map)],
        out_shape=[jax.ShapeDtypeStruct((B, NT, D_MODEL), F32)] * 2,
        scratch_shapes=[pltpu.VMEM((2, G, G), F32), pltpu.VMEM((2, G, G), F32),
                        pltpu.VMEM((2, G, GLA_W), F32), pltpu.VMEM((2, G, G), F32),
                        pltpu.VMEM((2, 8, G), F32), pltpu.VMEM((2, 8, G), F32)],
        compiler_params=pltpu.CompilerParams(dimension_semantics=("parallel", "arbitrary"),
                                             vmem_limit_bytes=VMEM_LIMIT),
        name="mix",
    )
    return lambda ps, pk, pg, rt_par: call(ps, ps, pk, pk, pg, pg, rt_par, *consts)


def _out_kernel(final, n_src, n_ctx_tiles, t0, *refs):
    x_refs = refs[:n_src]
    (yf_ref, yb_ref, gate_ref, mod_ref, gh_ref, bd_ref,
     wo_ref, w1_ref, w2_ref, gf_ref, o_ref) = refs[n_src:]
    D = D_MODEL
    x = _load_tokens(x_refs, n_ctx_tiles, t0)
    bd = bd_ref[...]
    for g in range(D // GROUP_W):
        sl = slice(g * GROUP_W, (g + 1) * GROUP_W)
        y = yf_ref[0, :, sl] + yb_ref[0, :, sl]
        msq = _sel_right(y * y, bd) * (1.0 / HEAD_DIM)
        yn = y * lax.rsqrt(msq + RMS_EPS) * gh_ref[:, sl] * gate_ref[0, :, sl].astype(F32)
        part = _dot(yn.astype(BF16), wo_ref[sl, :])
        mixed = part if g == 0 else mixed + part
    g1 = mod_ref[0, 0, :, 2 * D:3 * D]
    sh2 = mod_ref[0, 0, :, 3 * D:4 * D]
    sc2 = mod_ref[0, 0, :, 4 * D:5 * D]
    g2 = mod_ref[0, 0, :, 5 * D:6 * D]
    x1 = x + g1 * mixed
    hin = ((x1 * _rms_scale(x1)) * (1.0 + sc2) + sh2).astype(BF16)
    hid = jnp.maximum(_dot(hin, w1_ref[...]), 0.0)
    ff = _dot((hid * hid).astype(BF16), w2_ref[...])
    x2 = x1 + g2 * ff
    if final:
        x2 = (x2 * _rms_scale(x2)) * gf_ref[...]
    o_ref[0] = x2


def _out(xs, yf, yb, gate, mods, g_heads, w_out, w_ff1, w_ff2, g_final, n_ctx_tiles, final):
    B, NT, D = yf.shape
    T = ROW_TILE
    t0 = n_ctx_tiles if final else 0
    n_tiles = NT // T - t0
    tok = lambda b, i: (b, i + t0, 0)
    bd = jnp.asarray(_NP["bd256"], BF16)
    wspec = lambda shape: pl.BlockSpec(shape, lambda b, i: (0, 0), pipeline_mode=pl.Buffered(1))
    return pl.pallas_call(
        functools.partial(_out_kernel, final, len(xs), n_ctx_tiles, t0),
        grid=(B, n_tiles),
        in_specs=_token_specs(xs, n_ctx_tiles, t0)
        + [pl.BlockSpec((1, T, D), tok),
                  pl.BlockSpec((1, T, D), tok), pl.BlockSpec((1, T, D), tok),
                  pl.BlockSpec((1, 1, 1, 6 * D),
                               lambda b, i: (b, jnp.where(i + t0 < n_ctx_tiles, 0, 1), 0, 0)),
                  pl.BlockSpec((1, D), lambda b, i: (0, 0)),
                  pl.BlockSpec((GROUP_W, GROUP_W), lambda b, i: (0, 0)),
                  wspec((D, D)), wspec((D, D_FF)), wspec((D_FF, D)),
                  pl.BlockSpec((1, D), lambda b, i: (0, 0))],
        out_specs=pl.BlockSpec((1, T, D), lambda b, i: (b, i, 0)),
        out_shape=jax.ShapeDtypeStruct((B, n_tiles * T, D), F32),
        compiler_params=pltpu.CompilerParams(dimension_semantics=("parallel", "parallel"),
                                             vmem_limit_bytes=VMEM_LIMIT),
        name="out_final" if final else "out",
    )(*xs, yf, yb, gate, mods, g_heads.reshape(1, D), bd, w_out, w_ff1, w_ff2,
      g_final.reshape(1, D))


_REF_LAYOUT = (('hg_q', 256), ('hg_f_fwd', 256), ('hg_f_bwd', 256), ('hg_i', 256), ('hg_g', 256),
               ('ml_q', 256), ('ml_k', 256), ('ml_v', 256), ('ml_if', 16), ('ml_o', 256),
               ('rt_q', 256), ('rt_k', 256), ('rt_v', 256), ('rt_g', 256),
               ('gl_q', 128), ('gl_k', 128), ('gl_v', 256), ('gl_a_fwd', 16), ('gl_a_bwd', 16),
               ('gl_g', 256))
_NEW_ORDER = ('hg_q', 'hg_f_fwd', 'hg_f_bwd', 'hg_i', 'ml_q', 'ml_k', 'ml_v', 'rt_q', 'rt_k', 'rt_v',
              'gl_q', 'gl_k', 'gl_v', 'ml_if', 'gl_a_fwd', 'gl_a_bwd', 'PAD', 'hg_g', 'ml_o', 'rt_g', 'gl_g')


def _reorder_columns(w):
    starts, off = {}, 0
    for name, width in _REF_LAYOUT:
        starts[name] = (off, width)
        off += width
    parts = []
    for name in _NEW_ORDER:
        if name == 'PAD':
            parts.append(jnp.zeros((w.shape[0], 128 - 48), BF16))
        else:
            a, width = starts[name]
            parts.append(w[:, a:a + width].astype(BF16))
    out = jnp.concatenate(parts, axis=1)
    assert out.shape[1] == IN_PAD
    return out


def _rope_tables(n_ctx, n_lat):
    n = jnp.arange(n_lat)
    r = (n // GRID_W).astype(F32)
    col = (n % GRID_W).astype(F32)
    n_freq = HEAD_DIM // 4
    inv = ROPE_BASE ** (-jnp.arange(n_freq, dtype=F32) / n_freq)
    ar = r[:, None] * inv[None, :]
    ac = col[:, None] * inv[None, :]
    cos = jnp.concatenate([jnp.cos(ar), jnp.cos(ar), jnp.cos(ac), jnp.cos(ac)], axis=-1)
    sin = jnp.concatenate([-jnp.sin(ar), jnp.sin(ar), -jnp.sin(ac), jnp.sin(ac)], axis=-1)
    cos = jnp.concatenate([jnp.ones((n_ctx, HEAD_DIM), F32), cos], axis=0)
    sin = jnp.concatenate([jnp.zeros((n_ctx, HEAD_DIM), F32), sin], axis=0)
    return jnp.tile(cos, (1, HEADS)), jnp.tile(sin, (1, HEADS))


def kernel(x, c, ctx, c_ctx, w_ada, b_ada, w_in, g_heads, hgrn_lb_logits, ml_gate_bias,
           rt_decay_logit, gla_w_a, gla_b_a, w_out, w_ff1, w_ff2, g_final):
    B, L, D = x.shape
    Lc = ctx.shape[1]
    depth = w_in.shape[0]
    assert D == D_MODEL and L % ROW_TILE == 0 and Lc % ROW_TILE == 0 and L % GRID_W == 0
    n_ctx_tiles = Lc // ROW_TILE
    n_ctx_chunks = Lc // CHUNK

    xs = (ctx, x)
    cos_t, sin_t = _rope_tables(Lc, L)
    mix = _mix_call(B, Lc + L, n_ctx_chunks)

    n_c = -(-(B + 1) // 8) * 8
    cvecs = jnp.zeros((n_c, D), F32).at[:B].set(c).at[B].set(c_ctx)

    sm = jax.nn.softmax(hgrn_lb_logits.astype(F32), axis=0)
    lb_all = jnp.maximum(jnp.cumsum(sm, axis=0) - sm[:1], 0.0)

    out = None
    for layer in range(depth):
        final = layer == depth - 1
        mod = _ada(cvecs, w_ada[layer], b_ada[layer])
        mods = jnp.stack([jnp.broadcast_to(mod[B], (B, 6 * D)), mod[:B]], axis=1)[:, :, None, :]

        w_in_p = _reorder_columns(w_in[layer])
        lb = lb_all[layer]
        hg_par = jnp.stack([jnp.log(lb[0]) * LOG2E, jnp.log1p(-lb[0]) * LOG2E, 1.0 - lb[0],
                            jnp.log(lb[1]) * LOG2E, jnp.log1p(-lb[1]) * LOG2E, 1.0 - lb[1],
                            jnp.zeros_like(lb[0]), jnp.zeros_like(lb[0])], axis=0)
        ml_bias = jnp.zeros((1, 128), F32).at[0, :16].set(ml_gate_bias[layer].astype(F32).reshape(16))
        wa = jnp.zeros((2, 128, GLA_W), F32)
        wa = wa.at[0, 16:32].set(gla_w_a[layer, 0].astype(F32)).at[1, 32:48].set(gla_w_a[layer, 1].astype(F32))
        wa_hi = wa.astype(BF16)
        wa_lo = (wa - wa_hi.astype(F32)).astype(BF16)
        ba = jnp.zeros((8, GLA_W), F32).at[:2].set(gla_b_a[layer].astype(F32))
        ps, pk, pg, gate = _inp(xs, mods, w_in_p, cos_t, sin_t, hg_par, ml_bias, wa_hi, wa_lo, ba,
                                n_ctx_tiles)

        rt_par = jnp.zeros((8, GROUP_W), F32).at[:2].set(
            jnp.repeat(jax.nn.log_sigmoid(rt_decay_logit[layer].astype(F32)) * LOG2E, HEAD_DIM, axis=-1))
        yf, yb = mix(ps, pk, pg, rt_par)

        res = _out(xs, yf, yb, gate, mods, g_heads[layer], w_out[layer].astype(BF16),
                   w_ff1[layer].astype(BF16), w_ff2[layer].astype(BF16), g_final,
                   n_ctx_tiles, final)
        if final:
            out = res
        else:
            xs = (res,)
    return out
```

```python
import functools

import numpy as np
import jax
import jax.numpy as jnp
from jax import lax
from jax.experimental import pallas as pl
from jax.experimental.pallas import tpu as pltpu

F32 = jnp.float32
BF16 = jnp.bfloat16

D_MODEL = 1024
GROUP_W = 256
HEADS = 4
HEAD_DIM = 64
GLA_DK = 32
GLA_W = HEADS * GLA_DK
GLA_TAU = 16.0
D_FF = 4 * D_MODEL
GRID_W = 64
ROPE_BASE = 10000.0
RMS_EPS = 1e-6

CHUNK = 64
STEP_CHUNKS = 4
ROW_TILE = 256
LEVELS = (32, 16, 8, 4, 2, 1)

W_SMALL = 3072
W_GATES = 3200
IN_PAD = W_GATES + D_MODEL

O_HG_Q, O_HG_V = 0, 256
O_ML_Q, O_ML_K, O_ML_V = 512, 768, 1024
O_RT_Q, O_RT_K, O_RT_V = 1280, 1536, 1792
O_GL_Q, O_GL_K, O_GL_V = 2048, 2176, 2304
PS_W = 2560
G_HG_LF, G_GL_LA, G_SMALL = 0, 256, 384
PG_W = 512

VMEM_LIMIT = 56 * 1024 * 1024

NEG_INF = float("-inf")
LOG2E = 1.4426950408889634


def _dot(a, b):
    return jnp.dot(a, b, preferred_element_type=F32)


def _dot_nt(a, b):
    return lax.dot_general(a, b, (((1,), (1,)), ((), ())), preferred_element_type=F32)


def _dot_tn(a, b):
    return lax.dot_general(a, b, (((0,), (0,)), ((), ())), preferred_element_type=F32)


def _split2(a):
    hi = a.astype(BF16)
    return hi, (a - hi.astype(F32)).astype(BF16)


def _cumsum_rows(x, d):
    n, width = x.shape
    tiles = n // SUBLANES
    x3 = x.reshape(tiles, SUBLANES, width)
    row = lax.broadcasted_iota(jnp.int32, (1, SUBLANES, width), 1)
    s = 1
    while s < SUBLANES:
        if d == 0:
            x3 = x3 + jnp.where(row >= s, pltpu.roll(x3, s, 1), 0.0)
        else:
            x3 = x3 + jnp.where(row < SUBLANES - s, pltpu.roll(x3, SUBLANES - s, 1), 0.0)
        s *= 2
    order = range(tiles) if d == 0 else range(tiles - 1, -1, -1)
    edge = SUBLANES - 1 if d == 0 else 0
    out = [None] * tiles
    carry = None
    for j in order:
        out[j] = x3[j] if carry is None else x3[j] + carry
        carry = out[j][edge:edge + 1]
    return jnp.concatenate(out, axis=0)


def _sel_right(a, m):
    hi, lo = _split2(a)
    return _dot(hi, m) + _dot(lo, m)


def _neg_abs(x):
    return pltpu.bitcast(pltpu.bitcast(x, jnp.uint32) | jnp.uint32(0x80000000), F32)


def _log2_sigmoid(z2):
    return jnp.minimum(z2, 0.0) - jnp.log2(1.0 + jnp.exp2(_neg_abs(z2)))


def _sigmoid(z):
    return 1.0 / (1.0 + jnp.exp(-z))


def _silu(z):
    return z * _sigmoid(z)


def _rms_scale(x):
    return lax.rsqrt(jnp.mean(x * x, axis=-1, keepdims=True) + RMS_EPS)


def _np_consts():
    C = CHUNK
    t = np.arange(C)
    T, U = np.meshgrid(t, t, indexing="ij")
    incl = [U <= T, U >= T]

    lev = np.zeros((2, len(LEVELS), C, 4 * C), np.float32)
    for d in range(2):
        for li, h in enumerate(LEVELS):
            same = (T // (2 * h)) == (U // (2 * h))
            if d == 0:
                pair = same & (T % (2 * h) >= h) & (U % (2 * h) < h)
            else:
                pair = same & (T % (2 * h) < h) & (U % (2 * h) >= h)
            lev[d, li] = np.tile(pair.astype(np.float32), (1, 4))

    incl_neg = np.stack([np.tile(np.where(incl[d], 0.0, NEG_INF).astype(np.float32), (1, 4))
                         for d in range(2)])
    eye = np.tile(np.eye(C, dtype=np.float32), (1, 4))
    dist = np.tile(np.abs(T - U).astype(np.float32), (1, 4))

    head_of_row = np.arange(4 * C) // C
    head_of_lane = np.arange(GROUP_W) // HEAD_DIM
    parity = np.where(head_of_row < 2, head_of_row, -1)
    hmask = np.concatenate(
        [(head_of_lane % 2)[None, :] == parity[:, None],
         head_of_row[:, None] == (np.arange(GLA_W) // GLA_DK)[None, :]], axis=1).astype(np.float32)
    bd256 = ((np.arange(GROUP_W) // HEAD_DIM)[:, None]
             == (np.arange(GROUP_W) // HEAD_DIM)[None, :]).astype(np.float32)
    bd_gl = ((np.arange(GROUP_W) // HEAD_DIM)[:, None]
             == (np.arange(GLA_W) // GLA_DK)[None, :]).astype(np.float32)

    sel_i = np.zeros((2, 128, GROUP_W), np.float32)
    sel_f = np.zeros((2, 128, GROUP_W), np.float32)
    for d in range(2):
        for h in range(HEADS):
            sel_i[d, d * 8 + h, h * HEAD_DIM:(h + 1) * HEAD_DIM] = 1.0
            sel_f[d, d * 8 + 4 + h, h * HEAD_DIM:(h + 1) * HEAD_DIM] = 1.0
    is_f = np.zeros((1, 128), np.float32)
    is_f[0, 4:8] = 1.0
    is_f[0, 12:16] = 1.0
    return dict(lev=lev, incl_neg=incl_neg, eye=eye, dist=dist,
                hmask=hmask, bd256=bd256, bd_gl=bd_gl,
                sel_i=sel_i, sel_f=sel_f, is_f=is_f)


_NP = _np_consts()


def _full_spec(shape):
    n = len(shape)
    return pl.BlockSpec(tuple(shape), lambda *_: (0,) * n)


def _ada_kernel(c_ref, w_ref, b_ref, o_ref):
    act = _silu(c_ref[...]).astype(BF16)
    o_ref[...] = _dot(act, w_ref[...].astype(BF16)) + b_ref[...]


def _ada(cvecs, w, b):
    rows = cvecs.shape[0]
    n = w.shape[1]
    tn = 1024
    return pl.pallas_call(
        _ada_kernel,
        grid=(n // tn,),
        in_specs=[pl.BlockSpec((rows, D_MODEL), lambda i: (0, 0)),
                  pl.BlockSpec((D_MODEL, tn), lambda i: (0, i)),
                  pl.BlockSpec((1, tn), lambda i: (0, i))],
        out_specs=pl.BlockSpec((rows, tn), lambda i: (0, i)),
        out_shape=jax.ShapeDtypeStruct((rows, n), F32),
        compiler_params=pltpu.CompilerParams(dimension_semantics=("arbitrary",),
                                             vmem_limit_bytes=VMEM_LIMIT),
        name="ada",
    )(cvecs, w, b.reshape(1, n))


def _rope(x, cos, sin):
    lane = lax.broadcasted_iota(jnp.int32, x.shape, 1)
    low = (lane % 32) < 16
    partner = jnp.where(low, pltpu.roll(x, GROUP_W - 16, 1), pltpu.roll(x, 16, 1))
    return x * cos + partner * sin


def _token_specs(xs, n_ctx_tiles, t0=0):
    T = ROW_TILE
    if len(xs) == 1:
        return [pl.BlockSpec((1, T, D_MODEL), lambda b, i: (b, i + t0, 0))]
    return [pl.BlockSpec((1, T, D_MODEL), lambda b, i: (b, jnp.minimum(i + t0, n_ctx_tiles - 1), 0)),
            pl.BlockSpec((1, T, D_MODEL), lambda b, i: (b, jnp.maximum(i + t0 - n_ctx_tiles, 0), 0))]


def _load_tokens(x_refs, n_ctx_tiles, t0=0):
    if len(x_refs) == 1:
        return x_refs[0][0]
    return jnp.where(pl.program_id(1) + t0 < n_ctx_tiles, x_refs[0][0], x_refs[1][0])


def _inp_kernel(n_src, n_ctx_tiles, *refs):
    x_refs = refs[:n_src]
    (mod_ref, w_ref, cos_ref, sin_ref, hg_ref, mlb_ref, isf_ref,
     wa_hi_ref, wa_lo_ref, ba_ref, ps_ref, pk_ref, pg_ref, gate_ref) = refs[n_src:]
    x = _load_tokens(x_refs, n_ctx_tiles)
    sh = mod_ref[0, 0, :, 0:D_MODEL]
    sc = mod_ref[0, 0, :, D_MODEL:2 * D_MODEL]
    xn = ((x * _rms_scale(x)) * (1.0 + sc) + sh).astype(BF16)

    def proj(a, b):
        return _dot(xn, w_ref[:, a:b])

    G = GROUP_W

    def put(off, width, fn=None):
        def post(res):
            ps_ref[0, :, off:off + width] = (res if fn is None else fn(res)).astype(BF16)
        return post

    def hgrn_decay(d):
        def post(z):
            log_lb = hg_ref[3 * d + 0:3 * d + 1, :]
            log_1mlb = hg_ref[3 * d + 1:3 * d + 2, :]
            one_m_lb = hg_ref[3 * d + 2:3 * d + 3, :]
            z2 = z * LOG2E
            ls = _log2_sigmoid(z2)
            other = log_1mlb + ls
            pg_ref[d, 0, :, G_HG_LF:G_HG_LF + G] = (
                jnp.maximum(log_lb, other) + jnp.log2(1.0 + jnp.exp2(_neg_abs(log_lb - other))))
            pk_ref[d, 0] = (one_m_lb * jnp.exp2(ls - z2)).astype(BF16)
        return post

    def narrow(small):
        small_hi, small_lo = _split2(small)
        pre2 = (small + mlb_ref[...]) * LOG2E
        gate_logs = jnp.where(isf_ref[...] > 0.5, _log2_sigmoid(pre2), pre2)
        for d in range(2):
            za = (_dot(small_hi, wa_hi_ref[d]) + _dot(small_lo, wa_hi_ref[d])
                  + _dot(small_hi, wa_lo_ref[d]) + ba_ref[d:d + 1, :])
            pg_ref[d, 0, :, G_GL_LA:G_GL_LA + GLA_W] = _log2_sigmoid(za * LOG2E) * (1.0 / GLA_TAU)
            pg_ref[d, 0, :, G_SMALL:PG_W] = gate_logs

    def gates(half, fn):
        def post(g):
            gate_ref[0, :, half * 2 * G:(half + 1) * 2 * G] = fn(g).astype(BF16)
        return post

    def rope(scale):
        return lambda r: _rope(r * scale if scale != 1.0 else r, cos_ref[...], sin_ref[...])

    o = 10 * G
    stages = [
        (W_GATES, W_GATES + 2 * G, gates(0, _sigmoid)),
        (0, G, put(O_HG_Q, G, _silu)),
        (G, 2 * G, hgrn_decay(0)),
        (3 * G, 5 * G, put(O_HG_V, 2 * G)),
        (2 * G, 3 * G, hgrn_decay(1)),
        (5 * G, 6 * G, put(O_ML_K, G, lambda r: r * (HEAD_DIM ** -0.5))),
        (W_GATES + 2 * G, IN_PAD, gates(1, _silu)),
        (6 * G, 7 * G, put(O_ML_V, G)),
        (7 * G, 8 * G, put(O_RT_Q, G, rope(1.0))),
        (9 * G, o + GLA_W, put(O_RT_V, G + GLA_W)),
        (8 * G, 9 * G, put(O_RT_K, G, rope(HEAD_DIM ** -0.5))),
        (o + GLA_W, o + 2 * GLA_W, put(O_GL_K, GLA_W, lambda r: r * (GLA_DK ** -0.5))),
        (W_SMALL, W_GATES, narrow),
        (o + 2 * GLA_W, W_SMALL, put(O_GL_V, G)),
    ]
    pending = None
    for a, b, post in stages:
        res = proj(a, b)
        if pending is not None:
            pending[1](pending[0])
        pending = (res, post)
    pending[1](pending[0])


def _inp(xs, mods, w_in_p, cos_t, sin_t, hg_par, ml_bias, wa_hi, wa_lo, ba, n_ctx_tiles):
    B = xs[0].shape[0]
    NT = sum(a.shape[1] for a in xs)
    T = ROW_TILE
    params = [hg_par, ml_bias, jnp.asarray(_NP["is_f"]), wa_hi, wa_lo, ba]
    return pl.pallas_call(
        functools.partial(_inp_kernel, len(xs), n_ctx_tiles),
        grid=(B, NT // T),
        in_specs=_token_specs(xs, n_ctx_tiles)
        + [pl.BlockSpec((1, 1, 1, 6 * D_MODEL),
                        lambda b, i: (b, jnp.where(i < n_ctx_tiles, 0, 1), 0, 0)),
                  pl.BlockSpec((D_MODEL, IN_PAD), lambda b, i: (0, 0)),
                  pl.BlockSpec((T, GROUP_W), lambda b, i: (i, 0)),
                  pl.BlockSpec((T, GROUP_W), lambda b, i: (i, 0))]
        + [_full_spec(a.shape) for a in params],
        out_specs=[pl.BlockSpec((1, T, PS_W), lambda b, i: (b, i, 0)),
                   pl.BlockSpec((2, 1, T, GROUP_W), lambda b, i: (0, b, i, 0)),
                   pl.BlockSpec((2, 1, T, PG_W), lambda b, i: (0, b, i, 0)),
                   pl.BlockSpec((1, T, D_MODEL), lambda b, i: (b, i, 0))],
        out_shape=[jax.ShapeDtypeStruct((B, NT, PS_W), BF16),
                   jax.ShapeDtypeStruct((2, B, NT, GROUP_W), BF16),
                   jax.ShapeDtypeStruct((2, B, NT, PG_W), F32),
                   jax.ShapeDtypeStruct((B, NT, D_MODEL), BF16)],
        compiler_params=pltpu.CompilerParams(dimension_semantics=("parallel", "parallel"),
                                             vmem_limit_bytes=VMEM_LIMIT),
        name="inp",
    )(*xs, mods, w_in_p, cos_t, sin_t, *params)


LANES = 128
SUBLANES = 8


def _head_blocks(a16, hmask_ref):
    C, W = a16.shape
    if W == LANES:
        return jnp.concatenate([a16, a16, a16, a16], axis=0) * hmask_ref[:, GROUP_W:GROUP_W + W]
    even = a16 * hmask_ref[0:C, 0:W]
    odd = a16 * hmask_ref[C:2 * C, 0:W]
    zero = jnp.zeros((C, LANES), a16.dtype)
    return jnp.concatenate([
        jnp.concatenate([even[:, 0:LANES], zero], axis=1),
        jnp.concatenate([odd[:, 0:LANES], zero], axis=1),
        jnp.concatenate([zero, even[:, LANES:W]], axis=1),
        jnp.concatenate([zero, odd[:, LANES:W]], axis=1)], axis=0)


def _last_row(a, d):
    return a[CHUNK - 1:CHUNK, :] if d == 0 else a[0:1, :]


def _level_exponent(d, h, log_a, b, row4):
    C, W = log_a.shape
    if h >= 4:
        b3 = b.reshape(C // (2 * h), 2 * h, W)
        r = h - 1 if d == 0 else h
        ref = jnp.broadcast_to(b3[:, r:r + 1, :], b3.shape).reshape(C, W)
        return _neg_abs(b - ref)
    up = pltpu.roll(log_a, C - 1, 0)
    dn = pltpu.roll(log_a, 1, 0)
    if d == 0:
        return jnp.where(row4 == 0, up, jnp.where(row4 == 1, 0.0,
                                                  log_a + jnp.where(row4 == 3, dn, 0.0)))
    return jnp.where(row4 == 0, log_a + up,
                     jnp.where(row4 == 1, log_a, jnp.where(row4 == 2, 0.0, dn)))


CHUNK_STAGGER = 3


def _interleave(chains):
    live = list(chains)
    rnd = 0
    while live:
        for item in list(live):
            start, g = item
            if rnd >= start:
                try:
                    next(g)
                except StopIteration:
                    live.remove(item)
        rnd += 1


def _update_state(st_ref, d, st, decay_row, bd_state, upd):
    rows = st.shape[0] // HEADS
    per_head = st.shape[1] // HEADS
    for h in range(HEADS):
        j = (h * per_head) // LANES
        rs = slice(h * rows, (h + 1) * rows)
        cs = slice(j * LANES, (j + 1) * LANES)
        st_ref[d, rs, cs] = decay_row[:, cs] * st[rs, cs] + bd_state[rs, cs] * upd[rs, cs]


def _vector_decay_chain(d, q16, k16, v16, log_a, y_ref, o_y, st_ref, lev_ref, eye,
                        hmask_ref, bd_state):
    C, W = log_a.shape
    G = GROUP_W
    b = _cumsum_rows(log_a, d)
    row4 = lax.broadcasted_iota(jnp.int32, (C, W), 0) & 3
    order = sorted(range(len(LEVELS)), key=lambda li: LEVELS[li] >= 4)
    attn = None
    pending = None
    for li in order:
        h = LEVELS[li]
        if h == 1:
            qa = jnp.concatenate([q16 * jnp.exp2(log_a).astype(BF16), q16], axis=0)
            kb = k16
        else:
            f16 = jnp.exp2(_level_exponent(d, h, log_a, b, row4)).astype(BF16)
            qa = q16 * f16
            kb = k16 * f16
        part = _dot_nt(qa, _head_blocks(kb, hmask_ref))
        if pending is not None:
            attn = pending if attn is None else attn + pending
        yield
        pending = part * lev_ref[d, li] if h > 1 else part[0:C] * lev_ref[d, li] + part[C:2 * C] * eye
    attn = attn + pending

    st = st_ref[d]
    b_end = _last_row(b, d)
    q_state = q16 * jnp.exp2(b).astype(BF16)
    k_end = k16 * jnp.exp2(b_end - b).astype(BF16)
    o = _dot(attn.astype(BF16), _head_blocks(v16, hmask_ref)) + _dot_nt(q_state, st.astype(BF16))
    upd = _dot_tn(v16, k_end)
    yield
    y_ref(o_y, o)
    _update_state(st_ref, d, st, jnp.exp2(b_end), bd_state, upd)


def _mlstm_chain(d, q16, k16, v16, small, y_ref, s_ml, n_ml, m_ml, incl_neg, eye,
                 seli, self_, hmask_ref, bd256, bd256_16):
    C = CHUNK
    G = GROUP_W
    i_bc = _sel_right(small, seli)
    b_col = _sel_right(_cumsum_rows(small, d), self_)
    qk = _dot_nt(q16, _head_blocks(k16, hmask_ref))
    yield
    b_row = jnp.sum(b_col * eye, axis=0, keepdims=True)
    i_row = jnp.sum(i_bc * eye, axis=0, keepdims=True)
    w = b_col - b_row + i_row + incl_neg
    row_max = [jnp.max(w[:, h * HEAD_DIM:(h + 1) * HEAD_DIM], axis=-1, keepdims=True)
               for h in range(HEADS)]
    yield
    m_prev = m_ml[d, 0:1, :]
    inter = b_col + m_prev
    lane_head = lax.broadcasted_iota(jnp.int32, (C, G), 1) // HEAD_DIM
    m_t = None
    for h in range(HEADS):
        mh = jnp.maximum(row_max[h], inter[:, h * HEAD_DIM:h * HEAD_DIM + 1])
        mh = jnp.broadcast_to(mh, (C, G))
        m_t = mh if m_t is None else jnp.where(lane_head == h, mh, m_t)
    g_in = jnp.exp2(inter - m_t)
    s = (qk * jnp.exp2(w - m_t)).astype(BF16)
    c_st = s_ml[d]
    n_row = n_ml[d, 0:1, :]
    qn = (q16.astype(F32) * n_row).astype(BF16)
    num_a = _dot(s, _head_blocks(v16, hmask_ref))
    num_b = _dot(q16, c_st.astype(BF16))
    den_ab = _dot(jnp.concatenate([s, qn], axis=0), bd256_16)
    m_new = _last_row(m_t, d)
    b_end = _last_row(b_col, d)
    a_state = jnp.exp2(b_end + m_prev - m_new)
    kw = k16.astype(F32) * jnp.exp2(b_end - b_col + i_bc - m_new)
    upd = _dot_tn(kw.astype(BF16), v16)
    yield
    num = num_a + g_in * num_b
    den = den_ab[0:C] + g_in * den_ab[C:2 * C]
    y_ref(G, num / jnp.maximum(jnp.abs(den), jnp.exp2(-m_t)))
    _update_state(s_ml, d, c_st, a_state, bd256, upd)
    n_ml[d] = jnp.broadcast_to(a_state * n_row + jnp.sum(kw, axis=0, keepdims=True), (8, G))
    m_ml[d] = jnp.broadcast_to(m_new, (8, G))


def _retention_chain(d, q16, k16, v16, y_ref, s_rt, log_g, dist, incl_neg, hmask_ref, bd256):
    C = CHUNK
    G = GROUP_W
    qk = _dot_nt(q16, _head_blocks(k16, hmask_ref))
    decay = jnp.exp2(log_g * dist + incl_neg)
    yield
    rowc = lax.broadcasted_iota(jnp.int32, (C, 1), 0).astype(F32)
    cnt = (rowc + 1.0) if d == 0 else (C - rowc)
    q_state = q16 * jnp.exp2(log_g * cnt).astype(BF16)
    k_end = k16 * jnp.exp2(log_g * (C - cnt)).astype(BF16)
    st = s_rt[d]
    o = _dot((qk * decay).astype(BF16), _head_blocks(v16, hmask_ref)) + _dot(q_state, st.astype(BF16))
    upd = _dot_tn(k_end, v16)
    yield
    y_ref(2 * G, o)
    _update_state(s_rt, d, st, jnp.exp2(log_g * float(C)), bd256, upd)


def _mix_kernel(psf_ref, psb_ref, pkf_ref, pkb_ref, pgf_ref, pgb_ref, rtg_ref,
                lev_ref, incl_ref, eye_ref, dist_ref,
                hmask_ref, bd256_ref, bdgl_ref, seli_ref, self_ref,
                yf_ref, yb_ref,
                s_hg, s_rt, s_gl, s_ml, n_ml, m_ml):
    C = CHUNK
    G = GROUP_W

    @pl.when(pl.program_id(1) == 0)
    def _():
        s_hg[...] = jnp.zeros_like(s_hg)
        s_rt[...] = jnp.zeros_like(s_rt)
        s_gl[...] = jnp.zeros_like(s_gl)
        s_ml[...] = jnp.zeros_like(s_ml)
        n_ml[...] = jnp.zeros_like(n_ml)
        m_ml[...] = jnp.zeros_like(m_ml)

    bd256 = bd256_ref[...]
    bd256_16 = bd256.astype(BF16)

    def deferred(make):
        yield from make()

    chains = []
    for i in range(STEP_CHUNKS):
        for d, (ps_ref, pk_ref, pg_ref, y_ref) in enumerate(((psf_ref, pkf_ref, pgf_ref, yf_ref),
                                                             (psb_ref, pkb_ref, pgb_ref, yb_ref))):
            sub = i if d == 0 else STEP_CHUNKS - 1 - i
            rows = slice(sub * C, (sub + 1) * C)

            def ps(off, width, ps_ref=ps_ref, rows=rows):
                return ps_ref[0, rows, off:off + width]

            def pg(off, width, pg_ref=pg_ref, rows=rows):
                return pg_ref[0, 0, rows, off:off + width]

            def put(off, val, y_ref=y_ref, rows=rows):
                y_ref[0, rows, off:off + G] = val

            def hgrn(d=d, ps=ps, pg=pg, put=put, pk_ref=pk_ref, rows=rows):
                return _vector_decay_chain(d, ps(O_HG_Q, G), pk_ref[0, 0, rows, :], ps(O_HG_V, G),
                                           pg(G_HG_LF, G), put, 0, s_hg, lev_ref, eye_ref[...],
                                           hmask_ref, bd256)

            def mlstm(d=d, ps=ps, pg=pg, put=put):
                return _mlstm_chain(d, ps(O_ML_Q, G), ps(O_ML_K, G), ps(O_ML_V, G),
                                    pg(G_SMALL, PG_W - G_SMALL), put, s_ml, n_ml, m_ml, incl_ref[d],
                                    eye_ref[...], seli_ref[d], self_ref[d], hmask_ref, bd256, bd256_16)

            def retention(d=d, ps=ps, put=put):
                return _retention_chain(d, ps(O_RT_Q, G), ps(O_RT_K, G), ps(O_RT_V, G), put, s_rt,
                                        rtg_ref[d:d + 1, :], dist_ref[...], incl_ref[d], hmask_ref, bd256)

            def gla(d=d, ps=ps, pg=pg, put=put):
                return _vector_decay_chain(d, ps(O_GL_Q, GLA_W), ps(O_GL_K, GLA_W), ps(O_GL_V, G),
                                           pg(G_GL_LA, GLA_W), put, 3 * G, s_gl, lev_ref, eye_ref[...],
                                           hmask_ref, bdgl_ref[...])

            chains += [(i * CHUNK_STAGGER, deferred(make)) for make in (hgrn, mlstm, retention, gla)]
    _interleave(chains)


def _mix_call(B, NT, n_ctx_chunks):
    C = CHUNK * STEP_CHUNKS
    assert NT % C == 0 and n_ctx_chunks % STEP_CHUNKS == 0
    NC = NT // C
    n_ctx_blocks = n_ctx_chunks // STEP_CHUNKS
    G = GROUP_W

    def bwd_chunk(j):
        return jnp.where(j < n_ctx_blocks, n_ctx_blocks - 1 - j, NC - 1 - (j - n_ctx_blocks))

    def fwd_map(b, j):
        return (b, j, 0)

    def bwd_map(b, j):
        return (b, bwd_chunk(j), 0)

    consts = [jnp.asarray(_NP["lev"]), jnp.asarray(_NP["incl_neg"]),
              jnp.asarray(_NP["eye"]), jnp.asarray(_NP["dist"]),
              jnp.asarray(_NP["hmask"], BF16),
              jnp.asarray(_NP["bd256"]), jnp.asarray(_NP["bd_gl"]),
              jnp.asarray(_NP["sel_i"], BF16), jnp.asarray(_NP["sel_f"], BF16)]
    call = pl.pallas_call(
        _mix_kernel,
        grid=(B, NC),
        in_specs=[pl.BlockSpec((1, C, PS_W), fwd_map), pl.BlockSpec((1, C, PS_W), bwd_map),
                  pl.BlockSpec((1, 1, C, G), lambda b, j: (0, b, j, 0)),
                  pl.BlockSpec((1, 1, C, G), lambda b, j: (1, b, bwd_chunk(j), 0)),
                  pl.BlockSpec((1, 1, C, PG_W), lambda b, j: (0, b, j, 0)),
                  pl.BlockSpec((1, 1, C, PG_W), lambda b, j: (1, b, bwd_chunk(j), 0)),
                  _full_spec((8, G))] + [_full_spec(a.shape) for a in consts],
        out_specs=[pl.BlockSpec((1, C, D_MODEL), fwd_map), pl.BlockSpec((1, C, D_MODEL), bwd_map)],
        out_shape=[jax.ShapeDtypeStruct((B, NT, D_MODEL), F32)] * 2,
        scratch_shapes=[pltpu.VMEM((2, G, G), F32), pltpu.VMEM((2, G, G), F32),
                        pltpu.VMEM((2, G, GLA_W), F32), pltpu.VMEM((2, G, G), F32),
                        pltpu.VMEM((2, 8, G), F32), pltpu.VMEM((2, 8, G), F32)],
        compiler_params=pltpu.CompilerParams(dimension_semantics=("parallel", "arbitrary"),
                                             vmem_limit_bytes=VMEM_LIMIT),
        name="mix",
    )
    return lambda ps, pk, pg, rt_par: call(ps, ps, pk, pk, pg, pg, rt_par, *consts)


def _out_kernel(final, n_src, n_ctx_tiles, t0, *refs):
    x_refs = refs[:n_src]
    (yf_ref, yb_ref, gate_ref, mod_ref, gh_ref, bd_ref,
     wo_ref, w1_ref, w2_ref, gf_ref, o_ref) = refs[n_src:]
    D = D_MODEL
    x = _load_tokens(x_refs, n_ctx_tiles, t0)
    bd = bd_ref[...]
    for g in range(D // GROUP_W):
        sl = slice(g * GROUP_W, (g + 1) * GROUP_W)
        y = yf_ref[0, :, sl] + yb_ref[0, :, sl]
        msq = _sel_right(y * y, bd) * (1.0 / HEAD_DIM)
        yn = y * lax.rsqrt(msq + RMS_EPS) * gh_ref[:, sl] * gate_ref[0, :, sl].astype(F32)
        part = _dot(yn.astype(BF16), wo_ref[sl, :])
        mixed = part if g == 0 else mixed + part
    g1 = mod_ref[0, 0, :, 2 * D:3 * D]
    sh2 = mod_ref[0, 0, :, 3 * D:4 * D]
    sc2 = mod_ref[0, 0, :, 4 * D:5 * D]
    g2 = mod_ref[0, 0, :, 5 * D:6 * D]
    x1 = x + g1 * mixed
    hin = ((x1 * _rms_scale(x1)) * (1.0 + sc2) + sh2).astype(BF16)
    hid = jnp.maximum(_dot(hin, w1_ref[...]), 0.0)
    ff = _dot((hid * hid).astype(BF16), w2_ref[...])
    x2 = x1 + g2 * ff
    if final:
        x2 = (x2 * _rms_scale(x2)) * gf_ref[...]
    o_ref[0] = x2


def _out(xs, yf, yb, gate, mods, g_heads, w_out, w_ff1, w_ff2, g_final, n_ctx_tiles, final):
    B, NT, D = yf.shape
    T = ROW_TILE
    t0 = n_ctx_tiles if final else 0
    n_tiles = NT // T - t0
    tok = lambda b, i: (b, i + t0, 0)
    bd = jnp.asarray(_NP["bd256"], BF16)
    wspec = lambda shape: pl.BlockSpec(shape, lambda b, i: (0, 0), pipeline_mode=pl.Buffered(1))
    return pl.pallas_call(
        functools.partial(_out_kernel, final, len(xs), n_ctx_tiles, t0),
        grid=(B, n_tiles),
        in_specs=_token_specs(xs, n_ctx_tiles, t0)
        + [pl.BlockSpec((1, T, D), tok),
                  pl.BlockSpec((1, T, D), tok), pl.BlockSpec((1, T, D), tok),
                  pl.BlockSpec((1, 1, 1, 6 * D),
                               lambda b, i: (b, jnp.where(i + t0 < n_ctx_tiles, 0, 1), 0, 0)),
                  pl.BlockSpec((1, D), lambda b, i: (0, 0)),
                  pl.BlockSpec((GROUP_W, GROUP_W), lambda b, i: (0, 0)),
                  wspec((D, D)), wspec((D, D_FF)), wspec((D_FF, D)),
                  pl.BlockSpec((1, D), lambda b, i: (0, 0))],
        out_specs=pl.BlockSpec((1, T, D), lambda b, i: (b, i, 0)),
        out_shape=jax.ShapeDtypeStruct((B, n_tiles * T, D), F32),
        compiler_params=pltpu.CompilerParams(dimension_semantics=("parallel", "parallel"),
                                             vmem_limit_bytes=VMEM_LIMIT),
        name="out_final" if final else "out",
    )(*xs, yf, yb, gate, mods, g_heads.reshape(1, D), bd, w_out, w_ff1, w_ff2,
      g_final.reshape(1, D))


_REF_LAYOUT = (('hg_q', 256), ('hg_f_fwd', 256), ('hg_f_bwd', 256), ('hg_i', 256), ('hg_g', 256),
               ('ml_q', 256), ('ml_k', 256), ('ml_v', 256), ('ml_if', 16), ('ml_o', 256),
               ('rt_q', 256), ('rt_k', 256), ('rt_v', 256), ('rt_g', 256),
               ('gl_q', 128), ('gl_k', 128), ('gl_v', 256), ('gl_a_fwd', 16), ('gl_a_bwd', 16),
               ('gl_g', 256))
_NEW_ORDER = ('hg_q', 'hg_f_fwd', 'hg_f_bwd', 'hg_i', 'ml_q', 'ml_k', 'ml_v', 'rt_q', 'rt_k', 'rt_v',
              'gl_q', 'gl_k', 'gl_v', 'ml_if', 'gl_a_fwd', 'gl_a_bwd', 'PAD', 'hg_g', 'ml_o', 'rt_g', 'gl_g')


def _reorder_columns(w):
    starts, off = {}, 0
    for name, width in _REF_LAYOUT:
        starts[name] = (off, width)
        off += width
    parts = []
    for name in _NEW_ORDER:
        if name == 'PAD':
            parts.append(jnp.zeros((w.shape[0], 128 - 48), BF16))
        else:
            a, width = starts[name]
            parts.append(w[:, a:a + width].astype(BF16))
    out = jnp.concatenate(parts, axis=1)
    assert out.shape[1] == IN_PAD
    return out


def _rope_tables(n_ctx, n_lat):
    n = jnp.arange(n_lat)
    r = (n // GRID_W).astype(F32)
    col = (n % GRID_W).astype(F32)
    n_freq = HEAD_DIM // 4
    inv = ROPE_BASE ** (-jnp.arange(n_freq, dtype=F32) / n_freq)
    ar = r[:, None] * inv[None, :]
    ac = col[:, None] * inv[None, :]
    cos = jnp.concatenate([jnp.cos(ar), jnp.cos(ar), jnp.cos(ac), jnp.cos(ac)], axis=-1)
    sin = jnp.concatenate([-jnp.sin(ar), jnp.sin(ar), -jnp.sin(ac), jnp.sin(ac)], axis=-1)
    cos = jnp.concatenate([jnp.ones((n_ctx, HEAD_DIM), F32), cos], axis=0)
    sin = jnp.concatenate([jnp.zeros((n_ctx, HEAD_DIM), F32), sin], axis=0)
    return jnp.tile(cos, (1, HEADS)), jnp.tile(sin, (1, HEADS))


def kernel(x, c, ctx, c_ctx, w_ada, b_ada, w_in, g_heads, hgrn_lb_logits, ml_gate_bias,
           rt_decay_logit, gla_w_a, gla_b_a, w_out, w_ff1, w_ff2, g_final):
    B, L, D = x.shape
    Lc = ctx.shape[1]
    depth = w_in.shape[0]
    assert D == D_MODEL and L % ROW_TILE == 0 and Lc % ROW_TILE == 0 and L % GRID_W == 0
    n_ctx_tiles = Lc // ROW_TILE
    n_ctx_chunks = Lc // CHUNK

    xs = (ctx, x)
    cos_t, sin_t = _rope_tables(Lc, L)
    mix = _mix_call(B, Lc + L, n_ctx_chunks)

    n_c = -(-(B + 1) // 8) * 8
    cvecs = jnp.zeros((n_c, D), F32).at[:B].set(c).at[B].set(c_ctx)

    sm = jax.nn.softmax(hgrn_lb_logits.astype(F32), axis=0)
    lb_all = jnp.maximum(jnp.cumsum(sm, axis=0) - sm[:1], 0.0)

    out = None
    for layer in range(depth):
        final = layer == depth - 1
        mod = _ada(cvecs, w_ada[layer], b_ada[layer])
        mods = jnp.stack([jnp.broadcast_to(mod[B], (B, 6 * D)), mod[:B]], axis=1)[:, :, None, :]

        w_in_p = _reorder_columns(w_in[layer])
        lb = lb_all[layer]
        hg_par = jnp.stack([jnp.log(lb[0]) * LOG2E, jnp.log1p(-lb[0]) * LOG2E, 1.0 - lb[0],
                            jnp.log(lb[1]) * LOG2E, jnp.log1p(-lb[1]) * LOG2E, 1.0 - lb[1],
                            jnp.zeros_like(lb[0]), jnp.zeros_like(lb[0])], axis=0)
        ml_bias = jnp.zeros((1, 128), F32).at[0, :16].set(ml_gate_bias[layer].astype(F32).reshape(16))
        wa = jnp.zeros((2, 128, GLA_W), F32)
        wa = wa.at[0, 16:32].set(gla_w_a[layer, 0].astype(F32)).at[1, 32:48].set(gla_w_a[layer, 1].astype(F32))
        wa_hi = wa.astype(BF16)
        wa_lo = (wa - wa_hi.astype(F32)).astype(BF16)
        ba = jnp.zeros((8, GLA_W), F32).at[:2].set(gla_b_a[layer].astype(F32))
        ps, pk, pg, gate = _inp(xs, mods, w_in_p, cos_t, sin_t, hg_par, ml_bias, wa_hi, wa_lo, ba,
                                n_ctx_tiles)

        rt_par = jnp.zeros((8, GROUP_W), F32).at[:2].set(
            jnp.repeat(jax.nn.log_sigmoid(rt_decay_logit[layer].astype(F32)) * LOG2E, HEAD_DIM, axis=-1))
        yf, yb = mix(ps, pk, pg, rt_par)

        res = _out(xs, yf, yb, gate, mods, g_heads[layer], w_out[layer].astype(BF16),
                   w_ff1[layer].astype(BF16), w_ff2[layer].astype(BF16), g_final,
                   n_ctx_tiles, final)
        if final:
            out = res
        else:
            xs = (res,)
    return out
```

```python
import functools

import numpy as np
import jax
import jax.numpy as jnp
from jax import lax
from jax.experimental import pallas as pl
from jax.experimental.pallas import tpu as pltpu

F32 = jnp.float32
BF16 = jnp.bfloat16

D_MODEL = 1024
GROUP_W = 256
HEADS = 4
HEAD_DIM = 64
GLA_DK = 32
GLA_W = HEADS * GLA_DK
GLA_TAU = 16.0
D_FF = 4 * D_MODEL
GRID_W = 64
ROPE_BASE = 10000.0
RMS_EPS = 1e-6

CHUNK = 64
STEP_CHUNKS = 4
ROW_TILE = 256
LEVELS = (32, 16, 8, 4, 2, 1)

W_SMALL = 3072
W_GATES = 3200
IN_PAD = W_GATES + D_MODEL

O_HG_Q, O_HG_V = 0, 256
O_ML_Q, O_ML_K, O_ML_V = 512, 768, 1024
O_RT_Q, O_RT_K, O_RT_V = 1280, 1536, 1792
O_GL_Q, O_GL_K, O_GL_V = 2048, 2176, 2304
PS_W = 2560
G_HG_LF, G_GL_LA, G_SMALL = 0, 256, 384
PG_W = 512

VMEM_LIMIT = 56 * 1024 * 1024

NEG_INF = float("-inf")
LOG2E = 1.4426950408889634


def _dot(a, b):
    return jnp.dot(a, b, preferred_element_type=F32)


def _dot_nt(a, b):
    return lax.dot_general(a, b, (((1,), (1,)), ((), ())), preferred_element_type=F32)


def _dot_tn(a, b):
    return lax.dot_general(a, b, (((0,), (0,)), ((), ())), preferred_element_type=F32)


def _split2(a):
    hi = a.astype(BF16)
    return hi, (a - hi.astype(F32)).astype(BF16)


def _cumsum_rows(x, d):
    n, width = x.shape
    tiles = n // SUBLANES
    x3 = x.reshape(tiles, SUBLANES, width)
    row = lax.broadcasted_iota(jnp.int32, (1, SUBLANES, width), 1)
    s = 1
    while s < SUBLANES:
        if d == 0:
            x3 = x3 + jnp.where(row >= s, pltpu.roll(x3, s, 1), 0.0)
        else:
            x3 = x3 + jnp.where(row < SUBLANES - s, pltpu.roll(x3, SUBLANES - s, 1), 0.0)
        s *= 2
    order = range(tiles) if d == 0 else range(tiles - 1, -1, -1)
    edge = SUBLANES - 1 if d == 0 else 0
    out = [None] * tiles
    carry = None
    for j in order:
        out[j] = x3[j] if carry is None else x3[j] + carry
        carry = out[j][edge:edge + 1]
    return jnp.concatenate(out, axis=0)


def _sel_right(a, m):
    hi, lo = _split2(a)
    return _dot(hi, m) + _dot(lo, m)


def _neg_abs(x):
    return pltpu.bitcast(pltpu.bitcast(x, jnp.uint32) | jnp.uint32(0x80000000), F32)


def _log2_sigmoid(z2):
    return jnp.minimum(z2, 0.0) - jnp.log2(1.0 + jnp.exp2(_neg_abs(z2)))


def _sigmoid(z):
    return 1.0 / (1.0 + jnp.exp(-z))


def _silu(z):
    return z * _sigmoid(z)


def _rms_scale(x):
    return lax.rsqrt(jnp.mean(x * x, axis=-1, keepdims=True) + RMS_EPS)


def _np_consts():
    C = CHUNK
    t = np.arange(C)
    T, U = np.meshgrid(t, t, indexing="ij")
    incl = [U <= T, U >= T]

    lev = np.zeros((2, len(LEVELS) + 1, C, 4 * C), np.float32)
    lev[:, len(LEVELS)] = np.tile(np.eye(C, dtype=np.float32), (1, 4))
    for d in range(2):
        for li, h in enumerate(LEVELS):
            same = (T // (2 * h)) == (U // (2 * h))
            if d == 0:
                pair = same & (T % (2 * h) >= h) & (U % (2 * h) < h)
            else:
                pair = same & (T % (2 * h) < h) & (U % (2 * h) >= h)
            lev[d, li] = np.tile(pair.astype(np.float32), (1, 4))

    incl_neg = np.stack([np.tile(np.where(incl[d], 0.0, NEG_INF).astype(np.float32), (1, 4))
                         for d in range(2)])
    eye = np.tile(np.eye(C, dtype=np.float32), (1, 4))
    dist = np.tile(np.abs(T - U).astype(np.float32), (1, 4))

    head_of_row = np.arange(4 * C) // C
    head_of_lane = np.arange(GROUP_W) // HEAD_DIM
    parity = np.where(head_of_row < 2, head_of_row, -1)
    hmask = np.concatenate(
        [(head_of_lane % 2)[None, :] == parity[:, None],
         head_of_row[:, None] == (np.arange(GLA_W) // GLA_DK)[None, :]], axis=1).astype(np.float32)
    bd256 = ((np.arange(GROUP_W) // HEAD_DIM)[:, None]
             == (np.arange(GROUP_W) // HEAD_DIM)[None, :]).astype(np.float32)
    bd_gl = ((np.arange(GROUP_W) // HEAD_DIM)[:, None]
             == (np.arange(GLA_W) // GLA_DK)[None, :]).astype(np.float32)

    sel_i = np.zeros((2, 128, GROUP_W), np.float32)
    sel_f = np.zeros((2, 128, GROUP_W), np.float32)
    for d in range(2):
        for h in range(HEADS):
            sel_i[d, d * 8 + h, h * HEAD_DIM:(h + 1) * HEAD_DIM] = 1.0
            sel_f[d, d * 8 + 4 + h, h * HEAD_DIM:(h + 1) * HEAD_DIM] = 1.0
    is_f = np.zeros((1, 128), np.float32)
    is_f[0, 4:8] = 1.0
    is_f[0, 12:16] = 1.0
    return dict(lev=lev, incl_neg=incl_neg, eye=eye, dist=dist,
                hmask=hmask, bd256=bd256, bd_gl=bd_gl,
                sel_i=sel_i, sel_f=sel_f, is_f=is_f)


_NP = _np_consts()


def _full_spec(shape):
    n = len(shape)
    return pl.BlockSpec(tuple(shape), lambda *_: (0,) * n)


def _ada_kernel(c_ref, w_ref, b_ref, o_ref):
    act = _silu(c_ref[...]).astype(BF16)
    o_ref[...] = _dot(act, w_ref[...].astype(BF16)) + b_ref[...]


def _ada(cvecs, w, b):
    rows = cvecs.shape[0]
    n = w.shape[1]
    tn = 1024
    return pl.pallas_call(
        _ada_kernel,
        grid=(n // tn,),
        in_specs=[pl.BlockSpec((rows, D_MODEL), lambda i: (0, 0)),
                  pl.BlockSpec((D_MODEL, tn), lambda i: (0, i)),
                  pl.BlockSpec((1, tn), lambda i: (0, i))],
        out_specs=pl.BlockSpec((rows, tn), lambda i: (0, i)),
        out_shape=jax.ShapeDtypeStruct((rows, n), F32),
        compiler_params=pltpu.CompilerParams(dimension_semantics=("arbitrary",),
                                             vmem_limit_bytes=VMEM_LIMIT),
        name="ada",
    )(cvecs, w, b.reshape(1, n))


def _rope(x, cos, sin):
    lane = lax.broadcasted_iota(jnp.int32, x.shape, 1)
    low = (lane % 32) < 16
    partner = jnp.where(low, pltpu.roll(x, GROUP_W - 16, 1), pltpu.roll(x, 16, 1))
    return x * cos + partner * sin


def _token_specs(xs, n_ctx_tiles, t0=0):
    T = ROW_TILE
    if len(xs) == 1:
        return [pl.BlockSpec((1, T, D_MODEL), lambda b, i: (b, i + t0, 0))]
    return [pl.BlockSpec((1, T, D_MODEL), lambda b, i: (b, jnp.minimum(i + t0, n_ctx_tiles - 1), 0)),
            pl.BlockSpec((1, T, D_MODEL), lambda b, i: (b, jnp.maximum(i + t0 - n_ctx_tiles, 0), 0))]


def _load_tokens(x_refs, n_ctx_tiles, t0=0):
    if len(x_refs) == 1:
        return x_refs[0][0]
    return jnp.where(pl.program_id(1) + t0 < n_ctx_tiles, x_refs[0][0], x_refs[1][0])


def _inp_kernel(n_src, n_ctx_tiles, *refs):
    x_refs = refs[:n_src]
    (mod_ref, w_ref, cos_ref, sin_ref, hg_ref, mlb_ref, isf_ref,
     wa_hi_ref, wa_lo_ref, ba_ref, ps_ref, pk_ref, pg_ref, gate_ref) = refs[n_src:]
    x = _load_tokens(x_refs, n_ctx_tiles)
    sh = mod_ref[0, 0, :, 0:D_MODEL]
    sc = mod_ref[0, 0, :, D_MODEL:2 * D_MODEL]
    xn = ((x * _rms_scale(x)) * (1.0 + sc) + sh).astype(BF16)

    def proj(a, b):
        return _dot(xn, w_ref[:, a:b])

    G = GROUP_W

    def put(off, width, fn=None):
        def post(res):
            ps_ref[0, :, off:off + width] = (res if fn is None else fn(res)).astype(BF16)
        return post

    def hgrn_decay(d):
        def post(z):
            log_lb = hg_ref[3 * d + 0:3 * d + 1, :]
            log_1mlb = hg_ref[3 * d + 1:3 * d + 2, :]
            one_m_lb = hg_ref[3 * d + 2:3 * d + 3, :]
            z2 = z * LOG2E
            ls = _log2_sigmoid(z2)
            other = log_1mlb + ls
            pg_ref[d, 0, :, G_HG_LF:G_HG_LF + G] = (
                jnp.maximum(log_lb, other) + jnp.log2(1.0 + jnp.exp2(_neg_abs(log_lb - other))))
            pk_ref[d, 0] = (one_m_lb * jnp.exp2(ls - z2)).astype(BF16)
        return post

    def narrow(small):
        small_hi, small_lo = _split2(small)
        pre2 = (small + mlb_ref[...]) * LOG2E
        gate_logs = jnp.where(isf_ref[...] > 0.5, _log2_sigmoid(pre2), pre2)
        for d in range(2):
            za = (_dot(small_hi, wa_hi_ref[d]) + _dot(small_lo, wa_hi_ref[d])
                  + _dot(small_hi, wa_lo_ref[d]) + ba_ref[d:d + 1, :])
            pg_ref[d, 0, :, G_GL_LA:G_GL_LA + GLA_W] = _log2_sigmoid(za * LOG2E) * (1.0 / GLA_TAU)
            pg_ref[d, 0, :, G_SMALL:PG_W] = gate_logs

    def gates(half, fn):
        def post(g):
            gate_ref[0, :, half * 2 * G:(half + 1) * 2 * G] = fn(g).astype(BF16)
        return post

    def rope(scale):
        return lambda r: _rope(r * scale if scale != 1.0 else r, cos_ref[...], sin_ref[...])

    o = 10 * G
    stages = [
        (W_GATES, W_GATES + 2 * G, gates(0, _sigmoid)),
        (0, G, put(O_HG_Q, G, _silu)),
        (G, 2 * G, hgrn_decay(0)),
        (3 * G, 5 * G, put(O_HG_V, 2 * G)),
        (2 * G, 3 * G, hgrn_decay(1)),
        (5 * G, 6 * G, put(O_ML_K, G, lambda r: r * (HEAD_DIM ** -0.5))),
        (W_GATES + 2 * G, IN_PAD, gates(1, _silu)),
        (6 * G, 7 * G, put(O_ML_V, G)),
        (7 * G, 8 * G, put(O_RT_Q, G, rope(1.0))),
        (9 * G, o + GLA_W, put(O_RT_V, G + GLA_W)),
        (8 * G, 9 * G, put(O_RT_K, G, rope(HEAD_DIM ** -0.5))),
        (o + GLA_W, o + 2 * GLA_W, put(O_GL_K, GLA_W, lambda r: r * (GLA_DK ** -0.5))),
        (W_SMALL, W_GATES, narrow),
        (o + 2 * GLA_W, W_SMALL, put(O_GL_V, G)),
    ]
    pending = None
    for a, b, post in stages:
        res = proj(a, b)
        if pending is not None:
            pending[1](pending[0])
        pending = (res, post)
    pending[1](pending[0])


def _inp(xs, mods, w_in_p, cos_t, sin_t, hg_par, ml_bias, wa_hi, wa_lo, ba, n_ctx_tiles):
    B = xs[0].shape[0]
    NT = sum(a.shape[1] for a in xs)
    T = ROW_TILE
    params = [hg_par, ml_bias, jnp.asarray(_NP["is_f"]), wa_hi, wa_lo, ba]
    return pl.pallas_call(
        functools.partial(_inp_kernel, len(xs), n_ctx_tiles),
        grid=(B, NT // T),
        in_specs=_token_specs(xs, n_ctx_tiles)
        + [pl.BlockSpec((1, 1, 1, 6 * D_MODEL),
                        lambda b, i: (b, jnp.where(i < n_ctx_tiles, 0, 1), 0, 0)),
                  pl.BlockSpec((D_MODEL, IN_PAD), lambda b, i: (0, 0)),
                  pl.BlockSpec((T, GROUP_W), lambda b, i: (i, 0)),
                  pl.BlockSpec((T, GROUP_W), lambda b, i: (i, 0))]
        + [_full_spec(a.shape) for a in params],
        out_specs=[pl.BlockSpec((1, T, PS_W), lambda b, i: (b, i, 0)),
                   pl.BlockSpec((2, 1, T, GROUP_W), lambda b, i: (0, b, i, 0)),
                   pl.BlockSpec((2, 1, T, PG_W), lambda b, i: (0, b, i, 0)),
                   pl.BlockSpec((1, T, D_MODEL), lambda b, i: (b, i, 0))],
        out_shape=[jax.ShapeDtypeStruct((B, NT, PS_W), BF16),
                   jax.ShapeDtypeStruct((2, B, NT, GROUP_W), BF16),
                   jax.ShapeDtypeStruct((2, B, NT, PG_W), F32),
                   jax.ShapeDtypeStruct((B, NT, D_MODEL), BF16)],
        compiler_params=pltpu.CompilerParams(dimension_semantics=("parallel", "parallel"),
                                             vmem_limit_bytes=VMEM_LIMIT),
        name="inp",
    )(*xs, mods, w_in_p, cos_t, sin_t, *params)


LANES = 128
SUBLANES = 8


def _head_blocks(a16, hmask_ref):
    C, W = a16.shape
    if W == LANES:
        return jnp.concatenate([a16, a16, a16, a16], axis=0) * hmask_ref[:, GROUP_W:GROUP_W + W]
    even = a16 * hmask_ref[0:C, 0:W]
    odd = a16 * hmask_ref[C:2 * C, 0:W]
    zero = jnp.zeros((C, LANES), a16.dtype)
    return jnp.concatenate([
        jnp.concatenate([even[:, 0:LANES], zero], axis=1),
        jnp.concatenate([odd[:, 0:LANES], zero], axis=1),
        jnp.concatenate([zero, even[:, LANES:W]], axis=1),
        jnp.concatenate([zero, odd[:, LANES:W]], axis=1)], axis=0)


def _last_row(a, d):
    return a[CHUNK - 1:CHUNK, :] if d == 0 else a[0:1, :]


def _level_exponent(d, h, log_a, b, row4):
    C, W = log_a.shape
    if h >= 4:
        b3 = b.reshape(C // (2 * h), 2 * h, W)
        r = h - 1 if d == 0 else h
        ref = jnp.broadcast_to(b3[:, r:r + 1, :], b3.shape).reshape(C, W)
        return _neg_abs(b - ref)
    up = pltpu.roll(log_a, C - 1, 0)
    dn = pltpu.roll(log_a, 1, 0)
    if d == 0:
        return jnp.where(row4 == 0, up, jnp.where(row4 == 1, 0.0,
                                                  log_a + jnp.where(row4 == 3, dn, 0.0)))
    return jnp.where(row4 == 0, log_a + up,
                     jnp.where(row4 == 1, log_a, jnp.where(row4 == 2, 0.0, dn)))


CHUNK_STAGGER = 3


def _interleave(chains):
    live = list(chains)
    rnd = 0
    while live:
        for item in list(live):
            start, g = item
            if rnd >= start:
                try:
                    next(g)
                except StopIteration:
                    live.remove(item)
        rnd += 1


def _update_state(st_ref, d, st, decay_row, bd_state, upd):
    rows = st.shape[0] // HEADS
    per_head = st.shape[1] // HEADS
    for h in range(HEADS):
        j = (h * per_head) // LANES
        rs = slice(h * rows, (h + 1) * rows)
        cs = slice(j * LANES, (j + 1) * LANES)
        st_ref[d, rs, cs] = decay_row[:, cs] * st[rs, cs] + bd_state[rs, cs] * upd[rs, cs]


def _vector_decay_chain(d, q16, k16, v16, log_a, y_ref, o_y, st_ref, lev_ref, hmask_ref, bd_state):
    C, W = log_a.shape
    G = GROUP_W
    b = _cumsum_rows(log_a, d)
    row4 = lax.broadcasted_iota(jnp.int32, (C, W), 0) & 3
    order = sorted(range(len(LEVELS)), key=lambda li: LEVELS[li] >= 4)
    attn = None
    pending = None
    for li in order:
        h = LEVELS[li]
        if h == 1:
            qa = jnp.concatenate([q16 * jnp.exp2(log_a).astype(BF16), q16], axis=0)
            kb = k16
        else:
            f16 = jnp.exp2(_level_exponent(d, h, log_a, b, row4)).astype(BF16)
            qa = q16 * f16
            kb = k16 * f16
        part = _dot_nt(qa, _head_blocks(kb, hmask_ref))
        if pending is not None:
            attn = pending if attn is None else attn + pending
        yield
        part = part.astype(BF16)
        pending = (part * lev_ref[d, li] if h > 1 else
                   part[0:C] * lev_ref[d, li] + part[C:2 * C] * lev_ref[d, len(LEVELS)])
    attn = attn + pending

    st = st_ref[d]
    b_end = _last_row(b, d)
    q_state = q16 * jnp.exp2(b).astype(BF16)
    k_end = k16 * jnp.exp2(b_end - b).astype(BF16)
    o = _dot(attn, _head_blocks(v16, hmask_ref)) + _dot_nt(q_state, st.astype(BF16))
    upd = _dot_tn(v16, k_end)
    yield
    y_ref(o_y, o)
    _update_state(st_ref, d, st, jnp.exp2(b_end), bd_state, upd)


def _mlstm_chain(d, q16, k16, v16, small, y_ref, s_ml, n_ml, m_ml, incl_neg, eye,
                 seli, self_, hmask_ref, bd256, bd256_16):
    C = CHUNK
    G = GROUP_W
    i_bc = _sel_right(small, seli)
    b_col = _sel_right(_cumsum_rows(small, d), self_)
    qk = _dot_nt(q16, _head_blocks(k16, hmask_ref))
    yield
    b_row = jnp.sum(b_col * eye, axis=0, keepdims=True)
    i_row = jnp.sum(i_bc * eye, axis=0, keepdims=True)
    w = b_col - b_row + i_row + incl_neg
    row_max = [jnp.max(w[:, h * HEAD_DIM:(h + 1) * HEAD_DIM], axis=-1, keepdims=True)
               for h in range(HEADS)]
    yield
    m_prev = m_ml[d, 0:1, :]
    inter = b_col + m_prev
    lane_head = lax.broadcasted_iota(jnp.int32, (C, G), 1) // HEAD_DIM
    m_t = None
    for h in range(HEADS):
        mh = jnp.maximum(row_max[h], inter[:, h * HEAD_DIM:h * HEAD_DIM + 1])
        mh = jnp.broadcast_to(mh, (C, G))
        m_t = mh if m_t is None else jnp.where(lane_head == h, mh, m_t)
    g_in = jnp.exp2(inter - m_t)
    s = (qk * jnp.exp2(w - m_t)).astype(BF16)
    c_st = s_ml[d]
    n_row = n_ml[d, 0:1, :]
    qn = (q16.astype(F32) * n_row).astype(BF16)
    num_a = _dot(s, _head_blocks(v16, hmask_ref))
    num_b = _dot(q16, c_st.astype(BF16))
    den_ab = _dot(jnp.concatenate([s, qn], axis=0), bd256_16)
    m_new = _last_row(m_t, d)
    b_end = _last_row(b_col, d)
    a_state = jnp.exp2(b_end + m_prev - m_new)
    kw = k16.astype(F32) * jnp.exp2(b_end - b_col + i_bc - m_new)
    upd = _dot_tn(kw.astype(BF16), v16)
    yield
    num = num_a + g_in * num_b
    den = den_ab[0:C] + g_in * den_ab[C:2 * C]
    y_ref(G, num / jnp.maximum(jnp.abs(den), jnp.exp2(-m_t)))
    _update_state(s_ml, d, c_st, a_state, bd256, upd)
    n_ml[d] = jnp.broadcast_to(a_state * n_row + jnp.sum(kw, axis=0, keepdims=True), (8, G))
    m_ml[d] = jnp.broadcast_to(m_new, (8, G))


def _retention_chain(d, q16, k16, v16, y_ref, s_rt, log_g, dist, incl_neg, hmask_ref, bd256):
    C = CHUNK
    G = GROUP_W
    qk = _dot_nt(q16, _head_blocks(k16, hmask_ref))
    decay = jnp.exp2(log_g * dist + incl_neg)
    yield
    rowc = lax.broadcasted_iota(jnp.int32, (C, 1), 0).astype(F32)
    cnt = (rowc + 1.0) if d == 0 else (C - rowc)
    q_state = q16 * jnp.exp2(log_g * cnt).astype(BF16)
    k_end = k16 * jnp.exp2(log_g * (C - cnt)).astype(BF16)
    st = s_rt[d]
    o = _dot((qk * decay).astype(BF16), _head_blocks(v16, hmask_ref)) + _dot(q_state, st.astype(BF16))
    upd = _dot_tn(k_end, v16)
    yield
    y_ref(2 * G, o)
    _update_state(s_rt, d, st, jnp.exp2(log_g * float(C)), bd256, upd)


def _mix_kernel(psf_ref, psb_ref, pkf_ref, pkb_ref, pgf_ref, pgb_ref, rtg_ref,
                lev_ref, incl_ref, eye_ref, dist_ref,
                hmask_ref, bd256_ref, bdgl_ref, seli_ref, self_ref,
                yf_ref, yb_ref,
                s_hg, s_rt, s_gl, s_ml, n_ml, m_ml):
    C = CHUNK
    G = GROUP_W

    @pl.when(pl.program_id(1) == 0)
    def _():
        s_hg[...] = jnp.zeros_like(s_hg)
        s_rt[...] = jnp.zeros_like(s_rt)
        s_gl[...] = jnp.zeros_like(s_gl)
        s_ml[...] = jnp.zeros_like(s_ml)
        n_ml[...] = jnp.zeros_like(n_ml)
        m_ml[...] = jnp.zeros_like(m_ml)

    bd256 = bd256_ref[...]
    bd256_16 = bd256.astype(BF16)

    def deferred(make):
        yield from make()

    chains = []
    for i in range(STEP_CHUNKS):
        for d, (ps_ref, pk_ref, pg_ref, y_ref) in enumerate(((psf_ref, pkf_ref, pgf_ref, yf_ref),
                                                             (psb_ref, pkb_ref, pgb_ref, yb_ref))):
            sub = i if d == 0 else STEP_CHUNKS - 1 - i
            rows = slice(sub * C, (sub + 1) * C)

            def ps(off, width, ps_ref=ps_ref, rows=rows):
                return ps_ref[0, rows, off:off + width]

            def pg(off, width, pg_ref=pg_ref, rows=rows):
                return pg_ref[0, 0, rows, off:off + width]

            def put(off, val, y_ref=y_ref, rows=rows):
                y_ref[0, rows, off:off + G] = val

            def hgrn(d=d, ps=ps, pg=pg, put=put, pk_ref=pk_ref, rows=rows):
                return _vector_decay_chain(d, ps(O_HG_Q, G), pk_ref[0, 0, rows, :], ps(O_HG_V, G),
                                           pg(G_HG_LF, G), put, 0, s_hg, lev_ref, hmask_ref, bd256)

            def mlstm(d=d, ps=ps, pg=pg, put=put):
                return _mlstm_chain(d, ps(O_ML_Q, G), ps(O_ML_K, G), ps(O_ML_V, G),
                                    pg(G_SMALL, PG_W - G_SMALL), put, s_ml, n_ml, m_ml, incl_ref[d],
                                    eye_ref[...], seli_ref[d], self_ref[d], hmask_ref, bd256, bd256_16)

            def retention(d=d, ps=ps, put=put):
                return _retention_chain(d, ps(O_RT_Q, G), ps(O_RT_K, G), ps(O_RT_V, G), put, s_rt,
                                        rtg_ref[d:d + 1, :], dist_ref[...], incl_ref[d], hmask_ref, bd256)

            def gla(d=d, ps=ps, pg=pg, put=put):
                return _vector_decay_chain(d, ps(O_GL_Q, GLA_W), ps(O_GL_K, GLA_W), ps(O_GL_V, G),
                                           pg(G_GL_LA, GLA_W), put, 3 * G, s_gl, lev_ref, hmask_ref,
                                           bdgl_ref[...])

            chains += [(i * CHUNK_STAGGER, deferred(make)) for make in (hgrn, mlstm, retention, gla)]
    _interleave(chains)


def _mix_call(B, NT, n_ctx_chunks):
    C = CHUNK * STEP_CHUNKS
    assert NT % C == 0 and n_ctx_chunks % STEP_CHUNKS == 0
    NC = NT // C
    n_ctx_blocks = n_ctx_chunks // STEP_CHUNKS
    G = GROUP_W

    def bwd_chunk(j):
        return jnp.where(j < n_ctx_blocks, n_ctx_blocks - 1 - j, NC - 1 - (j - n_ctx_blocks))

    def fwd_map(b, j):
        return (b, j, 0)

    def bwd_map(b, j):
        return (b, bwd_chunk(j), 0)

    consts = [jnp.asarray(_NP["lev"], BF16), jnp.asarray(_NP["incl_neg"]),
              jnp.asarray(_NP["eye"]), jnp.asarray(_NP["dist"]),
              jnp.asarray(_NP["hmask"], BF16),
              jnp.asarray(_NP["bd256"]), jnp.asarray(_NP["bd_gl"]),
              jnp.asarray(_NP["sel_i"], BF16), jnp.asarray(_NP["sel_f"], BF16)]
    call = pl.pallas_call(
        _mix_kernel,
        grid=(B, NC),
        in_specs=[pl.BlockSpec((1, C, PS_W), fwd_map), pl.BlockSpec((1, C, PS_W), bwd_map),
                  pl.BlockSpec((1, 1, C, G), lambda b, j: (0, b, j, 0)),
                  pl.BlockSpec((1, 1, C, G), lambda b, j: (1, b, bwd_chunk(j), 0)),
                  pl.BlockSpec((1, 1, C, PG_W), lambda b, j: (0, b, j, 0)),
                  pl.BlockSpec((1, 1, C, PG_W), lambda b, j: (1, b, bwd_chunk(j), 0)),
                  _full_spec((8, G))] + [_full_spec(a.shape) for a in consts],
        out_specs=[pl.BlockSpec((1, C, D_MODEL), fwd_map), pl.BlockSpec((1, C, D_MODEL), bwd_map)],
        out_shape=[jax.ShapeDtypeStruct((B, NT, D_MODEL), F32)] * 2,
        scratch_shapes=[pltpu.VMEM((2, G, G), F32), pltpu.VMEM((2, G, G), F32),
                        pltpu.VMEM((2, G, GLA_W), F32), pltpu.VMEM((2, G, G), F32),
                        pltpu.VMEM((2, 8, G), F32), pltpu.VMEM((2, 8, G), F32)],
        compiler_params=pltpu.CompilerParams(dimension_semantics=("parallel", "arbitrary"),
                                             vmem_limit_bytes=VMEM_LIMIT),
        name="mix",
    )
    return lambda ps, pk, pg, rt_par: call(ps, ps, pk, pk, pg, pg, rt_par, *consts)


def _out_kernel(final, n_src, n_ctx_tiles, t0, *refs):
    x_refs = refs[:n_src]
    (yf_ref, yb_ref, gate_ref, mod_ref, gh_ref, bd_ref,
     wo_ref, w1_ref, w2_ref, gf_ref, o_ref) = refs[n_src:]
    D = D_MODEL
    x = _load_tokens(x_refs, n_ctx_tiles, t0)
    bd = bd_ref[...]
    for g in range(D // GROUP_W):
        sl = slice(g * GROUP_W, (g + 1) * GROUP_W)
        y = yf_ref[0, :, sl] + yb_ref[0, :, sl]
        msq = _dot((y * y).astype(BF16), bd) * (1.0 / HEAD_DIM)
        yn = y * lax.rsqrt(msq + RMS_EPS) * gh_ref[:, sl] * gate_ref[0, :, sl].astype(F32)
        part = _dot(yn.astype(BF16), wo_ref[sl, :])
        mixed = part if g == 0 else mixed + part
    g1 = mod_ref[0, 0, :, 2 * D:3 * D]
    sh2 = mod_ref[0, 0, :, 3 * D:4 * D]
    sc2 = mod_ref[0, 0, :, 4 * D:5 * D]
    g2 = mod_ref[0, 0, :, 5 * D:6 * D]
    x1 = x + g1 * mixed
    hin = ((x1 * _rms_scale(x1)) * (1.0 + sc2) + sh2).astype(BF16)
    hid = jnp.maximum(_dot(hin, w1_ref[...]), 0.0)
    ff = _dot((hid * hid).astype(BF16), w2_ref[...])
    x2 = x1 + g2 * ff
    if final:
        x2 = (x2 * _rms_scale(x2)) * gf_ref[...]
    o_ref[0] = x2


def _out(xs, yf, yb, gate, mods, g_heads, w_out, w_ff1, w_ff2, g_final, n_ctx_tiles, final):
    B, NT, D = yf.shape
    T = ROW_TILE
    t0 = n_ctx_tiles if final else 0
    n_tiles = NT // T - t0
    tok = lambda b, i: (b, i + t0, 0)
    bd = jnp.asarray(_NP["bd256"], BF16)
    wspec = lambda shape: pl.BlockSpec(shape, lambda b, i: (0, 0), pipeline_mode=pl.Buffered(1))
    return pl.pallas_call(
        functools.partial(_out_kernel, final, len(xs), n_ctx_tiles, t0),
        grid=(B, n_tiles),
        in_specs=_token_specs(xs, n_ctx_tiles, t0)
        + [pl.BlockSpec((1, T, D), tok),
                  pl.BlockSpec((1, T, D), tok), pl.BlockSpec((1, T, D), tok),
                  pl.BlockSpec((1, 1, 1, 6 * D),
                               lambda b, i: (b, jnp.where(i + t0 < n_ctx_tiles, 0, 1), 0, 0)),
                  pl.BlockSpec((1, D), lambda b, i: (0, 0)),
                  pl.BlockSpec((GROUP_W, GROUP_W), lambda b, i: (0, 0)),
                  wspec((D, D)), wspec((D, D_FF)), wspec((D_FF, D)),
                  pl.BlockSpec((1, D), lambda b, i: (0, 0))],
        out_specs=pl.BlockSpec((1, T, D), lambda b, i: (b, i, 0)),
        out_shape=jax.ShapeDtypeStruct((B, n_tiles * T, D), F32),
        compiler_params=pltpu.CompilerParams(dimension_semantics=("parallel", "parallel"),
                                             vmem_limit_bytes=VMEM_LIMIT),
        name="out_final" if final else "out",
    )(*xs, yf, yb, gate, mods, g_heads.reshape(1, D), bd, w_out, w_ff1, w_ff2,
      g_final.reshape(1, D))


_REF_LAYOUT = (('hg_q', 256), ('hg_f_fwd', 256), ('hg_f_bwd', 256), ('hg_i', 256), ('hg_g', 256),
               ('ml_q', 256), ('ml_k', 256), ('ml_v', 256), ('ml_if', 16), ('ml_o', 256),
               ('rt_q', 256), ('rt_k', 256), ('rt_v', 256), ('rt_g', 256),
               ('gl_q', 128), ('gl_k', 128), ('gl_v', 256), ('gl_a_fwd', 16), ('gl_a_bwd', 16),
               ('gl_g', 256))
_NEW_ORDER = ('hg_q', 'hg_f_fwd', 'hg_f_bwd', 'hg_i', 'ml_q', 'ml_k', 'ml_v', 'rt_q', 'rt_k', 'rt_v',
              'gl_q', 'gl_k', 'gl_v', 'ml_if', 'gl_a_fwd', 'gl_a_bwd', 'PAD', 'hg_g', 'ml_o', 'rt_g', 'gl_g')


def _reorder_columns(w):
    starts, off = {}, 0
    for name, width in _REF_LAYOUT:
        starts[name] = (off, width)
        off += width
    parts = []
    for name in _NEW_ORDER:
        if name == 'PAD':
            parts.append(jnp.zeros((w.shape[0], 128 - 48), BF16))
        else:
            a, width = starts[name]
            parts.append(w[:, a:a + width].astype(BF16))
    out = jnp.concatenate(parts, axis=1)
    assert out.shape[1] == IN_PAD
    return out


def _rope_tables(n_ctx, n_lat):
    n = jnp.arange(n_lat)
    r = (n // GRID_W).astype(F32)
    col = (n % GRID_W).astype(F32)
    n_freq = HEAD_DIM // 4
    inv = ROPE_BASE ** (-jnp.arange(n_freq, dtype=F32) / n_freq)
    ar = r[:, None] * inv[None, :]
    ac = col[:, None] * inv[None, :]
    cos = jnp.concatenate([jnp.cos(ar), jnp.cos(ar), jnp.cos(ac), jnp.cos(ac)], axis=-1)
    sin = jnp.concatenate([-jnp.sin(ar), jnp.sin(ar), -jnp.sin(ac), jnp.sin(ac)], axis=-1)
    cos = jnp.concatenate([jnp.ones((n_ctx, HEAD_DIM), F32), cos], axis=0)
    sin = jnp.concatenate([jnp.zeros((n_ctx, HEAD_DIM), F32), sin], axis=0)
    return jnp.tile(cos, (1, HEADS)), jnp.tile(sin, (1, HEADS))


def kernel(x, c, ctx, c_ctx, w_ada, b_ada, w_in, g_heads, hgrn_lb_logits, ml_gate_bias,
           rt_decay_logit, gla_w_a, gla_b_a, w_out, w_ff1, w_ff2, g_final):
    B, L, D = x.shape
    Lc = ctx.shape[1]
    depth = w_in.shape[0]
    assert D == D_MODEL and L % ROW_TILE == 0 and Lc % ROW_TILE == 0 and L % GRID_W == 0
    n_ctx_tiles = Lc // ROW_TILE
    n_ctx_chunks = Lc // CHUNK

    xs = (ctx, x)
    cos_t, sin_t = _rope_tables(Lc, L)
    mix = _mix_call(B, Lc + L, n_ctx_chunks)

    n_c = -(-(B + 1) // 8) * 8
    cvecs = jnp.zeros((n_c, D), F32).at[:B].set(c).at[B].set(c_ctx)

    sm = jax.nn.softmax(hgrn_lb_logits.astype(F32), axis=0)
    lb_all = jnp.maximum(jnp.cumsum(sm, axis=0) - sm[:1], 0.0)

    out = None
    for layer in range(depth):
        final = layer == depth - 1
        mod = _ada(cvecs, w_ada[layer], b_ada[layer])
        mods = jnp.stack([jnp.broadcast_to(mod[B], (B, 6 * D)), mod[:B]], axis=1)[:, :, None, :]

        w_in_p = _reorder_columns(w_in[layer])
        lb = lb_all[layer]
        hg_par = jnp.stack([jnp.log(lb[0]) * LOG2E, jnp.log1p(-lb[0]) * LOG2E, 1.0 - lb[0],
                            jnp.log(lb[1]) * LOG2E, jnp.log1p(-lb[1]) * LOG2E, 1.0 - lb[1],
                            jnp.zeros_like(lb[0]), jnp.zeros_like(lb[0])], axis=0)
        ml_bias = jnp.zeros((1, 128), F32).at[0, :16].set(ml_gate_bias[layer].astype(F32).reshape(16))
        wa = jnp.zeros((2, 128, GLA_W), F32)
        wa = wa.at[0, 16:32].set(gla_w_a[layer, 0].astype(F32)).at[1, 32:48].set(gla_w_a[layer, 1].astype(F32))
        wa_hi = wa.astype(BF16)
        wa_lo = (wa - wa_hi.astype(F32)).astype(BF16)
        ba = jnp.zeros((8, GLA_W), F32).at[:2].set(gla_b_a[layer].astype(F32))
        ps, pk, pg, gate = _inp(xs, mods, w_in_p, cos_t, sin_t, hg_par, ml_bias, wa_hi, wa_lo, ba,
                                n_ctx_tiles)

        rt_par = jnp.zeros((8, GROUP_W), F32).at[:2].set(
            jnp.repeat(jax.nn.log_sigmoid(rt_decay_logit[layer].astype(F32)) * LOG2E, HEAD_DIM, axis=-1))
        yf, yb = mix(ps, pk, pg, rt_par)

        res = _out(xs, yf, yb, gate, mods, g_heads[layer], w_out[layer].astype(BF16),
                   w_ff1[layer].astype(BF16), w_ff2[layer].astype(BF16), g_final,
                   n_ctx_tiles, final)
        if final:
            out = res
        else:
            xs = (res,)
    return out
```

```python
import functools

import numpy as np
import jax
import jax.numpy as jnp
from jax import lax
from jax.experimental import pallas as pl
from jax.experimental.pallas import tpu as pltpu

F32 = jnp.float32
BF16 = jnp.bfloat16

D_MODEL = 1024
GROUP_W = 256
HEADS = 4
HEAD_DIM = 64
GLA_DK = 32
GLA_W = HEADS * GLA_DK
GLA_TAU = 16.0
D_FF = 4 * D_MODEL
GRID_W = 64
ROPE_BASE = 10000.0
RMS_EPS = 1e-6

CHUNK = 64
STEP_CHUNKS = 4
ROW_TILE = 256
LEVELS = (32, 16, 8, 4, 2, 1)

W_SMALL = 3072
W_GATES = 3200
IN_PAD = W_GATES + D_MODEL

O_HG_Q, O_HG_V = 0, 256
O_ML_Q, O_ML_K, O_ML_V = 512, 768, 1024
O_RT_Q, O_RT_K, O_RT_V = 1280, 1536, 1792
O_GL_Q, O_GL_K, O_GL_V = 2048, 2176, 2304
PS_W = 2560
G_HG_LF, G_GL_LA, G_SMALL = 0, 256, 384
PG_W = 512

VMEM_LIMIT = 56 * 1024 * 1024

NEG_INF = float("-inf")
LOG2E = 1.4426950408889634


def _dot(a, b):
    return jnp.dot(a, b, preferred_element_type=F32)


def _dot_nt(a, b):
    return lax.dot_general(a, b, (((1,), (1,)), ((), ())), preferred_element_type=F32)


def _dot_tn(a, b):
    return lax.dot_general(a, b, (((0,), (0,)), ((), ())), preferred_element_type=F32)


def _split2(a):
    hi = a.astype(BF16)
    return hi, (a - hi.astype(F32)).astype(BF16)


def _cumsum_rows(x, d):
    n, width = x.shape
    tiles = n // SUBLANES
    x3 = x.reshape(tiles, SUBLANES, width)
    row = lax.broadcasted_iota(jnp.int32, (1, SUBLANES, width), 1)
    s = 1
    while s < SUBLANES:
        if d == 0:
            x3 = x3 + jnp.where(row >= s, pltpu.roll(x3, s, 1), 0.0)
        else:
            x3 = x3 + jnp.where(row < SUBLANES - s, pltpu.roll(x3, SUBLANES - s, 1), 0.0)
        s *= 2
    order = range(tiles) if d == 0 else range(tiles - 1, -1, -1)
    edge = SUBLANES - 1 if d == 0 else 0
    out = [None] * tiles
    carry = None
    for j in order:
        out[j] = x3[j] if carry is None else x3[j] + carry
        carry = out[j][edge:edge + 1]
    return jnp.concatenate(out, axis=0)


def _sel_right(a, m):
    hi, lo = _split2(a)
    return _dot(hi, m) + _dot(lo, m)


def _neg_abs(x):
    return pltpu.bitcast(pltpu.bitcast(x, jnp.uint32) | jnp.uint32(0x80000000), F32)


def _log2_sigmoid(z2):
    return jnp.minimum(z2, 0.0) - jnp.log2(1.0 + jnp.exp2(_neg_abs(z2)))


def _sigmoid(z):
    return 1.0 / (1.0 + jnp.exp(-z))


def _silu(z):
    return z * _sigmoid(z)


def _rms_scale(x):
    return lax.rsqrt(jnp.mean(x * x, axis=-1, keepdims=True) + RMS_EPS)


def _np_consts():
    C = CHUNK
    t = np.arange(C)
    T, U = np.meshgrid(t, t, indexing="ij")
    incl = [U <= T, U >= T]

    lev = np.zeros((2, len(LEVELS) + 1, C, 4 * C), np.float32)
    lev[:, len(LEVELS)] = np.tile(np.eye(C, dtype=np.float32), (1, 4))
    for d in range(2):
        for li, h in enumerate(LEVELS):
            same = (T // (2 * h)) == (U // (2 * h))
            if d == 0:
                pair = same & (T % (2 * h) >= h) & (U % (2 * h) < h)
            else:
                pair = same & (T % (2 * h) < h) & (U % (2 * h) >= h)
            lev[d, li] = np.tile(pair.astype(np.float32), (1, 4))

    incl_neg = np.stack([np.tile(np.where(incl[d], 0.0, NEG_INF).astype(np.float32), (1, 4))
                         for d in range(2)])
    eye = np.tile(np.eye(C, dtype=np.float32), (1, 4))
    dist = np.tile(np.abs(T - U).astype(np.float32), (1, 4))

    head_of_row = np.arange(4 * C) // C
    head_of_lane = np.arange(GROUP_W) // HEAD_DIM
    parity = np.where(head_of_row < 2, head_of_row, -1)
    hmask = np.concatenate(
        [(head_of_lane % 2)[None, :] == parity[:, None],
         head_of_row[:, None] == (np.arange(GLA_W) // GLA_DK)[None, :]], axis=1).astype(np.float32)
    bd256 = ((np.arange(GROUP_W) // HEAD_DIM)[:, None]
             == (np.arange(GROUP_W) // HEAD_DIM)[None, :]).astype(np.float32)
    bd_gl = ((np.arange(GROUP_W) // HEAD_DIM)[:, None]
             == (np.arange(GLA_W) // GLA_DK)[None, :]).astype(np.float32)

    sel_i = np.zeros((2, 128, GROUP_W), np.float32)
    sel_f = np.zeros((2, 128, GROUP_W), np.float32)
    for d in range(2):
        for h in range(HEADS):
            sel_i[d, d * 8 + h, h * HEAD_DIM:(h + 1) * HEAD_DIM] = 1.0
            sel_f[d, d * 8 + 4 + h, h * HEAD_DIM:(h + 1) * HEAD_DIM] = 1.0
    is_f = np.zeros((1, 128), np.float32)
    is_f[0, 4:8] = 1.0
    is_f[0, 12:16] = 1.0
    return dict(lev=lev, incl_neg=incl_neg, eye=eye, dist=dist,
                hmask=hmask, bd256=bd256, bd_gl=bd_gl,
                sel_i=sel_i, sel_f=sel_f, is_f=is_f)


_NP = _np_consts()


def _full_spec(shape):
    n = len(shape)
    return pl.BlockSpec(tuple(shape), lambda *_: (0,) * n)


def _ada_kernel(c_ref, w_ref, b_ref, o_ref):
    act = _silu(c_ref[...]).astype(BF16)
    o_ref[...] = _dot(act, w_ref[...].astype(BF16)) + b_ref[...]


def _ada(cvecs, w, b):
    rows = cvecs.shape[0]
    n = w.shape[1]
    tn = 1024
    return pl.pallas_call(
        _ada_kernel,
        grid=(n // tn,),
        in_specs=[pl.BlockSpec((rows, D_MODEL), lambda i: (0, 0)),
                  pl.BlockSpec((D_MODEL, tn), lambda i: (0, i)),
                  pl.BlockSpec((1, tn), lambda i: (0, i))],
        out_specs=pl.BlockSpec((rows, tn), lambda i: (0, i)),
        out_shape=jax.ShapeDtypeStruct((rows, n), F32),
        compiler_params=pltpu.CompilerParams(dimension_semantics=("arbitrary",),
                                             vmem_limit_bytes=VMEM_LIMIT),
        name="ada",
    )(cvecs, w, b.reshape(1, n))


def _rope(x, cos, sin):
    lane = lax.broadcasted_iota(jnp.int32, x.shape, 1)
    low = (lane % 32) < 16
    partner = jnp.where(low, pltpu.roll(x, GROUP_W - 16, 1), pltpu.roll(x, 16, 1))
    return x * cos + partner * sin


def _token_specs(xs, n_ctx_tiles, t0=0, per_step=1, p=0):
    T = ROW_TILE
    tile = lambda i: per_step * i + p + t0
    if len(xs) == 1:
        return [pl.BlockSpec((1, T, D_MODEL), lambda b, i: (b, tile(i), 0))]
    return [pl.BlockSpec((1, T, D_MODEL), lambda b, i: (b, jnp.minimum(tile(i), n_ctx_tiles - 1), 0)),
            pl.BlockSpec((1, T, D_MODEL), lambda b, i: (b, jnp.maximum(tile(i) - n_ctx_tiles, 0), 0))]


def _load_tokens(x_refs, n_ctx_tiles, t0=0, per_step=1, p=0):
    if len(x_refs) == 1:
        return x_refs[0][0]
    tile = per_step * pl.program_id(1) + p + t0
    return jnp.where(tile < n_ctx_tiles, x_refs[0][0], x_refs[1][0])


def _inp_kernel(n_src, n_ctx_tiles, *refs):
    x_refs = refs[:n_src]
    (mod_ref, w_ref, cos_ref, sin_ref, hg_ref, mlb_ref, isf_ref,
     wa_hi_ref, wa_lo_ref, ba_ref, ps_ref, pk_ref, pg_ref, gate_ref) = refs[n_src:]
    x = _load_tokens(x_refs, n_ctx_tiles)
    sh = mod_ref[0, 0, :, 0:D_MODEL]
    sc = mod_ref[0, 0, :, D_MODEL:2 * D_MODEL]
    xn = ((x * _rms_scale(x)) * (1.0 + sc) + sh).astype(BF16)

    def proj(a, b):
        return _dot(xn, w_ref[:, a:b])

    G = GROUP_W

    def put(off, width, fn=None):
        def post(res):
            ps_ref[0, :, off:off + width] = (res if fn is None else fn(res)).astype(BF16)
        return post

    def hgrn_decay(d):
        def post(z):
            log_lb = hg_ref[3 * d + 0:3 * d + 1, :]
            log_1mlb = hg_ref[3 * d + 1:3 * d + 2, :]
            one_m_lb = hg_ref[3 * d + 2:3 * d + 3, :]
            z2 = z * LOG2E
            ls = _log2_sigmoid(z2)
            other = log_1mlb + ls
            pg_ref[d, 0, :, G_HG_LF:G_HG_LF + G] = (
                jnp.maximum(log_lb, other) + jnp.log2(1.0 + jnp.exp2(_neg_abs(log_lb - other))))
            pk_ref[d, 0] = (one_m_lb * jnp.exp2(ls - z2)).astype(BF16)
        return post

    def narrow(small):
        small_hi, small_lo = _split2(small)
        pre2 = (small + mlb_ref[...]) * LOG2E
        gate_logs = jnp.where(isf_ref[...] > 0.5, _log2_sigmoid(pre2), pre2)
        for d in range(2):
            za = (_dot(small_hi, wa_hi_ref[d]) + _dot(small_lo, wa_hi_ref[d])
                  + _dot(small_hi, wa_lo_ref[d]) + ba_ref[d:d + 1, :])
            pg_ref[d, 0, :, G_GL_LA:G_GL_LA + GLA_W] = _log2_sigmoid(za * LOG2E) * (1.0 / GLA_TAU)
            pg_ref[d, 0, :, G_SMALL:PG_W] = gate_logs

    def gates(half, fn):
        def post(g):
            gate_ref[0, :, half * 2 * G:(half + 1) * 2 * G] = fn(g).astype(BF16)
        return post

    def rope(scale):
        return lambda r: _rope(r * scale if scale != 1.0 else r, cos_ref[...], sin_ref[...])

    o = 10 * G
    stages = [
        (W_GATES, W_GATES + 2 * G, gates(0, _sigmoid)),
        (0, G, put(O_HG_Q, G, _silu)),
        (G, 2 * G, hgrn_decay(0)),
        (3 * G, 5 * G, put(O_HG_V, 2 * G)),
        (2 * G, 3 * G, hgrn_decay(1)),
        (5 * G, 6 * G, put(O_ML_K, G, lambda r: r * (HEAD_DIM ** -0.5))),
        (W_GATES + 2 * G, IN_PAD, gates(1, _silu)),
        (6 * G, 7 * G, put(O_ML_V, G)),
        (7 * G, 8 * G, put(O_RT_Q, G, rope(1.0))),
        (9 * G, o + GLA_W, put(O_RT_V, G + GLA_W)),
        (8 * G, 9 * G, put(O_RT_K, G, rope(HEAD_DIM ** -0.5))),
        (o + GLA_W, o + 2 * GLA_W, put(O_GL_K, GLA_W, lambda r: r * (GLA_DK ** -0.5))),
        (W_SMALL, W_GATES, narrow),
        (o + 2 * GLA_W, W_SMALL, put(O_GL_V, G)),
    ]
    pending = None
    for a, b, post in stages:
        res = proj(a, b)
        if pending is not None:
            pending[1](pending[0])
        pending = (res, post)
    pending[1](pending[0])


def _inp(xs, mods, w_in_p, cos_t, sin_t, hg_par, ml_bias, wa_hi, wa_lo, ba, n_ctx_tiles):
    B = xs[0].shape[0]
    NT = sum(a.shape[1] for a in xs)
    T = ROW_TILE
    params = [hg_par, ml_bias, jnp.asarray(_NP["is_f"]), wa_hi, wa_lo, ba]
    return pl.pallas_call(
        functools.partial(_inp_kernel, len(xs), n_ctx_tiles),
        grid=(B, NT // T),
        in_specs=_token_specs(xs, n_ctx_tiles)
        + [pl.BlockSpec((1, 1, 1, 6 * D_MODEL),
                        lambda b, i: (b, jnp.where(i < n_ctx_tiles, 0, 1), 0, 0)),
                  pl.BlockSpec((D_MODEL, IN_PAD), lambda b, i: (0, 0)),
                  pl.BlockSpec((T, GROUP_W), lambda b, i: (i, 0)),
                  pl.BlockSpec((T, GROUP_W), lambda b, i: (i, 0))]
        + [_full_spec(a.shape) for a in params],
        out_specs=[pl.BlockSpec((1, T, PS_W), lambda b, i: (b, i, 0)),
                   pl.BlockSpec((2, 1, T, GROUP_W), lambda b, i: (0, b, i, 0)),
                   pl.BlockSpec((2, 1, T, PG_W), lambda b, i: (0, b, i, 0)),
                   pl.BlockSpec((1, T, D_MODEL), lambda b, i: (b, i, 0))],
        out_shape=[jax.ShapeDtypeStruct((B, NT, PS_W), BF16),
                   jax.ShapeDtypeStruct((2, B, NT, GROUP_W), BF16),
                   jax.ShapeDtypeStruct((2, B, NT, PG_W), F32),
                   jax.ShapeDtypeStruct((B, NT, D_MODEL), BF16)],
        compiler_params=pltpu.CompilerParams(dimension_semantics=("parallel", "parallel"),
                                             vmem_limit_bytes=VMEM_LIMIT),
        name="inp",
    )(*xs, mods, w_in_p, cos_t, sin_t, *params)


LANES = 128
SUBLANES = 8


def _head_blocks(a16, hmask_ref):
    C, W = a16.shape
    if W == LANES:
        return jnp.concatenate([a16, a16, a16, a16], axis=0) * hmask_ref[:, GROUP_W:GROUP_W + W]
    even = a16 * hmask_ref[0:C, 0:W]
    odd = a16 * hmask_ref[C:2 * C, 0:W]
    zero = jnp.zeros((C, LANES), a16.dtype)
    return jnp.concatenate([
        jnp.concatenate([even[:, 0:LANES], zero], axis=1),
        jnp.concatenate([odd[:, 0:LANES], zero], axis=1),
        jnp.concatenate([zero, even[:, LANES:W]], axis=1),
        jnp.concatenate([zero, odd[:, LANES:W]], axis=1)], axis=0)


def _last_row(a, d):
    return a[CHUNK - 1:CHUNK, :] if d == 0 else a[0:1, :]


def _level_exponent(d, h, log_a, b, row4):
    C, W = log_a.shape
    if h >= 4:
        b3 = b.reshape(C // (2 * h), 2 * h, W)
        r = h - 1 if d == 0 else h
        ref = jnp.broadcast_to(b3[:, r:r + 1, :], b3.shape).reshape(C, W)
        return _neg_abs(b - ref)
    up = pltpu.roll(log_a, C - 1, 0)
    dn = pltpu.roll(log_a, 1, 0)
    if d == 0:
        return jnp.where(row4 == 0, up, jnp.where(row4 == 1, 0.0,
                                                  log_a + jnp.where(row4 == 3, dn, 0.0)))
    return jnp.where(row4 == 0, log_a + up,
                     jnp.where(row4 == 1, log_a, jnp.where(row4 == 2, 0.0, dn)))


CHUNK_STAGGER = 3


def _interleave(chains):
    live = list(chains)
    rnd = 0
    while live:
        for item in list(live):
            start, g = item
            if rnd >= start:
                try:
                    next(g)
                except StopIteration:
                    live.remove(item)
        rnd += 1


def _update_state(st_ref, d, st, decay_row, bd_state, upd):
    rows = st.shape[0] // HEADS
    per_head = st.shape[1] // HEADS
    for h in range(HEADS):
        j = (h * per_head) // LANES
        rs = slice(h * rows, (h + 1) * rows)
        cs = slice(j * LANES, (j + 1) * LANES)
        st_ref[d, rs, cs] = decay_row[:, cs] * st[rs, cs] + bd_state[rs, cs] * upd[rs, cs]


def _vector_decay_chain(d, q16, k16, v16, log_a, y_ref, o_y, st_ref, lev_ref, hmask_ref, bd_state):
    C, W = log_a.shape
    G = GROUP_W
    b = _cumsum_rows(log_a, d)
    row4 = lax.broadcasted_iota(jnp.int32, (C, W), 0) & 3
    order = sorted(range(len(LEVELS)), key=lambda li: LEVELS[li] >= 4)
    attn = None
    pending = None
    for li in order:
        h = LEVELS[li]
        if h == 1:
            qa = jnp.concatenate([q16 * jnp.exp2(log_a).astype(BF16), q16], axis=0)
            kb = k16
        else:
            f16 = jnp.exp2(_level_exponent(d, h, log_a, b, row4)).astype(BF16)
            qa = q16 * f16
            kb = k16 * f16
        part = _dot_nt(qa, _head_blocks(kb, hmask_ref))
        if pending is not None:
            attn = pending if attn is None else attn + pending
        yield
        part = part.astype(BF16)
        pending = (part * lev_ref[d, li] if h > 1 else
                   part[0:C] * lev_ref[d, li] + part[C:2 * C] * lev_ref[d, len(LEVELS)])
    attn = attn + pending

    st = st_ref[d]
    b_end = _last_row(b, d)
    q_state = q16 * jnp.exp2(b).astype(BF16)
    k_end = k16 * jnp.exp2(b_end - b).astype(BF16)
    o = _dot(attn, _head_blocks(v16, hmask_ref)) + _dot_nt(q_state, st.astype(BF16))
    upd = _dot_tn(v16, k_end)
    yield
    y_ref(o_y, o)
    _update_state(st_ref, d, st, jnp.exp2(b_end), bd_state, upd)


def _mlstm_chain(d, q16, k16, v16, small, y_ref, s_ml, n_ml, m_ml, incl_neg, eye,
                 seli, self_, hmask_ref, bd256, bd256_16):
    C = CHUNK
    G = GROUP_W
    i_bc = _sel_right(small, seli)
    b_col = _sel_right(_cumsum_rows(small, d), self_)
    qk = _dot_nt(q16, _head_blocks(k16, hmask_ref))
    yield
    b_row = jnp.sum(b_col * eye, axis=0, keepdims=True)
    i_row = jnp.sum(i_bc * eye, axis=0, keepdims=True)
    w = b_col - b_row + i_row + incl_neg
    row_max = [jnp.max(w[:, h * HEAD_DIM:(h + 1) * HEAD_DIM], axis=-1, keepdims=True)
               for h in range(HEADS)]
    yield
    m_prev = m_ml[d, 0:1, :]
    inter = b_col + m_prev
    lane_head = lax.broadcasted_iota(jnp.int32, (C, G), 1) // HEAD_DIM
    m_t = None
    for h in range(HEADS):
        mh = jnp.maximum(row_max[h], inter[:, h * HEAD_DIM:h * HEAD_DIM + 1])
        mh = jnp.broadcast_to(mh, (C, G))
        m_t = mh if m_t is None else jnp.where(lane_head == h, mh, m_t)
    g_in = jnp.exp2(inter - m_t)
    s = (qk * jnp.exp2(w - m_t)).astype(BF16)
    c_st = s_ml[d]
    n_row = n_ml[d, 0:1, :]
    qn = (q16.astype(F32) * n_row).astype(BF16)
    num_a = _dot(s, _head_blocks(v16, hmask_ref))
    num_b = _dot(q16, c_st.astype(BF16))
    den_ab = _dot(jnp.concatenate([s, qn], axis=0), bd256_16)
    m_new = _last_row(m_t, d)
    b_end = _last_row(b_col, d)
    a_state = jnp.exp2(b_end + m_prev - m_new)
    kw = k16.astype(F32) * jnp.exp2(b_end - b_col + i_bc - m_new)
    upd = _dot_tn(kw.astype(BF16), v16)
    yield
    num = num_a + g_in * num_b
    den = den_ab[0:C] + g_in * den_ab[C:2 * C]
    y_ref(G, num / jnp.maximum(jnp.abs(den), jnp.exp2(-m_t)))
    _update_state(s_ml, d, c_st, a_state, bd256, upd)
    n_ml[d] = jnp.broadcast_to(a_state * n_row + jnp.sum(kw, axis=0, keepdims=True), (8, G))
    m_ml[d] = jnp.broadcast_to(m_new, (8, G))


def _retention_chain(d, q16, k16, v16, y_ref, s_rt, log_g, dist, incl_neg, hmask_ref, bd256):
    C = CHUNK
    G = GROUP_W
    qk = _dot_nt(q16, _head_blocks(k16, hmask_ref))
    decay = jnp.exp2(log_g * dist + incl_neg)
    yield
    rowc = lax.broadcasted_iota(jnp.int32, (C, 1), 0).astype(F32)
    cnt = (rowc + 1.0) if d == 0 else (C - rowc)
    q_state = q16 * jnp.exp2(log_g * cnt).astype(BF16)
    k_end = k16 * jnp.exp2(log_g * (C - cnt)).astype(BF16)
    st = s_rt[d]
    o = _dot((qk * decay).astype(BF16), _head_blocks(v16, hmask_ref)) + _dot(q_state, st.astype(BF16))
    upd = _dot_tn(k_end, v16)
    yield
    y_ref(2 * G, o)
    _update_state(s_rt, d, st, jnp.exp2(log_g * float(C)), bd256, upd)


def _mix_kernel(psf_ref, psb_ref, pkf_ref, pkb_ref, pgf_ref, pgb_ref, rtg_ref,
                lev_ref, incl_ref, eye_ref, dist_ref,
                hmask_ref, bd256_ref, bdgl_ref, seli_ref, self_ref,
                yf_ref, yb_ref,
                s_hg, s_rt, s_gl, s_ml, n_ml, m_ml):
    C = CHUNK
    G = GROUP_W

    @pl.when(pl.program_id(1) == 0)
    def _():
        s_hg[...] = jnp.zeros_like(s_hg)
        s_rt[...] = jnp.zeros_like(s_rt)
        s_gl[...] = jnp.zeros_like(s_gl)
        s_ml[...] = jnp.zeros_like(s_ml)
        n_ml[...] = jnp.zeros_like(n_ml)
        m_ml[...] = jnp.zeros_like(m_ml)

    bd256 = bd256_ref[...]
    bd256_16 = bd256.astype(BF16)

    def deferred(make):
        yield from make()

    chains = []
    for i in range(STEP_CHUNKS):
        for d, (ps_ref, pk_ref, pg_ref, y_ref) in enumerate(((psf_ref, pkf_ref, pgf_ref, yf_ref),
                                                             (psb_ref, pkb_ref, pgb_ref, yb_ref))):
            sub = i if d == 0 else STEP_CHUNKS - 1 - i
            rows = slice(sub * C, (sub + 1) * C)

            def ps(off, width, ps_ref=ps_ref, rows=rows):
                return ps_ref[0, rows, off:off + width]

            def pg(off, width, pg_ref=pg_ref, rows=rows):
                return pg_ref[0, 0, rows, off:off + width]

            def put(off, val, y_ref=y_ref, rows=rows):
                y_ref[0, rows, off:off + G] = val

            def hgrn(d=d, ps=ps, pg=pg, put=put, pk_ref=pk_ref, rows=rows):
                return _vector_decay_chain(d, ps(O_HG_Q, G), pk_ref[0, 0, rows, :], ps(O_HG_V, G),
                                           pg(G_HG_LF, G), put, 0, s_hg, lev_ref, hmask_ref, bd256)

            def mlstm(d=d, ps=ps, pg=pg, put=put):
                return _mlstm_chain(d, ps(O_ML_Q, G), ps(O_ML_K, G), ps(O_ML_V, G),
                                    pg(G_SMALL, PG_W - G_SMALL), put, s_ml, n_ml, m_ml, incl_ref[d],
                                    eye_ref[...], seli_ref[d], self_ref[d], hmask_ref, bd256, bd256_16)

            def retention(d=d, ps=ps, put=put):
                return _retention_chain(d, ps(O_RT_Q, G), ps(O_RT_K, G), ps(O_RT_V, G), put, s_rt,
                                        rtg_ref[d:d + 1, :], dist_ref[...], incl_ref[d], hmask_ref, bd256)

            def gla(d=d, ps=ps, pg=pg, put=put):
                return _vector_decay_chain(d, ps(O_GL_Q, GLA_W), ps(O_GL_K, GLA_W), ps(O_GL_V, G),
                                           pg(G_GL_LA, GLA_W), put, 3 * G, s_gl, lev_ref, hmask_ref,
                                           bdgl_ref[...])

            chains += [(i * CHUNK_STAGGER, deferred(make)) for make in (hgrn, mlstm, retention, gla)]
    _interleave(chains)


def _mix_call(B, NT, n_ctx_chunks):
    C = CHUNK * STEP_CHUNKS
    assert NT % C == 0 and n_ctx_chunks % STEP_CHUNKS == 0
    NC = NT // C
    n_ctx_blocks = n_ctx_chunks // STEP_CHUNKS
    G = GROUP_W

    def bwd_chunk(j):
        return jnp.where(j < n_ctx_blocks, n_ctx_blocks - 1 - j, NC - 1 - (j - n_ctx_blocks))

    def fwd_map(b, j):
        return (b, j, 0)

    def bwd_map(b, j):
        return (b, bwd_chunk(j), 0)

    consts = [jnp.asarray(_NP["lev"], BF16), jnp.asarray(_NP["incl_neg"]),
              jnp.asarray(_NP["eye"]), jnp.asarray(_NP["dist"]),
              jnp.asarray(_NP["hmask"], BF16),
              jnp.asarray(_NP["bd256"]), jnp.asarray(_NP["bd_gl"]),
              jnp.asarray(_NP["sel_i"], BF16), jnp.asarray(_NP["sel_f"], BF16)]
    call = pl.pallas_call(
        _mix_kernel,
        grid=(B, NC),
        in_specs=[pl.BlockSpec((1, C, PS_W), fwd_map), pl.BlockSpec((1, C, PS_W), bwd_map),
                  pl.BlockSpec((1, 1, C, G), lambda b, j: (0, b, j, 0)),
                  pl.BlockSpec((1, 1, C, G), lambda b, j: (1, b, bwd_chunk(j), 0)),
                  pl.BlockSpec((1, 1, C, PG_W), lambda b, j: (0, b, j, 0)),
                  pl.BlockSpec((1, 1, C, PG_W), lambda b, j: (1, b, bwd_chunk(j), 0)),
                  _full_spec((8, G))] + [_full_spec(a.shape) for a in consts],
        out_specs=[pl.BlockSpec((1, C, D_MODEL), fwd_map), pl.BlockSpec((1, C, D_MODEL), bwd_map)],
        out_shape=[jax.ShapeDtypeStruct((B, NT, D_MODEL), F32)] * 2,
        scratch_shapes=[pltpu.VMEM((2, G, G), F32), pltpu.VMEM((2, G, G), F32),
                        pltpu.VMEM((2, G, GLA_W), F32), pltpu.VMEM((2, G, G), F32),
                        pltpu.VMEM((2, 8, G), F32), pltpu.VMEM((2, 8, G), F32)],
        compiler_params=pltpu.CompilerParams(dimension_semantics=("parallel", "arbitrary"),
                                             vmem_limit_bytes=VMEM_LIMIT),
        name="mix",
    )
    return lambda ps, pk, pg, rt_par: call(ps, ps, pk, pk, pg, pg, rt_par, *consts)


def _out_tile_chain(final, load_x, yf_ref, yb_ref, gate_ref, mod_ref, gh_ref, bd_ref,
                    wo_ref, w1_ref, w2_ref, gf_ref, store):
    D = D_MODEL
    bd = bd_ref[...]
    for g in range(D // GROUP_W):
        sl = slice(g * GROUP_W, (g + 1) * GROUP_W)
        y = yf_ref[0, :, sl] + yb_ref[0, :, sl]
        msq = _dot((y * y).astype(BF16), bd) * (1.0 / HEAD_DIM)
        yn = y * lax.rsqrt(msq + RMS_EPS) * gh_ref[:, sl] * gate_ref[0, :, sl].astype(F32)
        part = _dot(yn.astype(BF16), wo_ref[sl, :])
        mixed = part if g == 0 else mixed + part
    yield
    g1 = mod_ref[0, 0, :, 2 * D:3 * D]
    sh2 = mod_ref[0, 0, :, 3 * D:4 * D]
    sc2 = mod_ref[0, 0, :, 4 * D:5 * D]
    g2 = mod_ref[0, 0, :, 5 * D:6 * D]
    x1 = load_x() + g1 * mixed
    hin = ((x1 * _rms_scale(x1)) * (1.0 + sc2) + sh2).astype(BF16)
    hid = _dot(hin, w1_ref[...])
    yield
    hid = jnp.maximum(hid, 0.0)
    ff = _dot((hid * hid).astype(BF16), w2_ref[...])
    yield
    x2 = x1 + g2 * ff
    if final:
        x2 = (x2 * _rms_scale(x2)) * gf_ref[...]
    store(x2)


def _out_kernel(final, n_src, n_ctx_tiles, t0, per_step, *refs):
    T = ROW_TILE
    n_tok = per_step * (n_src + 3)
    tok_refs = refs[:n_tok]
    mod_ref, gh_ref, bd_ref, wo_ref, w1_ref, w2_ref, gf_ref, o_ref = refs[n_tok:]
    chains = []
    for p in range(per_step):
        x_refs = tok_refs[p * n_src:(p + 1) * n_src]
        yf_ref, yb_ref, gate_ref = (tok_refs[per_step * (n_src + k) + p] for k in range(3))

        def load_x(x_refs=x_refs, p=p):
            return _load_tokens(x_refs, n_ctx_tiles, t0, per_step, p)

        def store(val, p=p):
            o_ref[0, p * T:(p + 1) * T, :] = val

        chains.append((p, _out_tile_chain(final, load_x, yf_ref, yb_ref, gate_ref, mod_ref, gh_ref,
                                          bd_ref, wo_ref, w1_ref, w2_ref, gf_ref, store)))
    _interleave(chains)


def _out(xs, yf, yb, gate, mods, g_heads, w_out, w_ff1, w_ff2, g_final, n_ctx_tiles, final):
    B, NT, D = yf.shape
    T = ROW_TILE
    t0 = n_ctx_tiles if final else 0
    n_tiles = NT // T - t0
    per_step = 2 if (final and n_tiles % 2 == 0 and len(xs) == 1) else 1
    bd = jnp.asarray(_NP["bd256"], BF16)
    wspec = lambda shape: pl.BlockSpec(shape, lambda b, i: (0, 0), pipeline_mode=pl.Buffered(1))

    def tok(p):
        return pl.BlockSpec((1, T, D), lambda b, i: (b, per_step * i + p + t0, 0))

    x_specs = [s for p in range(per_step) for s in _token_specs(xs, n_ctx_tiles, t0, per_step, p)]
    return pl.pallas_call(
        functools.partial(_out_kernel, final, len(xs), n_ctx_tiles, t0, per_step),
        grid=(B, n_tiles // per_step),
        in_specs=x_specs + [tok(p) for _ in range(3) for p in range(per_step)]
        + [pl.BlockSpec((1, 1, 1, 6 * D),
                        lambda b, i: (b, jnp.where(per_step * i + t0 < n_ctx_tiles, 0, 1), 0, 0)),
           pl.BlockSpec((1, D), lambda b, i: (0, 0)),
           pl.BlockSpec((GROUP_W, GROUP_W), lambda b, i: (0, 0)),
           wspec((D, D)), wspec((D, D_FF)), wspec((D_FF, D)),
           pl.BlockSpec((1, D), lambda b, i: (0, 0))],
        out_specs=pl.BlockSpec((1, per_step * T, D), lambda b, i: (b, i, 0)),
        out_shape=jax.ShapeDtypeStruct((B, n_tiles * T, D), F32),
        compiler_params=pltpu.CompilerParams(dimension_semantics=("parallel", "parallel"),
                                             vmem_limit_bytes=VMEM_LIMIT),
        name="out_final" if final else "out",
    )(*(list(xs) * per_step), *([yf] * per_step), *([yb] * per_step), *([gate] * per_step),
      mods, g_heads.reshape(1, D), bd, w_out, w_ff1, w_ff2, g_final.reshape(1, D))


_REF_LAYOUT = (('hg_q', 256), ('hg_f_fwd', 256), ('hg_f_bwd', 256), ('hg_i', 256), ('hg_g', 256),
               ('ml_q', 256), ('ml_k', 256), ('ml_v', 256), ('ml_if', 16), ('ml_o', 256),
               ('rt_q', 256), ('rt_k', 256), ('rt_v', 256), ('rt_g', 256),
               ('gl_q', 128), ('gl_k', 128), ('gl_v', 256), ('gl_a_fwd', 16), ('gl_a_bwd', 16),
               ('gl_g', 256))
_NEW_ORDER = ('hg_q', 'hg_f_fwd', 'hg_f_bwd', 'hg_i', 'ml_q', 'ml_k', 'ml_v', 'rt_q', 'rt_k', 'rt_v',
              'gl_q', 'gl_k', 'gl_v', 'ml_if', 'gl_a_fwd', 'gl_a_bwd', 'PAD', 'hg_g', 'ml_o', 'rt_g', 'gl_g')


def _reorder_columns(w):
    starts, off = {}, 0
    for name, width in _REF_LAYOUT:
        starts[name] = (off, width)
        off += width
    parts = []
    for name in _NEW_ORDER:
        if name == 'PAD':
            parts.append(jnp.zeros((w.shape[0], 128 - 48), BF16))
        else:
            a, width = starts[name]
            parts.append(w[:, a:a + width].astype(BF16))
    out = jnp.concatenate(parts, axis=1)
    assert out.shape[1] == IN_PAD
    return out


def _rope_tables(n_ctx, n_lat):
    n = jnp.arange(n_lat)
    r = (n // GRID_W).astype(F32)
    col = (n % GRID_W).astype(F32)
    n_freq = HEAD_DIM // 4
    inv = ROPE_BASE ** (-jnp.arange(n_freq, dtype=F32) / n_freq)
    ar = r[:, None] * inv[None, :]
    ac = col[:, None] * inv[None, :]
    cos = jnp.concatenate([jnp.cos(ar), jnp.cos(ar), jnp.cos(ac), jnp.cos(ac)], axis=-1)
    sin = jnp.concatenate([-jnp.sin(ar), jnp.sin(ar), -jnp.sin(ac), jnp.sin(ac)], axis=-1)
    cos = jnp.concatenate([jnp.ones((n_ctx, HEAD_DIM), F32), cos], axis=0)
    sin = jnp.concatenate([jnp.zeros((n_ctx, HEAD_DIM), F32), sin], axis=0)
    return jnp.tile(cos, (1, HEADS)), jnp.tile(sin, (1, HEADS))


def kernel(x, c, ctx, c_ctx, w_ada, b_ada, w_in, g_heads, hgrn_lb_logits, ml_gate_bias,
           rt_decay_logit, gla_w_a, gla_b_a, w_out, w_ff1, w_ff2, g_final):
    B, L, D = x.shape
    Lc = ctx.shape[1]
    depth = w_in.shape[0]
    assert D == D_MODEL and L % ROW_TILE == 0 and Lc % ROW_TILE == 0 and L % GRID_W == 0
    n_ctx_tiles = Lc // ROW_TILE
    n_ctx_chunks = Lc // CHUNK

    xs = (ctx, x)
    cos_t, sin_t = _rope_tables(Lc, L)
    mix = _mix_call(B, Lc + L, n_ctx_chunks)

    n_c = -(-(B + 1) // 8) * 8
    cvecs = jnp.zeros((n_c, D), F32).at[:B].set(c).at[B].set(c_ctx)

    sm = jax.nn.softmax(hgrn_lb_logits.astype(F32), axis=0)
    lb_all = jnp.maximum(jnp.cumsum(sm, axis=0) - sm[:1], 0.0)

    out = None
    for layer in range(depth):
        final = layer == depth - 1
        mod = _ada(cvecs, w_ada[layer], b_ada[layer])
        mods = jnp.stack([jnp.broadcast_to(mod[B], (B, 6 * D)), mod[:B]], axis=1)[:, :, None, :]

        w_in_p = _reorder_columns(w_in[layer])
        lb = lb_all[layer]
        hg_par = jnp.stack([jnp.log(lb[0]) * LOG2E, jnp.log1p(-lb[0]) * LOG2E, 1.0 - lb[0],
                            jnp.log(lb[1]) * LOG2E, jnp.log1p(-lb[1]) * LOG2E, 1.0 - lb[1],
                            jnp.zeros_like(lb[0]), jnp.zeros_like(lb[0])], axis=0)
        ml_bias = jnp.zeros((1, 128), F32).at[0, :16].set(ml_gate_bias[layer].astype(F32).reshape(16))
        wa = jnp.zeros((2, 128, GLA_W), F32)
        wa = wa.at[0, 16:32].set(gla_w_a[layer, 0].astype(F32)).at[1, 32:48].set(gla_w_a[layer, 1].astype(F32))
        wa_hi = wa.astype(BF16)
        wa_lo = (wa - wa_hi.astype(F32)).astype(BF16)
        ba = jnp.zeros((8, GLA_W), F32).at[:2].set(gla_b_a[layer].astype(F32))
        ps, pk, pg, gate = _inp(xs, mods, w_in_p, cos_t, sin_t, hg_par, ml_bias, wa_hi, wa_lo, ba,
                                n_ctx_tiles)

        rt_par = jnp.zeros((8, GROUP_W), F32).at[:2].set(
            jnp.repeat(jax.nn.log_sigmoid(rt_decay_logit[layer].astype(F32)) * LOG2E, HEAD_DIM, axis=-1))
        yf, yb = mix(ps, pk, pg, rt_par)

        res = _out(xs, yf, yb, gate, mods, g_heads[layer], w_out[layer].astype(BF16),
                   w_ff1[layer].astype(BF16), w_ff2[layer].astype(BF16), g_final,
                   n_ctx_tiles, final)
        if final:
            out = res
        else:
            xs = (res,)
    return out
```

```python
import functools

import numpy as np
import jax
import jax.numpy as jnp
from jax import lax
from jax.experimental import pallas as pl
from jax.experimental.pallas import tpu as pltpu

F32 = jnp.float32
BF16 = jnp.bfloat16

D_MODEL = 1024
GROUP_W = 256
HEADS = 4
HEAD_DIM = 64
GLA_DK = 32
GLA_W = HEADS * GLA_DK
GLA_TAU = 16.0
D_FF = 4 * D_MODEL
GRID_W = 64
ROPE_BASE = 10000.0
RMS_EPS = 1e-6

CHUNK = 64
STEP_CHUNKS = 4
ROW_TILE = 256
LEVELS = (32, 16, 8, 4, 2, 1)

W_SMALL = 3072
W_GATES = 3200
IN_PAD = W_GATES + D_MODEL

O_HG_Q, O_HG_V = 0, 256
O_ML_Q, O_ML_K, O_ML_V = 512, 768, 1024
O_RT_Q, O_RT_K, O_RT_V = 1280, 1536, 1792
O_GL_Q, O_GL_K, O_GL_V = 2048, 2176, 2304
PS_W = 2560
G_HG_LF, G_GL_LA, G_SMALL = 0, 256, 384
PG_W = 512

VMEM_LIMIT = 56 * 1024 * 1024

NEG_INF = float("-inf")
LOG2E = 1.4426950408889634


def _dot(a, b):
    return jnp.dot(a, b, preferred_element_type=F32)


def _dot_nt(a, b):
    return lax.dot_general(a, b, (((1,), (1,)), ((), ())), preferred_element_type=F32)


def _dot_tn(a, b):
    return lax.dot_general(a, b, (((0,), (0,)), ((), ())), preferred_element_type=F32)


def _split2(a):
    hi = a.astype(BF16)
    return hi, (a - hi.astype(F32)).astype(BF16)


def _cumsum_rows(x, d):
    n, width = x.shape
    tiles = n // SUBLANES
    x3 = x.reshape(tiles, SUBLANES, width)
    row = lax.broadcasted_iota(jnp.int32, (1, SUBLANES, width), 1)
    s = 1
    while s < SUBLANES:
        if d == 0:
            x3 = x3 + jnp.where(row >= s, pltpu.roll(x3, s, 1), 0.0)
        else:
            x3 = x3 + jnp.where(row < SUBLANES - s, pltpu.roll(x3, SUBLANES - s, 1), 0.0)
        s *= 2
    order = range(tiles) if d == 0 else range(tiles - 1, -1, -1)
    edge = SUBLANES - 1 if d == 0 else 0
    out = [None] * tiles
    carry = None
    for j in order:
        out[j] = x3[j] if carry is None else x3[j] + carry
        carry = out[j][edge:edge + 1]
    return jnp.concatenate(out, axis=0)


def _sel_right(a, m):
    hi, lo = _split2(a)
    return _dot(hi, m) + _dot(lo, m)


def _neg_abs(x):
    return pltpu.bitcast(pltpu.bitcast(x, jnp.uint32) | jnp.uint32(0x80000000), F32)


def _log2_sigmoid(z2):
    return jnp.minimum(z2, 0.0) - jnp.log2(1.0 + jnp.exp2(_neg_abs(z2)))


def _sigmoid(z):
    return 1.0 / (1.0 + jnp.exp(-z))


def _silu(z):
    return z * _sigmoid(z)


def _rms_scale(x):
    return lax.rsqrt(jnp.mean(x * x, axis=-1, keepdims=True) + RMS_EPS)


def _np_consts():
    C = CHUNK
    t = np.arange(C)
    T, U = np.meshgrid(t, t, indexing="ij")
    incl = [U <= T, U >= T]

    lev = np.zeros((2, len(LEVELS) + 1, C, 4 * C), np.float32)
    lev[:, len(LEVELS)] = np.tile(np.eye(C, dtype=np.float32), (1, 4))
    for d in range(2):
        for li, h in enumerate(LEVELS):
            same = (T // (2 * h)) == (U // (2 * h))
            if d == 0:
                pair = same & (T % (2 * h) >= h) & (U % (2 * h) < h)
            else:
                pair = same & (T % (2 * h) < h) & (U % (2 * h) >= h)
            lev[d, li] = np.tile(pair.astype(np.float32), (1, 4))

    incl_neg = np.stack([np.tile(np.where(incl[d], 0.0, NEG_INF).astype(np.float32), (1, 4))
                         for d in range(2)])
    eye = np.tile(np.eye(C, dtype=np.float32), (1, 4))
    dist = np.tile(np.abs(T - U).astype(np.float32), (1, 4))

    head_of_row = np.arange(4 * C) // C
    head_of_lane = np.arange(GROUP_W) // HEAD_DIM
    parity = np.where(head_of_row < 2, head_of_row, -1)
    hmask = np.concatenate(
        [(head_of_lane % 2)[None, :] == parity[:, None],
         head_of_row[:, None] == (np.arange(GLA_W) // GLA_DK)[None, :]], axis=1).astype(np.float32)
    bd256 = ((np.arange(GROUP_W) // HEAD_DIM)[:, None]
             == (np.arange(GROUP_W) // HEAD_DIM)[None, :]).astype(np.float32)
    bd_gl = ((np.arange(GROUP_W) // HEAD_DIM)[:, None]
             == (np.arange(GLA_W) // GLA_DK)[None, :]).astype(np.float32)

    sel_i = np.zeros((2, 128, GROUP_W), np.float32)
    sel_f = np.zeros((2, 128, GROUP_W), np.float32)
    for d in range(2):
        for h in range(HEADS):
            sel_i[d, d * 8 + h, h * HEAD_DIM:(h + 1) * HEAD_DIM] = 1.0
            sel_f[d, d * 8 + 4 + h, h * HEAD_DIM:(h + 1) * HEAD_DIM] = 1.0
    is_f = np.zeros((1, 128), np.float32)
    is_f[0, 4:8] = 1.0
    is_f[0, 12:16] = 1.0
    return dict(lev=lev, incl_neg=incl_neg, eye=eye, dist=dist,
                hmask=hmask, bd256=bd256, bd_gl=bd_gl,
                sel_i=sel_i, sel_f=sel_f, is_f=is_f)


_NP = _np_consts()


def _full_spec(shape):
    n = len(shape)
    return pl.BlockSpec(tuple(shape), lambda *_: (0,) * n)


def _ada_kernel(c_ref, w_ref, b_ref, o_ref):
    act = _silu(c_ref[...]).astype(BF16)
    o_ref[...] = _dot(act, w_ref[...].astype(BF16)) + b_ref[...]


def _ada(cvecs, w, b):
    rows = cvecs.shape[0]
    n = w.shape[1]
    tn = 1024
    return pl.pallas_call(
        _ada_kernel,
        grid=(n // tn,),
        in_specs=[pl.BlockSpec((rows, D_MODEL), lambda i: (0, 0)),
                  pl.BlockSpec((D_MODEL, tn), lambda i: (0, i)),
                  pl.BlockSpec((1, tn), lambda i: (0, i))],
        out_specs=pl.BlockSpec((rows, tn), lambda i: (0, i)),
        out_shape=jax.ShapeDtypeStruct((rows, n), F32),
        compiler_params=pltpu.CompilerParams(dimension_semantics=("arbitrary",),
                                             vmem_limit_bytes=VMEM_LIMIT),
        name="ada",
    )(cvecs, w, b.reshape(1, n))


def _rope(x, cos, sin):
    lane = lax.broadcasted_iota(jnp.int32, x.shape, 1)
    low = (lane % 32) < 16
    partner = jnp.where(low, pltpu.roll(x, GROUP_W - 16, 1), pltpu.roll(x, 16, 1))
    return x * cos + partner * sin


def _token_specs(xs, row_tile, n_ctx_tiles):
    T = ROW_TILE
    if len(xs) == 1:
        return [pl.BlockSpec((1, T, D_MODEL), lambda j: (*row_tile(j), 0))]
    return [pl.BlockSpec((1, T, D_MODEL),
                         lambda j: (row_tile(j)[0], jnp.minimum(row_tile(j)[1], n_ctx_tiles - 1), 0)),
            pl.BlockSpec((1, T, D_MODEL),
                         lambda j: (row_tile(j)[0], jnp.maximum(row_tile(j)[1] - n_ctx_tiles, 0), 0))]


def _load_tokens(x_refs, tile, n_ctx_tiles):
    if len(x_refs) == 1:
        return x_refs[0][0]
    return jnp.where(tile < n_ctx_tiles, x_refs[0][0], x_refs[1][0])


def _inp_tile_chain(p, load_x, mod_ref, cos_ref, sin_ref, w_ref, hg_ref, mlb_ref, isf_ref,
                    wa_hi_ref, wa_lo_ref, ba_ref, ps_ref, pk_ref, pg_ref, gate_ref):
    x = load_x()
    sh = mod_ref[0, 0, :, 0:D_MODEL]
    sc = mod_ref[0, 0, :, D_MODEL:2 * D_MODEL]
    xn = ((x * _rms_scale(x)) * (1.0 + sc) + sh).astype(BF16)

    def proj(a, b):
        return _dot(xn, w_ref[:, a:b])

    G = GROUP_W

    def put(off, width, fn=None):
        def post(res):
            ps_ref[p, :, off:off + width] = (res if fn is None else fn(res)).astype(BF16)
        return post

    def hgrn_decay(d):
        def post(z):
            log_lb = hg_ref[3 * d + 0:3 * d + 1, :]
            log_1mlb = hg_ref[3 * d + 1:3 * d + 2, :]
            one_m_lb = hg_ref[3 * d + 2:3 * d + 3, :]
            z2 = z * LOG2E
            ls = _log2_sigmoid(z2)
            other = log_1mlb + ls
            pg_ref[d, p, :, G_HG_LF:G_HG_LF + G] = (
                jnp.maximum(log_lb, other) + jnp.log2(1.0 + jnp.exp2(_neg_abs(log_lb - other))))
            pk_ref[d, p] = (one_m_lb * jnp.exp2(ls - z2)).astype(BF16)
        return post

    def narrow(small):
        small_hi, small_lo = _split2(small)
        pre2 = (small + mlb_ref[...]) * LOG2E
        gate_logs = jnp.where(isf_ref[...] > 0.5, _log2_sigmoid(pre2), pre2)
        for d in range(2):
            za = (_dot(small_hi, wa_hi_ref[d]) + _dot(small_lo, wa_hi_ref[d])
                  + _dot(small_hi, wa_lo_ref[d]) + ba_ref[d:d + 1, :])
            pg_ref[d, p, :, G_GL_LA:G_GL_LA + GLA_W] = _log2_sigmoid(za * LOG2E) * (1.0 / GLA_TAU)
            pg_ref[d, p, :, G_SMALL:PG_W] = gate_logs

    def gates(half, fn):
        def post(g):
            gate_ref[p, :, half * 2 * G:(half + 1) * 2 * G] = fn(g).astype(BF16)
        return post

    def rope(scale):
        return lambda r: _rope(r * scale if scale != 1.0 else r, cos_ref[...], sin_ref[...])

    o = 10 * G
    stages = [
        (W_GATES, W_GATES + 2 * G, gates(0, _sigmoid)),
        (0, G, put(O_HG_Q, G, _silu)),
        (G, 2 * G, hgrn_decay(0)),
        (3 * G, 5 * G, put(O_HG_V, 2 * G)),
        (2 * G, 3 * G, hgrn_decay(1)),
        (5 * G, 6 * G, put(O_ML_K, G, lambda r: r * (HEAD_DIM ** -0.5))),
        (W_GATES + 2 * G, IN_PAD, gates(1, _silu)),
        (6 * G, 7 * G, put(O_ML_V, G)),
        (7 * G, 8 * G, put(O_RT_Q, G, rope(1.0))),
        (9 * G, o + GLA_W, put(O_RT_V, G + GLA_W)),
        (8 * G, 9 * G, put(O_RT_K, G, rope(HEAD_DIM ** -0.5))),
        (o + GLA_W, o + 2 * GLA_W, put(O_GL_K, GLA_W, lambda r: r * (GLA_DK ** -0.5))),
        (W_SMALL, W_GATES, narrow),
        (o + 2 * GLA_W, W_SMALL, put(O_GL_V, G)),
    ]
    assert len(stages) == INP_STAGES
    pending = None
    for a, b, post in stages:
        res = proj(a, b)
        if pending is not None:
            pending[1](pending[0])
        pending = (res, post)
        yield
    pending[1](pending[0])


INP_STAGES = 14
INP_TILES_PER_STEP = 4


def _inp_kernel(n_src, n_ctx_tiles, tiles_per_row, per_step, *refs):
    n_tok = per_step * (n_src + 3)
    tok_refs = refs[:n_tok]
    w_ref, hg_ref, mlb_ref, isf_ref, wa_hi_ref, wa_lo_ref, ba_ref = refs[n_tok:n_tok + 7]
    ps_ref, pk_ref, pg_ref, gate_ref = refs[n_tok + 7:]
    chains = []
    for p in range(per_step):
        x_refs = tok_refs[p * n_src:(p + 1) * n_src]
        mod_ref, cos_ref, sin_ref = (tok_refs[per_step * (n_src + k) + p] for k in range(3))

        def load_x(x_refs=x_refs, p=p):
            return _load_tokens(x_refs, (per_step * pl.program_id(0) + p) % tiles_per_row, n_ctx_tiles)

        chains.append((p * (INP_STAGES - 1),
                       _inp_tile_chain(p, load_x, mod_ref, cos_ref, sin_ref, w_ref, hg_ref, mlb_ref,
                                       isf_ref, wa_hi_ref, wa_lo_ref, ba_ref, ps_ref, pk_ref, pg_ref,
                                       gate_ref)))
    _interleave(chains)


def _inp(xs, mods, w_in_p, cos_t, sin_t, hg_par, ml_bias, wa_hi, wa_lo, ba, n_ctx_tiles):
    B = xs[0].shape[0]
    NT = sum(a.shape[1] for a in xs)
    T = ROW_TILE
    tpr = NT // T
    n_all = B * tpr
    most = INP_TILES_PER_STEP if len(xs) == 1 else 2
    per_step = next(n for n in (most, 2, 1) if n_all % n == 0)
    params = [hg_par, ml_bias, jnp.asarray(_NP["is_f"]), wa_hi, wa_lo, ba]

    def row_tile(j, p):
        g = per_step * j + p
        return g // tpr, g % tpr

    def x_specs(p):
        return _token_specs(xs, functools.partial(row_tile, p=p), n_ctx_tiles)

    def mod_spec(p):
        return pl.BlockSpec((1, 1, 1, 6 * D_MODEL),
                            lambda j: (row_tile(j, p)[0], jnp.where(row_tile(j, p)[1] < n_ctx_tiles, 0, 1), 0, 0))

    def rope_spec(p):
        return pl.BlockSpec((T, GROUP_W), lambda j: (row_tile(j, p)[1], 0))

    slots = range(per_step)
    outs = pl.pallas_call(
        functools.partial(_inp_kernel, len(xs), n_ctx_tiles, tpr, per_step),
        grid=(n_all // per_step,),
        in_specs=[s for p in slots for s in x_specs(p)]
        + [mod_spec(p) for p in slots] + [rope_spec(p) for p in slots] + [rope_spec(p) for p in slots]
        + [pl.BlockSpec((D_MODEL, IN_PAD), lambda j: (0, 0), pipeline_mode=pl.Buffered(1))]
        + [_full_spec(a.shape) for a in params],
        out_specs=[pl.BlockSpec((per_step, T, PS_W), lambda j: (j, 0, 0)),
                   pl.BlockSpec((2, per_step, T, GROUP_W), lambda j: (0, j, 0, 0)),
                   pl.BlockSpec((2, per_step, T, PG_W), lambda j: (0, j, 0, 0)),
                   pl.BlockSpec((per_step, T, D_MODEL), lambda j: (j, 0, 0))],
        out_shape=[jax.ShapeDtypeStruct((n_all, T, PS_W), BF16),
                   jax.ShapeDtypeStruct((2, n_all, T, GROUP_W), BF16),
                   jax.ShapeDtypeStruct((2, n_all, T, PG_W), F32),
                   jax.ShapeDtypeStruct((n_all, T, D_MODEL), BF16)],
        compiler_params=pltpu.CompilerParams(dimension_semantics=("parallel",),
                                             vmem_limit_bytes=VMEM_LIMIT),
        name="inp",
    )(*(list(xs) * per_step), *([mods] * per_step), *([cos_t] * per_step), *([sin_t] * per_step),
      w_in_p, *params)
    ps, pk, pg, gate = outs
    return (ps.reshape(B, NT, PS_W), pk.reshape(2, B, NT, GROUP_W), pg.reshape(2, B, NT, PG_W),
            gate.reshape(B, NT, D_MODEL))


LANES = 128
SUBLANES = 8


def _head_blocks(a16, hmask_ref):
    C, W = a16.shape
    if W == LANES:
        return jnp.concatenate([a16, a16, a16, a16], axis=0) * hmask_ref[:, GROUP_W:GROUP_W + W]
    even = a16 * hmask_ref[0:C, 0:W]
    odd = a16 * hmask_ref[C:2 * C, 0:W]
    zero = jnp.zeros((C, LANES), a16.dtype)
    return jnp.concatenate([
        jnp.concatenate([even[:, 0:LANES], zero], axis=1),
        jnp.concatenate([odd[:, 0:LANES], zero], axis=1),
        jnp.concatenate([zero, even[:, LANES:W]], axis=1),
        jnp.concatenate([zero, odd[:, LANES:W]], axis=1)], axis=0)


def _last_row(a, d):
    return a[CHUNK - 1:CHUNK, :] if d == 0 else a[0:1, :]


def _level_exponent(d, h, log_a, b, row4):
    C, W = log_a.shape
    if h >= 4:
        b3 = b.reshape(C // (2 * h), 2 * h, W)
        r = h - 1 if d == 0 else h
        ref = jnp.broadcast_to(b3[:, r:r + 1, :], b3.shape).reshape(C, W)
        return _neg_abs(b - ref)
    up = pltpu.roll(log_a, C - 1, 0)
    dn = pltpu.roll(log_a, 1, 0)
    if d == 0:
        return jnp.where(row4 == 0, up, jnp.where(row4 == 1, 0.0,
                                                  log_a + jnp.where(row4 == 3, dn, 0.0)))
    return jnp.where(row4 == 0, log_a + up,
                     jnp.where(row4 == 1, log_a, jnp.where(row4 == 2, 0.0, dn)))


CHUNK_STAGGER = 3


def _interleave(chains):
    live = list(chains)
    rnd = 0
    while live:
        for item in list(live):
            start, g = item
            if rnd >= start:
                try:
                    next(g)
                except StopIteration:
                    live.remove(item)
        rnd += 1


def _update_state(st_ref, d, st, decay_row, bd_state, upd):
    rows = st.shape[0] // HEADS
    per_head = st.shape[1] // HEADS
    for h in range(HEADS):
        j = (h * per_head) // LANES
        rs = slice(h * rows, (h + 1) * rows)
        cs = slice(j * LANES, (j + 1) * LANES)
        st_ref[d, rs, cs] = decay_row[:, cs] * st[rs, cs] + bd_state[rs, cs] * upd[rs, cs]


def _vector_decay_chain(d, q16, k16, v16, log_a, y_ref, o_y, st_ref, lev_ref, hmask_ref, bd_state):
    C, W = log_a.shape
    G = GROUP_W
    b = _cumsum_rows(log_a, d)
    row4 = lax.broadcasted_iota(jnp.int32, (C, W), 0) & 3
    order = sorted(range(len(LEVELS)), key=lambda li: LEVELS[li] >= 4)
    attn = None
    pending = None
    for li in order:
        h = LEVELS[li]
        if h == 1:
            qa = jnp.concatenate([q16 * jnp.exp2(log_a).astype(BF16), q16], axis=0)
            kb = k16
        else:
            f16 = jnp.exp2(_level_exponent(d, h, log_a, b, row4)).astype(BF16)
            qa = q16 * f16
            kb = k16 * f16
        part = _dot_nt(qa, _head_blocks(kb, hmask_ref))
        if pending is not None:
            attn = pending if attn is None else attn + pending
        yield
        part = part.astype(BF16)
        pending = (part * lev_ref[d, li] if h > 1 else
                   part[0:C] * lev_ref[d, li] + part[C:2 * C] * lev_ref[d, len(LEVELS)])
    attn = attn + pending

    st = st_ref[d]
    b_end = _last_row(b, d)
    q_state = q16 * jnp.exp2(b).astype(BF16)
    k_end = k16 * jnp.exp2(b_end - b).astype(BF16)
    o = _dot(attn, _head_blocks(v16, hmask_ref)) + _dot_nt(q_state, st.astype(BF16))
    upd = _dot_tn(v16, k_end)
    yield
    y_ref(o_y, o)
    _update_state(st_ref, d, st, jnp.exp2(b_end), bd_state, upd)


def _mlstm_chain(d, q16, k16, v16, small, y_ref, s_ml, n_ml, m_ml, incl_neg, eye,
                 seli, self_, hmask_ref, bd256, bd256_16):
    C = CHUNK
    G = GROUP_W
    i_bc = _sel_right(small, seli)
    b_col = _sel_right(_cumsum_rows(small, d), self_)
    qk = _dot_nt(q16, _head_blocks(k16, hmask_ref))
    yield
    b_row = jnp.sum(b_col * eye, axis=0, keepdims=True)
    i_row = jnp.sum(i_bc * eye, axis=0, keepdims=True)
    w = b_col - b_row + i_row + incl_neg
    row_max = [jnp.max(w[:, h * HEAD_DIM:(h + 1) * HEAD_DIM], axis=-1, keepdims=True)
               for h in range(HEADS)]
    yield
    m_prev = m_ml[d, 0:1, :]
    inter = b_col + m_prev
    lane_head = lax.broadcasted_iota(jnp.int32, (C, G), 1) // HEAD_DIM
    m_t = None
    for h in range(HEADS):
        mh = jnp.maximum(row_max[h], inter[:, h * HEAD_DIM:h * HEAD_DIM + 1])
        mh = jnp.broadcast_to(mh, (C, G))
        m_t = mh if m_t is None else jnp.where(lane_head == h, mh, m_t)
    g_in = jnp.exp2(inter - m_t)
    s = (qk * jnp.exp2(w - m_t)).astype(BF16)
    c_st = s_ml[d]
    n_row = n_ml[d, 0:1, :]
    qn = (q16.astype(F32) * n_row).astype(BF16)
    num_a = _dot(s, _head_blocks(v16, hmask_ref))
    num_b = _dot(q16, c_st.astype(BF16))
    den_ab = _dot(jnp.concatenate([s, qn], axis=0), bd256_16)
    m_new = _last_row(m_t, d)
    b_end = _last_row(b_col, d)
    a_state = jnp.exp2(b_end + m_prev - m_new)
    kw = k16.astype(F32) * jnp.exp2(b_end - b_col + i_bc - m_new)
    upd = _dot_tn(kw.astype(BF16), v16)
    yield
    num = num_a + g_in * num_b
    den = den_ab[0:C] + g_in * den_ab[C:2 * C]
    y_ref(G, num / jnp.maximum(jnp.abs(den), jnp.exp2(-m_t)))
    _update_state(s_ml, d, c_st, a_state, bd256, upd)
    n_ml[d] = jnp.broadcast_to(a_state * n_row + jnp.sum(kw, axis=0, keepdims=True), (8, G))
    m_ml[d] = jnp.broadcast_to(m_new, (8, G))


def _retention_chain(d, q16, k16, v16, y_ref, s_rt, log_g, dist, incl_neg, hmask_ref, bd256):
    C = CHUNK
    G = GROUP_W
    qk = _dot_nt(q16, _head_blocks(k16, hmask_ref))
    decay = jnp.exp2(log_g * dist + incl_neg)
    yield
    rowc = lax.broadcasted_iota(jnp.int32, (C, 1), 0).astype(F32)
    cnt = (rowc + 1.0) if d == 0 else (C - rowc)
    q_state = q16 * jnp.exp2(log_g * cnt).astype(BF16)
    k_end = k16 * jnp.exp2(log_g * (C - cnt)).astype(BF16)
    st = s_rt[d]
    o = _dot((qk * decay).astype(BF16), _head_blocks(v16, hmask_ref)) + _dot(q_state, st.astype(BF16))
    upd = _dot_tn(k_end, v16)
    yield
    y_ref(2 * G, o)
    _update_state(s_rt, d, st, jnp.exp2(log_g * float(C)), bd256, upd)


def _mix_kernel(psf_ref, psb_ref, pkf_ref, pkb_ref, pgf_ref, pgb_ref, rtg_ref,
                lev_ref, incl_ref, eye_ref, dist_ref,
                hmask_ref, bd256_ref, bdgl_ref, seli_ref, self_ref,
                yf_ref, yb_ref,
                s_hg, s_rt, s_gl, s_ml, n_ml, m_ml):
    C = CHUNK
    G = GROUP_W

    @pl.when(pl.program_id(1) == 0)
    def _():
        s_hg[...] = jnp.zeros_like(s_hg)
        s_rt[...] = jnp.zeros_like(s_rt)
        s_gl[...] = jnp.zeros_like(s_gl)
        s_ml[...] = jnp.zeros_like(s_ml)
        n_ml[...] = jnp.zeros_like(n_ml)
        m_ml[...] = jnp.zeros_like(m_ml)

    bd256 = bd256_ref[...]
    bd256_16 = bd256.astype(BF16)

    def deferred(make):
        yield from make()

    chains = []
    for i in range(STEP_CHUNKS):
        for d, (ps_ref, pk_ref, pg_ref, y_ref) in enumerate(((psf_ref, pkf_ref, pgf_ref, yf_ref),
                                                             (psb_ref, pkb_ref, pgb_ref, yb_ref))):
            sub = i if d == 0 else STEP_CHUNKS - 1 - i
            rows = slice(sub * C, (sub + 1) * C)

            def ps(off, width, ps_ref=ps_ref, rows=rows):
                return ps_ref[0, rows, off:off + width]

            def pg(off, width, pg_ref=pg_ref, rows=rows):
                return pg_ref[0, 0, rows, off:off + width]

            def put(off, val, y_ref=y_ref, rows=rows):
                y_ref[0, rows, off:off + G] = val

            def hgrn(d=d, ps=ps, pg=pg, put=put, pk_ref=pk_ref, rows=rows):
                return _vector_decay_chain(d, ps(O_HG_Q, G), pk_ref[0, 0, rows, :], ps(O_HG_V, G),
                                           pg(G_HG_LF, G), put, 0, s_hg, lev_ref, hmask_ref, bd256)

            def mlstm(d=d, ps=ps, pg=pg, put=put):
                return _mlstm_chain(d, ps(O_ML_Q, G), ps(O_ML_K, G), ps(O_ML_V, G),
                                    pg(G_SMALL, PG_W - G_SMALL), put, s_ml, n_ml, m_ml, incl_ref[d],
                                    eye_ref[...], seli_ref[d], self_ref[d], hmask_ref, bd256, bd256_16)

            def retention(d=d, ps=ps, put=put):
                return _retention_chain(d, ps(O_RT_Q, G), ps(O_RT_K, G), ps(O_RT_V, G), put, s_rt,
                                        rtg_ref[d:d + 1, :], dist_ref[...], incl_ref[d], hmask_ref, bd256)

            def gla(d=d, ps=ps, pg=pg, put=put):
                return _vector_decay_chain(d, ps(O_GL_Q, GLA_W), ps(O_GL_K, GLA_W), ps(O_GL_V, G),
                                           pg(G_GL_LA, GLA_W), put, 3 * G, s_gl, lev_ref, hmask_ref,
                                           bdgl_ref[...])

            chains += [(i * CHUNK_STAGGER, deferred(make)) for make in (hgrn, mlstm, retention, gla)]
    _interleave(chains)


def _mix_call(B, NT, n_ctx_chunks):
    C = CHUNK * STEP_CHUNKS
    assert NT % C == 0 and n_ctx_chunks % STEP_CHUNKS == 0
    NC = NT // C
    n_ctx_blocks = n_ctx_chunks // STEP_CHUNKS
    G = GROUP_W

    def bwd_chunk(j):
        return jnp.where(j < n_ctx_blocks, n_ctx_blocks - 1 - j, NC - 1 - (j - n_ctx_blocks))

    def fwd_map(b, j):
        return (b, j, 0)

    def bwd_map(b, j):
        return (b, bwd_chunk(j), 0)

    consts = [jnp.asarray(_NP["lev"], BF16), jnp.asarray(_NP["incl_neg"]),
              jnp.asarray(_NP["eye"]), jnp.asarray(_NP["dist"]),
              jnp.asarray(_NP["hmask"], BF16),
              jnp.asarray(_NP["bd256"]), jnp.asarray(_NP["bd_gl"]),
              jnp.asarray(_NP["sel_i"], BF16), jnp.asarray(_NP["sel_f"], BF16)]
    call = pl.pallas_call(
        _mix_kernel,
        grid=(B, NC),
        in_specs=[pl.BlockSpec((1, C, PS_W), fwd_map), pl.BlockSpec((1, C, PS_W), bwd_map),
                  pl.BlockSpec((1, 1, C, G), lambda b, j: (0, b, j, 0)),
                  pl.BlockSpec((1, 1, C, G), lambda b, j: (1, b, bwd_chunk(j), 0)),
                  pl.BlockSpec((1, 1, C, PG_W), lambda b, j: (0, b, j, 0)),
                  pl.BlockSpec((1, 1, C, PG_W), lambda b, j: (1, b, bwd_chunk(j), 0)),
                  _full_spec((8, G))] + [_full_spec(a.shape) for a in consts],
        out_specs=[pl.BlockSpec((1, C, D_MODEL), fwd_map), pl.BlockSpec((1, C, D_MODEL), bwd_map)],
        out_shape=[jax.ShapeDtypeStruct((B, NT, D_MODEL), F32)] * 2,
        scratch_shapes=[pltpu.VMEM((2, G, G), F32), pltpu.VMEM((2, G, G), F32),
                        pltpu.VMEM((2, G, GLA_W), F32), pltpu.VMEM((2, G, G), F32),
                        pltpu.VMEM((2, 8, G), F32), pltpu.VMEM((2, 8, G), F32)],
        compiler_params=pltpu.CompilerParams(dimension_semantics=("parallel", "arbitrary"),
                                             vmem_limit_bytes=VMEM_LIMIT),
        name="mix",
    )
    return lambda ps, pk, pg, rt_par: call(ps, ps, pk, pk, pg, pg, rt_par, *consts)


def _out_tile_chain(final, load_x, yf_ref, yb_ref, gate_ref, mod_ref, gh_ref, bd_ref,
                    wo_ref, w1_ref, w2_ref, gf_ref, store):
    D = D_MODEL
    bd = bd_ref[...]
    for g in range(D // GROUP_W):
        sl = slice(g * GROUP_W, (g + 1) * GROUP_W)
        y = yf_ref[0, :, sl] + yb_ref[0, :, sl]
        msq = _dot((y * y).astype(BF16), bd) * (1.0 / HEAD_DIM)
        yn = y * lax.rsqrt(msq + RMS_EPS) * gh_ref[:, sl] * gate_ref[0, :, sl].astype(F32)
        part = _dot(yn.astype(BF16), wo_ref[sl, :])
        mixed = part if g == 0 else mixed + part
    yield
    g1 = mod_ref[0, 0, :, 2 * D:3 * D]
    sh2 = mod_ref[0, 0, :, 3 * D:4 * D]
    sc2 = mod_ref[0, 0, :, 4 * D:5 * D]
    g2 = mod_ref[0, 0, :, 5 * D:6 * D]
    x1 = load_x() + g1 * mixed
    hin = ((x1 * _rms_scale(x1)) * (1.0 + sc2) + sh2).astype(BF16)
    hid = _dot(hin, w1_ref[...])
    yield
    hid = jnp.maximum(hid, 0.0)
    ff = _dot((hid * hid).astype(BF16), w2_ref[...])
    yield
    x2 = x1 + g2 * ff
    if final:
        x2 = (x2 * _rms_scale(x2)) * gf_ref[...]
    store(x2)


def _out_kernel(final, n_src, n_ctx_tiles, t0, tiles_per_row, per_step, *refs):
    n_tok = per_step * (n_src + 4)
    tok_refs = refs[:n_tok]
    gh_ref, bd_ref, wo_ref, w1_ref, w2_ref, gf_ref, o_ref = refs[n_tok:]
    chains = []
    for p in range(per_step):
        x_refs = tok_refs[p * n_src:(p + 1) * n_src]
        yf_ref, yb_ref, gate_ref, mod_ref = (tok_refs[per_step * (n_src + k) + p] for k in range(4))

        def load_x(x_refs=x_refs, p=p):
            return _load_tokens(x_refs, (per_step * pl.program_id(0) + p) % tiles_per_row + t0,
                                n_ctx_tiles)

        def store(val, p=p):
            o_ref[p] = val

        chains.append((p, _out_tile_chain(final, load_x, yf_ref, yb_ref, gate_ref, mod_ref, gh_ref,
                                          bd_ref, wo_ref, w1_ref, w2_ref, gf_ref, store)))
    _interleave(chains)


def _out(xs, yf, yb, gate, mods, g_heads, w_out, w_ff1, w_ff2, g_final, n_ctx_tiles, final):
    B, NT, D = yf.shape
    T = ROW_TILE
    t0 = n_ctx_tiles if final else 0
    tpr = NT // T - t0
    n_all = B * tpr
    per_step = 2 if n_all % 2 == 0 else 1
    bd = jnp.asarray(_NP["bd256"], BF16)
    wspec = lambda shape: pl.BlockSpec(shape, lambda j: (0, 0), pipeline_mode=pl.Buffered(1))

    def row_tile(j, p):
        g = per_step * j + p
        return g // tpr, g % tpr + t0

    def x_specs(p):
        return _token_specs(xs, functools.partial(row_tile, p=p), n_ctx_tiles)

    def tok(p):
        return pl.BlockSpec((1, T, D), lambda j: (*row_tile(j, p), 0))

    def mod_spec(p):
        return pl.BlockSpec((1, 1, 1, 6 * D),
                            lambda j: (row_tile(j, p)[0], jnp.where(row_tile(j, p)[1] < n_ctx_tiles, 0, 1), 0, 0))

    slots = range(per_step)
    out = pl.pallas_call(
        functools.partial(_out_kernel, final, len(xs), n_ctx_tiles, t0, tpr, per_step),
        grid=(n_all // per_step,),
        in_specs=[s for p in slots for s in x_specs(p)]
        + [tok(p) for _ in range(3) for p in slots] + [mod_spec(p) for p in slots]
        + [pl.BlockSpec((1, D), lambda j: (0, 0)),
           pl.BlockSpec((GROUP_W, GROUP_W), lambda j: (0, 0)),
           wspec((D, D)), wspec((D, D_FF)), wspec((D_FF, D)),
           pl.BlockSpec((1, D), lambda j: (0, 0))],
        out_specs=pl.BlockSpec((per_step, T, D), lambda j: (j, 0, 0)),
        out_shape=jax.ShapeDtypeStruct((n_all, T, D), F32),
        compiler_params=pltpu.CompilerParams(dimension_semantics=("parallel",),
                                             vmem_limit_bytes=VMEM_LIMIT),
        name="out_final" if final else "out",
    )(*(list(xs) * per_step), *([yf] * per_step), *([yb] * per_step), *([gate] * per_step),
      *([mods] * per_step), g_heads.reshape(1, D), bd, w_out, w_ff1, w_ff2, g_final.reshape(1, D))
    return out.reshape(B, tpr * T, D)


_REF_LAYOUT = (('hg_q', 256), ('hg_f_fwd', 256), ('hg_f_bwd', 256), ('hg_i', 256), ('hg_g', 256),
               ('ml_q', 256), ('ml_k', 256), ('ml_v', 256), ('ml_if', 16), ('ml_o', 256),
               ('rt_q', 256), ('rt_k', 256), ('rt_v', 256), ('rt_g', 256),
               ('gl_q', 128), ('gl_k', 128), ('gl_v', 256), ('gl_a_fwd', 16), ('gl_a_bwd', 16),
               ('gl_g', 256))
_NEW_ORDER = ('hg_q', 'hg_f_fwd', 'hg_f_bwd', 'hg_i', 'ml_q', 'ml_k', 'ml_v', 'rt_q', 'rt_k', 'rt_v',
              'gl_q', 'gl_k', 'gl_v', 'ml_if', 'gl_a_fwd', 'gl_a_bwd', 'PAD', 'hg_g', 'ml_o', 'rt_g', 'gl_g')


def _reorder_columns(w):
    starts, off = {}, 0
    for name, width in _REF_LAYOUT:
        starts[name] = (off, width)
        off += width
    parts = []
    for name in _NEW_ORDER:
        if name == 'PAD':
            parts.append(jnp.zeros((w.shape[0], 128 - 48), BF16))
        else:
            a, width = starts[name]
            parts.append(w[:, a:a + width].astype(BF16))
    out = jnp.concatenate(parts, axis=1)
    assert out.shape[1] == IN_PAD
    return out


def _rope_tables(n_ctx, n_lat):
    n = jnp.arange(n_lat)
    r = (n // GRID_W).astype(F32)
    col = (n % GRID_W).astype(F32)
    n_freq = HEAD_DIM // 4
    inv = ROPE_BASE ** (-jnp.arange(n_freq, dtype=F32) / n_freq)
    ar = r[:, None] * inv[None, :]
    ac = col[:, None] * inv[None, :]
    cos = jnp.concatenate([jnp.cos(ar), jnp.cos(ar), jnp.cos(ac), jnp.cos(ac)], axis=-1)
    sin = jnp.concatenate([-jnp.sin(ar), jnp.sin(ar), -jnp.sin(ac), jnp.sin(ac)], axis=-1)
    cos = jnp.concatenate([jnp.ones((n_ctx, HEAD_DIM), F32), cos], axis=0)
    sin = jnp.concatenate([jnp.zeros((n_ctx, HEAD_DIM), F32), sin], axis=0)
    return jnp.tile(cos, (1, HEADS)), jnp.tile(sin, (1, HEADS))


def kernel(x, c, ctx, c_ctx, w_ada, b_ada, w_in, g_heads, hgrn_lb_logits, ml_gate_bias,
           rt_decay_logit, gla_w_a, gla_b_a, w_out, w_ff1, w_ff2, g_final):
    B, L, D = x.shape
    Lc = ctx.shape[1]
    depth = w_in.shape[0]
    assert D == D_MODEL and L % ROW_TILE == 0 and Lc % ROW_TILE == 0 and L % GRID_W == 0
    n_ctx_tiles = Lc // ROW_TILE
    n_ctx_chunks = Lc // CHUNK

    xs = (ctx, x)
    cos_t, sin_t = _rope_tables(Lc, L)
    mix = _mix_call(B, Lc + L, n_ctx_chunks)

    n_c = -(-(B + 1) // 8) * 8
    cvecs = jnp.zeros((n_c, D), F32).at[:B].set(c).at[B].set(c_ctx)

    sm = jax.nn.softmax(hgrn_lb_logits.astype(F32), axis=0)
    lb_all = jnp.maximum(jnp.cumsum(sm, axis=0) - sm[:1], 0.0)

    out = None
    for layer in range(depth):
        final = layer == depth - 1
        mod = _ada(cvecs, w_ada[layer], b_ada[layer])
        mods = jnp.stack([jnp.broadcast_to(mod[B], (B, 6 * D)), mod[:B]], axis=1)[:, :, None, :]

        w_in_p = _reorder_columns(w_in[layer])
        lb = lb_all[layer]
        hg_par = jnp.stack([jnp.log(lb[0]) * LOG2E, jnp.log1p(-lb[0]) * LOG2E, 1.0 - lb[0],
                            jnp.log(lb[1]) * LOG2E, jnp.log1p(-lb[1]) * LOG2E, 1.0 - lb[1],
                            jnp.zeros_like(lb[0]), jnp.zeros_like(lb[0])], axis=0)
        ml_bias = jnp.zeros((1, 128), F32).at[0, :16].set(ml_gate_bias[layer].astype(F32).reshape(16))
        wa = jnp.zeros((2, 128, GLA_W), F32)
        wa = wa.at[0, 16:32].set(gla_w_a[layer, 0].astype(F32)).at[1, 32:48].set(gla_w_a[layer, 1].astype(F32))
        wa_hi = wa.astype(BF16)
        wa_lo = (wa - wa_hi.astype(F32)).astype(BF16)
        ba = jnp.zeros((8, GLA_W), F32).at[:2].set(gla_b_a[layer].astype(F32))
        ps, pk, pg, gate = _inp(xs, mods, w_in_p, cos_t, sin_t, hg_par, ml_bias, wa_hi, wa_lo, ba,
                                n_ctx_tiles)

        rt_par = jnp.zeros((8, GROUP_W), F32).at[:2].set(
            jnp.repeat(jax.nn.log_sigmoid(rt_decay_logit[layer].astype(F32)) * LOG2E, HEAD_DIM, axis=-1))
        yf, yb = mix(ps, pk, pg, rt_par)

        res = _out(xs, yf, yb, gate, mods, g_heads[layer], w_out[layer].astype(BF16),
                   w_ff1[layer].astype(BF16), w_ff2[layer].astype(BF16), g_final,
                   n_ctx_tiles, final)
        if final:
            out = res
        else:
            xs = (res,)
    return out
```

```python
import functools

import numpy as np
import jax
import jax.numpy as jnp
from jax import lax
from jax.experimental import pallas as pl
from jax.experimental.pallas import tpu as pltpu

F32 = jnp.float32
BF16 = jnp.bfloat16

D_MODEL = 1024
GROUP_W = 256
HEADS = 4
HEAD_DIM = 64
GLA_DK = 32
GLA_W = HEADS * GLA_DK
GLA_TAU = 16.0
D_FF = 4 * D_MODEL
GRID_W = 64
ROPE_BASE = 10000.0
RMS_EPS = 1e-6

LANES = 128
SUBLANES = 8

CHUNK = 64
STEP_CHUNKS = 4
CHUNK_STAGGER = 3
LEVELS = (32, 16, 8, 4, 2, 1)
ROW_TILE = 256
INP_STAGES = 14
INP_TILES_PER_STEP = 4

W_SMALL = 3072
W_GATES = 3200
IN_PAD = W_GATES + D_MODEL

O_HG_Q, O_HG_V = 0, 256
O_ML_Q, O_ML_K, O_ML_V = 512, 768, 1024
O_RT_Q, O_RT_K, O_RT_V = 1280, 1536, 1792
O_GL_Q, O_GL_K, O_GL_V = 2048, 2176, 2304
PS_W = 2560
G_HG_LF, G_GL_LA, G_SMALL = 0, 256, 384
PG_W = 512

VMEM_LIMIT = 56 * 1024 * 1024

NEG_INF = float("-inf")
LOG2E = 1.4426950408889634


def _dot(a, b):
    return jnp.dot(a, b, preferred_element_type=F32)


def _dot_nt(a, b):
    return lax.dot_general(a, b, (((1,), (1,)), ((), ())), preferred_element_type=F32)


def _dot_tn(a, b):
    return lax.dot_general(a, b, (((0,), (0,)), ((), ())), preferred_element_type=F32)


def _split2(a):
    hi = a.astype(BF16)
    return hi, (a - hi.astype(F32)).astype(BF16)


def _cumsum_rows(x, d):
    n, width = x.shape
    tiles = n // SUBLANES
    x3 = x.reshape(tiles, SUBLANES, width)
    row = lax.broadcasted_iota(jnp.int32, (1, SUBLANES, width), 1)
    s = 1
    while s < SUBLANES:
        if d == 0:
            x3 = x3 + jnp.where(row >= s, pltpu.roll(x3, s, 1), 0.0)
        else:
            x3 = x3 + jnp.where(row < SUBLANES - s, pltpu.roll(x3, SUBLANES - s, 1), 0.0)
        s *= 2
    order = range(tiles) if d == 0 else range(tiles - 1, -1, -1)
    edge = SUBLANES - 1 if d == 0 else 0
    out = [None] * tiles
    carry = None
    for j in order:
        out[j] = x3[j] if carry is None else x3[j] + carry
        carry = out[j][edge:edge + 1]
    return jnp.concatenate(out, axis=0)


def _sel_right(a, m):
    hi, lo = _split2(a)
    return _dot(hi, m) + _dot(lo, m)


def _neg_abs(x):
    return pltpu.bitcast(pltpu.bitcast(x, jnp.uint32) | jnp.uint32(0x80000000), F32)


def _log2_sigmoid(z2):
    return jnp.minimum(z2, 0.0) - jnp.log2(1.0 + jnp.exp2(_neg_abs(z2)))


def _sigmoid(z):
    return 1.0 / (1.0 + jnp.exp(-z))


def _silu(z):
    return z * _sigmoid(z)


def _rms_scale(x):
    return lax.rsqrt(jnp.mean(x * x, axis=-1, keepdims=True) + RMS_EPS)


def _np_consts():
    C = CHUNK
    t = np.arange(C)
    T, U = np.meshgrid(t, t, indexing="ij")
    incl = [U <= T, U >= T]

    lev = np.zeros((2, len(LEVELS) + 1, C, 4 * C), np.float32)
    lev[:, len(LEVELS)] = np.tile(np.eye(C, dtype=np.float32), (1, 4))
    for d in range(2):
        for li, h in enumerate(LEVELS):
            same = (T // (2 * h)) == (U // (2 * h))
            if d == 0:
                pair = same & (T % (2 * h) >= h) & (U % (2 * h) < h)
            else:
                pair = same & (T % (2 * h) < h) & (U % (2 * h) >= h)
            lev[d, li] = np.tile(pair.astype(np.float32), (1, 4))

    incl_neg = np.stack([np.tile(np.where(incl[d], 0.0, NEG_INF).astype(np.float32), (1, 4))
                         for d in range(2)])
    eye = np.tile(np.eye(C, dtype=np.float32), (1, 4))
    dist = np.tile(np.abs(T - U).astype(np.float32), (1, 4))

    head_of_row = np.arange(4 * C) // C
    head_of_lane = np.arange(GROUP_W) // HEAD_DIM
    parity = np.where(head_of_row < 2, head_of_row, -1)
    hmask = np.concatenate(
        [(head_of_lane % 2)[None, :] == parity[:, None],
         head_of_row[:, None] == (np.arange(GLA_W) // GLA_DK)[None, :]], axis=1).astype(np.float32)
    bd256 = ((np.arange(GROUP_W) // HEAD_DIM)[:, None]
             == (np.arange(GROUP_W) // HEAD_DIM)[None, :]).astype(np.float32)
    bd_gl = ((np.arange(GROUP_W) // HEAD_DIM)[:, None]
             == (np.arange(GLA_W) // GLA_DK)[None, :]).astype(np.float32)

    sel_i = np.zeros((2, 128, GROUP_W), np.float32)
    sel_f = np.zeros((2, 128, GROUP_W), np.float32)
    for d in range(2):
        for h in range(HEADS):
            sel_i[d, d * 8 + h, h * HEAD_DIM:(h + 1) * HEAD_DIM] = 1.0
            sel_f[d, d * 8 + 4 + h, h * HEAD_DIM:(h + 1) * HEAD_DIM] = 1.0
    is_f = np.zeros((1, 128), np.float32)
    is_f[0, 4:8] = 1.0
    is_f[0, 12:16] = 1.0
    return dict(lev=lev, incl_neg=incl_neg, eye=eye, dist=dist,
                hmask=hmask, bd256=bd256, bd_gl=bd_gl,
                sel_i=sel_i, sel_f=sel_f, is_f=is_f)


_NP = _np_consts()


def _full_spec(shape):
    n = len(shape)
    return pl.BlockSpec(tuple(shape), lambda *_: (0,) * n)


def _ada_kernel(c_ref, w_ref, b_ref, o_ref):
    act = _silu(c_ref[...]).astype(BF16)
    o_ref[...] = _dot(act, w_ref[...].astype(BF16)) + b_ref[...]


def _ada(cvecs, w, b):
    rows = cvecs.shape[0]
    n = w.shape[1]
    tn = 1024
    return pl.pallas_call(
        _ada_kernel,
        grid=(n // tn,),
        in_specs=[pl.BlockSpec((rows, D_MODEL), lambda i: (0, 0)),
                  pl.BlockSpec((D_MODEL, tn), lambda i: (0, i)),
                  pl.BlockSpec((1, tn), lambda i: (0, i))],
        out_specs=pl.BlockSpec((rows, tn), lambda i: (0, i)),
        out_shape=jax.ShapeDtypeStruct((rows, n), F32),
        compiler_params=pltpu.CompilerParams(dimension_semantics=("arbitrary",),
                                             vmem_limit_bytes=VMEM_LIMIT),
        name="ada",
    )(cvecs, w, b.reshape(1, n))


def _rope(x, cos, sin):
    lane = lax.broadcasted_iota(jnp.int32, x.shape, 1)
    low = (lane % 32) < 16
    partner = jnp.where(low, pltpu.roll(x, GROUP_W - 16, 1), pltpu.roll(x, 16, 1))
    return x * cos + partner * sin


def _token_specs(xs, row_tile, n_ctx_tiles):
    T = ROW_TILE
    if len(xs) == 1:
        return [pl.BlockSpec((1, T, D_MODEL), lambda j: (*row_tile(j), 0))]
    return [pl.BlockSpec((1, T, D_MODEL),
                         lambda j: (row_tile(j)[0], jnp.minimum(row_tile(j)[1], n_ctx_tiles - 1), 0)),
            pl.BlockSpec((1, T, D_MODEL),
                         lambda j: (row_tile(j)[0], jnp.maximum(row_tile(j)[1] - n_ctx_tiles, 0), 0))]


def _load_tokens(x_refs, tile, n_ctx_tiles):
    if len(x_refs) == 1:
        return x_refs[0][0]
    return jnp.where(tile < n_ctx_tiles, x_refs[0][0], x_refs[1][0])


def _inp_tile_chain(p, load_x, mod_ref, cos_ref, sin_ref, w_ref, hg_ref, mlb_ref, isf_ref,
                    wa_hi_ref, wa_lo_ref, ba_ref, ps_ref, pk_ref, pg_ref, gate_ref):
    x = load_x()
    sh = mod_ref[0, 0, :, 0:D_MODEL]
    sc = mod_ref[0, 0, :, D_MODEL:2 * D_MODEL]
    xn = ((x * _rms_scale(x)) * (1.0 + sc) + sh).astype(BF16)

    def proj(a, b):
        return _dot(xn, w_ref[:, a:b])

    G = GROUP_W

    def put(off, width, fn=None):
        def post(res):
            ps_ref[p, :, off:off + width] = (res if fn is None else fn(res)).astype(BF16)
        return post

    def hgrn_decay(d):
        def post(z):
            log_lb = hg_ref[3 * d + 0:3 * d + 1, :]
            log_1mlb = hg_ref[3 * d + 1:3 * d + 2, :]
            one_m_lb = hg_ref[3 * d + 2:3 * d + 3, :]
            z2 = z * LOG2E
            ls = _log2_sigmoid(z2)
            other = log_1mlb + ls
            pg_ref[d, p, :, G_HG_LF:G_HG_LF + G] = (
                jnp.maximum(log_lb, other) + jnp.log2(1.0 + jnp.exp2(_neg_abs(log_lb - other))))
            pk_ref[d, p] = (one_m_lb * jnp.exp2(ls - z2)).astype(BF16)
        return post

    def narrow(small):
        small_hi, small_lo = _split2(small)
        pre2 = (small + mlb_ref[...]) * LOG2E
        gate_logs = jnp.where(isf_ref[...] > 0.5, _log2_sigmoid(pre2), pre2)
        for d in range(2):
            za = (_dot(small_hi, wa_hi_ref[d]) + _dot(small_lo, wa_hi_ref[d])
                  + _dot(small_hi, wa_lo_ref[d]) + ba_ref[d:d + 1, :])
            pg_ref[d, p, :, G_GL_LA:G_GL_LA + GLA_W] = _log2_sigmoid(za * LOG2E) * (1.0 / GLA_TAU)
            pg_ref[d, p, :, G_SMALL:PG_W] = gate_logs

    def gates(half, fn):
        def post(g):
            gate_ref[p, :, half * 2 * G:(half + 1) * 2 * G] = fn(g).astype(BF16)
        return post

    def rope(scale):
        return lambda r: _rope(r * scale if scale != 1.0 else r, cos_ref[...], sin_ref[...])

    o = 10 * G
    stages = [
        (W_GATES, W_GATES + 2 * G, gates(0, _sigmoid)),
        (0, G, put(O_HG_Q, G, _silu)),
        (G, 2 * G, hgrn_decay(0)),
        (3 * G, 5 * G, put(O_HG_V, 2 * G)),
        (2 * G, 3 * G, hgrn_decay(1)),
        (5 * G, 6 * G, put(O_ML_K, G, lambda r: r * (HEAD_DIM ** -0.5))),
        (W_GATES + 2 * G, IN_PAD, gates(1, _silu)),
        (6 * G, 7 * G, put(O_ML_V, G)),
        (7 * G, 8 * G, put(O_RT_Q, G, rope(1.0))),
        (9 * G, o + GLA_W, put(O_RT_V, G + GLA_W)),
        (8 * G, 9 * G, put(O_RT_K, G, rope(HEAD_DIM ** -0.5))),
        (o + GLA_W, o + 2 * GLA_W, put(O_GL_K, GLA_W, lambda r: r * (GLA_DK ** -0.5))),
        (W_SMALL, W_GATES, narrow),
        (o + 2 * GLA_W, W_SMALL, put(O_GL_V, G)),
    ]
    assert len(stages) == INP_STAGES
    pending = None
    for a, b, post in stages:
        res = proj(a, b)
        if pending is not None:
            pending[1](pending[0])
        pending = (res, post)
        yield
    pending[1](pending[0])


def _inp_kernel(n_src, n_ctx_tiles, tiles_per_row, per_step, *refs):
    n_tok = per_step * (n_src + 3)
    tok_refs = refs[:n_tok]
    w_ref, hg_ref, mlb_ref, isf_ref, wa_hi_ref, wa_lo_ref, ba_ref = refs[n_tok:n_tok + 7]
    ps_ref, pk_ref, pg_ref, gate_ref = refs[n_tok + 7:]
    chains = []
    for p in range(per_step):
        x_refs = tok_refs[p * n_src:(p + 1) * n_src]
        mod_ref, cos_ref, sin_ref = (tok_refs[per_step * (n_src + k) + p] for k in range(3))

        def load_x(x_refs=x_refs, p=p):
            return _load_tokens(x_refs, (per_step * pl.program_id(0) + p) % tiles_per_row, n_ctx_tiles)

        chains.append((p * (INP_STAGES - 1),
                       _inp_tile_chain(p, load_x, mod_ref, cos_ref, sin_ref, w_ref, hg_ref, mlb_ref,
                                       isf_ref, wa_hi_ref, wa_lo_ref, ba_ref, ps_ref, pk_ref, pg_ref,
                                       gate_ref)))
    _interleave(chains)


def _inp(xs, mods, w_in_p, cos_t, sin_t, hg_par, ml_bias, wa_hi, wa_lo, ba, n_ctx_tiles):
    B = xs[0].shape[0]
    NT = sum(a.shape[1] for a in xs)
    T = ROW_TILE
    tpr = NT // T
    n_all = B * tpr
    most = INP_TILES_PER_STEP if len(xs) == 1 else 2
    per_step = next(n for n in (most, 2, 1) if n_all % n == 0)
    params = [hg_par, ml_bias, jnp.asarray(_NP["is_f"]), wa_hi, wa_lo, ba]

    def row_tile(j, p):
        g = per_step * j + p
        return g // tpr, g % tpr

    def x_specs(p):
        return _token_specs(xs, functools.partial(row_tile, p=p), n_ctx_tiles)

    def mod_spec(p):
        return pl.BlockSpec((1, 1, 1, 6 * D_MODEL),
                            lambda j: (row_tile(j, p)[0], jnp.where(row_tile(j, p)[1] < n_ctx_tiles, 0, 1), 0, 0))

    def rope_spec(p):
        return pl.BlockSpec((T, GROUP_W), lambda j: (row_tile(j, p)[1], 0))

    slots = range(per_step)
    outs = pl.pallas_call(
        functools.partial(_inp_kernel, len(xs), n_ctx_tiles, tpr, per_step),
        grid=(n_all // per_step,),
        in_specs=[s for p in slots for s in x_specs(p)]
        + [mod_spec(p) for p in slots] + [rope_spec(p) for p in slots] + [rope_spec(p) for p in slots]
        + [pl.BlockSpec((D_MODEL, IN_PAD), lambda j: (0, 0), pipeline_mode=pl.Buffered(1))]
        + [_full_spec(a.shape) for a in params],
        out_specs=[pl.BlockSpec((per_step, T, PS_W), lambda j: (j, 0, 0)),
                   pl.BlockSpec((2, per_step, T, GROUP_W), lambda j: (0, j, 0, 0)),
                   pl.BlockSpec((2, per_step, T, PG_W), lambda j: (0, j, 0, 0)),
                   pl.BlockSpec((per_step, T, D_MODEL), lambda j: (j, 0, 0))],
        out_shape=[jax.ShapeDtypeStruct((n_all, T, PS_W), BF16),
                   jax.ShapeDtypeStruct((2, n_all, T, GROUP_W), BF16),
                   jax.ShapeDtypeStruct((2, n_all, T, PG_W), F32),
                   jax.ShapeDtypeStruct((n_all, T, D_MODEL), BF16)],
        compiler_params=pltpu.CompilerParams(dimension_semantics=("parallel",),
                                             vmem_limit_bytes=VMEM_LIMIT),
        name="inp",
    )(*(list(xs) * per_step), *([mods] * per_step), *([cos_t] * per_step), *([sin_t] * per_step),
      w_in_p, *params)
    ps, pk, pg, gate = outs
    return (ps.reshape(B, NT, PS_W), pk.reshape(2, B, NT, GROUP_W), pg.reshape(2, B, NT, PG_W),
            gate.reshape(B, NT, D_MODEL))


def _head_blocks(a16, hmask_ref):
    C, W = a16.shape
    if W == LANES:
        return jnp.concatenate([a16, a16, a16, a16], axis=0) * hmask_ref[:, GROUP_W:GROUP_W + W]
    even = a16 * hmask_ref[0:C, 0:W]
    odd = a16 * hmask_ref[C:2 * C, 0:W]
    zero = jnp.zeros((C, LANES), a16.dtype)
    return jnp.concatenate([
        jnp.concatenate([even[:, 0:LANES], zero], axis=1),
        jnp.concatenate([odd[:, 0:LANES], zero], axis=1),
        jnp.concatenate([zero, even[:, LANES:W]], axis=1),
        jnp.concatenate([zero, odd[:, LANES:W]], axis=1)], axis=0)


def _last_row(a, d):
    return a[CHUNK - 1:CHUNK, :] if d == 0 else a[0:1, :]


def _level_exponent(d, h, log_a, b, row4):
    C, W = log_a.shape
    if h >= 4:
        b3 = b.reshape(C // (2 * h), 2 * h, W)
        r = h - 1 if d == 0 else h
        ref = jnp.broadcast_to(b3[:, r:r + 1, :], b3.shape).reshape(C, W)
        return _neg_abs(b - ref)
    up = pltpu.roll(log_a, C - 1, 0)
    dn = pltpu.roll(log_a, 1, 0)
    if d == 0:
        return jnp.where(row4 == 0, up, jnp.where(row4 == 1, 0.0,
                                                  log_a + jnp.where(row4 == 3, dn, 0.0)))
    return jnp.where(row4 == 0, log_a + up,
                     jnp.where(row4 == 1, log_a, jnp.where(row4 == 2, 0.0, dn)))


def _interleave(chains):
    live = list(chains)
    rnd = 0
    while live:
        for item in list(live):
            start, g = item
            if rnd >= start:
                try:
                    next(g)
                except StopIteration:
                    live.remove(item)
        rnd += 1


def _update_state(st_ref, d, st, decay_row, bd_state, upd):
    rows = st.shape[0] // HEADS
    per_head = st.shape[1] // HEADS
    for h in range(HEADS):
        j = (h * per_head) // LANES
        rs = slice(h * rows, (h + 1) * rows)
        cs = slice(j * LANES, (j + 1) * LANES)
        st_ref[d, rs, cs] = decay_row[:, cs] * st[rs, cs] + bd_state[rs, cs] * upd[rs, cs]


def _vector_decay_chain(d, q16, k16, v16, log_a, put, o_y, st_ref, lev_ref, hmask_ref, bd_state):
    C, W = log_a.shape
    G = GROUP_W
    b = _cumsum_rows(log_a, d)
    row4 = lax.broadcasted_iota(jnp.int32, (C, W), 0) & 3
    order = sorted(range(len(LEVELS)), key=lambda li: LEVELS[li] >= 4)
    attn = None
    pending = None
    for li in order:
        h = LEVELS[li]
        if h == 1:
            qa = jnp.concatenate([q16 * jnp.exp2(log_a).astype(BF16), q16], axis=0)
            kb = k16
        else:
            f16 = jnp.exp2(_level_exponent(d, h, log_a, b, row4)).astype(BF16)
            qa = q16 * f16
            kb = k16 * f16
        part = _dot_nt(qa, _head_blocks(kb, hmask_ref))
        if pending is not None:
            attn = pending if attn is None else attn + pending
        yield
        part = part.astype(BF16)
        pending = (part * lev_ref[d, li] if h > 1 else
                   part[0:C] * lev_ref[d, li] + part[C:2 * C] * lev_ref[d, len(LEVELS)])
    attn = attn + pending

    st = st_ref[d]
    b_end = _last_row(b, d)
    q_state = q16 * jnp.exp2(b).astype(BF16)
    k_end = k16 * jnp.exp2(b_end - b).astype(BF16)
    o = _dot(attn, _head_blocks(v16, hmask_ref)) + _dot_nt(q_state, st.astype(BF16))
    upd = _dot_tn(v16, k_end)
    yield
    put(o_y, o)
    _update_state(st_ref, d, st, jnp.exp2(b_end), bd_state, upd)


def _mlstm_chain(d, q16, k16, v16, small, put, s_ml, n_ml, m_ml, incl_neg, eye,
                 seli, self_, hmask_ref, bd256, bd256_16):
    C = CHUNK
    G = GROUP_W
    i_bc = _sel_right(small, seli)
    b_col = _sel_right(_cumsum_rows(small, d), self_)
    qk = _dot_nt(q16, _head_blocks(k16, hmask_ref))
    yield
    b_row = jnp.sum(b_col * eye, axis=0, keepdims=True)
    i_row = jnp.sum(i_bc * eye, axis=0, keepdims=True)
    w = b_col - b_row + i_row + incl_neg
    row_max = [jnp.max(w[:, h * HEAD_DIM:(h + 1) * HEAD_DIM], axis=-1, keepdims=True)
               for h in range(HEADS)]
    yield
    m_prev = m_ml[d, 0:1, :]
    inter = b_col + m_prev
    lane_head = lax.broadcasted_iota(jnp.int32, (C, G), 1) // HEAD_DIM
    m_t = None
    for h in range(HEADS):
        mh = jnp.maximum(row_max[h], inter[:, h * HEAD_DIM:h * HEAD_DIM + 1])
        mh = jnp.broadcast_to(mh, (C, G))
        m_t = mh if m_t is None else jnp.where(lane_head == h, mh, m_t)
    g_in = jnp.exp2(inter - m_t)
    s = (qk * jnp.exp2(w - m_t)).astype(BF16)
    c_st = s_ml[d]
    n_row = n_ml[d, 0:1, :]
    qn = (q16.astype(F32) * n_row).astype(BF16)
    num_a = _dot(s, _head_blocks(v16, hmask_ref))
    num_b = _dot(q16, c_st.astype(BF16))
    den_ab = _dot(jnp.concatenate([s, qn], axis=0), bd256_16)
    m_new = _last_row(m_t, d)
    b_end = _last_row(b_col, d)
    a_state = jnp.exp2(b_end + m_prev - m_new)
    kw = k16.astype(F32) * jnp.exp2(b_end - b_col + i_bc - m_new)
    upd = _dot_tn(kw.astype(BF16), v16)
    yield
    num = num_a + g_in * num_b
    den = den_ab[0:C] + g_in * den_ab[C:2 * C]
    put(G, num / jnp.maximum(jnp.abs(den), jnp.exp2(-m_t)))
    _update_state(s_ml, d, c_st, a_state, bd256, upd)
    n_ml[d] = jnp.broadcast_to(a_state * n_row + jnp.sum(kw, axis=0, keepdims=True), (8, G))
    m_ml[d] = jnp.broadcast_to(m_new, (8, G))


def _retention_chain(d, q16, k16, v16, put, s_rt, log_g, dist, incl_neg, hmask_ref, bd256):
    C = CHUNK
    G = GROUP_W
    qk = _dot_nt(q16, _head_blocks(k16, hmask_ref))
    decay = jnp.exp2(log_g * dist + incl_neg)
    yield
    rowc = lax.broadcasted_iota(jnp.int32, (C, 1), 0).astype(F32)
    cnt = (rowc + 1.0) if d == 0 else (C - rowc)
    q_state = q16 * jnp.exp2(log_g * cnt).astype(BF16)
    k_end = k16 * jnp.exp2(log_g * (C - cnt)).astype(BF16)
    st = s_rt[d]
    o = _dot((qk * decay).astype(BF16), _head_blocks(v16, hmask_ref)) + _dot(q_state, st.astype(BF16))
    upd = _dot_tn(k_end, v16)
    yield
    put(2 * G, o)
    _update_state(s_rt, d, st, jnp.exp2(log_g * float(C)), bd256, upd)


def _mix_kernel(psf_ref, psb_ref, pkf_ref, pkb_ref, pgf_ref, pgb_ref, rtg_ref,
                lev_ref, incl_ref, eye_ref, dist_ref,
                hmask_ref, bd256_ref, bdgl_ref, seli_ref, self_ref,
                yf_ref, yb_ref,
                s_hg, s_rt, s_gl, s_ml, n_ml, m_ml):
    C = CHUNK
    G = GROUP_W

    @pl.when(pl.program_id(1) == 0)
    def _():
        s_hg[...] = jnp.zeros_like(s_hg)
        s_rt[...] = jnp.zeros_like(s_rt)
        s_gl[...] = jnp.zeros_like(s_gl)
        s_ml[...] = jnp.zeros_like(s_ml)
        n_ml[...] = jnp.zeros_like(n_ml)
        m_ml[...] = jnp.zeros_like(m_ml)

    bd256 = bd256_ref[...]
    bd256_16 = bd256.astype(BF16)

    def deferred(make):
        yield from make()

    chains = []
    for i in range(STEP_CHUNKS):
        for d, (ps_ref, pk_ref, pg_ref, y_ref) in enumerate(((psf_ref, pkf_ref, pgf_ref, yf_ref),
                                                             (psb_ref, pkb_ref, pgb_ref, yb_ref))):
            sub = i if d == 0 else STEP_CHUNKS - 1 - i
            rows = slice(sub * C, (sub + 1) * C)

            def ps(off, width, ps_ref=ps_ref, rows=rows):
                return ps_ref[0, rows, off:off + width]

            def pg(off, width, pg_ref=pg_ref, rows=rows):
                return pg_ref[0, 0, rows, off:off + width]

            def put(off, val, y_ref=y_ref, rows=rows):
                y_ref[0, rows, off:off + G] = val

            def hgrn(d=d, ps=ps, pg=pg, put=put, pk_ref=pk_ref, rows=rows):
                return _vector_decay_chain(d, ps(O_HG_Q, G), pk_ref[0, 0, rows, :], ps(O_HG_V, G),
                                           pg(G_HG_LF, G), put, 0, s_hg, lev_ref, hmask_ref, bd256)

            def mlstm(d=d, ps=ps, pg=pg, put=put):
                return _mlstm_chain(d, ps(O_ML_Q, G), ps(O_ML_K, G), ps(O_ML_V, G),
                                    pg(G_SMALL, PG_W - G_SMALL), put, s_ml, n_ml, m_ml, incl_ref[d],
                                    eye_ref[...], seli_ref[d], self_ref[d], hmask_ref, bd256, bd256_16)

            def retention(d=d, ps=ps, put=put):
                return _retention_chain(d, ps(O_RT_Q, G), ps(O_RT_K, G), ps(O_RT_V, G), put, s_rt,
                                        rtg_ref[d:d + 1, :], dist_ref[...], incl_ref[d], hmask_ref, bd256)

            def gla(d=d, ps=ps, pg=pg, put=put):
                return _vector_decay_chain(d, ps(O_GL_Q, GLA_W), ps(O_GL_K, GLA_W), ps(O_GL_V, G),
                                           pg(G_GL_LA, GLA_W), put, 3 * G, s_gl, lev_ref, hmask_ref,
                                           bdgl_ref[...])

            chains += [(i * CHUNK_STAGGER, deferred(make)) for make in (hgrn, mlstm, retention, gla)]
    _interleave(chains)


def _mix_call(B, NT, n_ctx_chunks):
    C = CHUNK * STEP_CHUNKS
    assert NT % C == 0 and n_ctx_chunks % STEP_CHUNKS == 0
    NC = NT // C
    n_ctx_blocks = n_ctx_chunks // STEP_CHUNKS
    G = GROUP_W

    def bwd_chunk(j):
        return jnp.where(j < n_ctx_blocks, n_ctx_blocks - 1 - j, NC - 1 - (j - n_ctx_blocks))

    def fwd_map(b, j):
        return (b, j, 0)

    def bwd_map(b, j):
        return (b, bwd_chunk(j), 0)

    consts = [jnp.asarray(_NP["lev"], BF16), jnp.asarray(_NP["incl_neg"]),
              jnp.asarray(_NP["eye"]), jnp.asarray(_NP["dist"]),
              jnp.asarray(_NP["hmask"], BF16),
              jnp.asarray(_NP["bd256"]), jnp.asarray(_NP["bd_gl"]),
              jnp.asarray(_NP["sel_i"], BF16), jnp.asarray(_NP["sel_f"], BF16)]
    call = pl.pallas_call(
        _mix_kernel,
        grid=(B, NC),
        in_specs=[pl.BlockSpec((1, C, PS_W), fwd_map), pl.BlockSpec((1, C, PS_W), bwd_map),
                  pl.BlockSpec((1, 1, C, G), lambda b, j: (0, b, j, 0)),
                  pl.BlockSpec((1, 1, C, G), lambda b, j: (1, b, bwd_chunk(j), 0)),
                  pl.BlockSpec((1, 1, C, PG_W), lambda b, j: (0, b, j, 0)),
                  pl.BlockSpec((1, 1, C, PG_W), lambda b, j: (1, b, bwd_chunk(j), 0)),
                  _full_spec((8, G))] + [_full_spec(a.shape) for a in consts],
        out_specs=[pl.BlockSpec((1, C, D_MODEL), fwd_map), pl.BlockSpec((1, C, D_MODEL), bwd_map)],
        out_shape=[jax.ShapeDtypeStruct((B, NT, D_MODEL), F32)] * 2,
        scratch_shapes=[pltpu.VMEM((2, G, G), F32), pltpu.VMEM((2, G, G), F32),
                        pltpu.VMEM((2, G, GLA_W), F32), pltpu.VMEM((2, G, G), F32),
                        pltpu.VMEM((2, 8, G), F32), pltpu.VMEM((2, 8, G), F32)],
        compiler_params=pltpu.CompilerParams(dimension_semantics=("parallel", "arbitrary"),
                                             vmem_limit_bytes=VMEM_LIMIT),
        name="mix",
    )
    return lambda ps, pk, pg, rt_par: call(ps, ps, pk, pk, pg, pg, rt_par, *consts)


def _out_tile_chain(final, load_x, yf_ref, yb_ref, gate_ref, mod_ref, gh_ref, bd_ref,
                    wo_ref, w1_ref, w2_ref, gf_ref, store):
    D = D_MODEL
    bd = bd_ref[...]
    for g in range(D // GROUP_W):
        sl = slice(g * GROUP_W, (g + 1) * GROUP_W)
        y = yf_ref[0, :, sl] + yb_ref[0, :, sl]
        msq = _dot((y * y).astype(BF16), bd) * (1.0 / HEAD_DIM)
        yn = y * lax.rsqrt(msq + RMS_EPS) * gh_ref[:, sl] * gate_ref[0, :, sl].astype(F32)
        part = _dot(yn.astype(BF16), wo_ref[sl, :])
        mixed = part if g == 0 else mixed + part
    yield
    g1 = mod_ref[0, 0, :, 2 * D:3 * D]
    sh2 = mod_ref[0, 0, :, 3 * D:4 * D]
    sc2 = mod_ref[0, 0, :, 4 * D:5 * D]
    g2 = mod_ref[0, 0, :, 5 * D:6 * D]
    x1 = load_x() + g1 * mixed
    hin = ((x1 * _rms_scale(x1)) * (1.0 + sc2) + sh2).astype(BF16)
    hid = _dot(hin, w1_ref[...])
    yield
    hid = jnp.maximum(hid, 0.0)
    ff = _dot((hid * hid).astype(BF16), w2_ref[...])
    yield
    x2 = x1 + g2 * ff
    if final:
        x2 = (x2 * _rms_scale(x2)) * gf_ref[...]
    store(x2)


def _out_kernel(final, n_src, n_ctx_tiles, t0, tiles_per_row, per_step, *refs):
    n_tok = per_step * (n_src + 4)
    tok_refs = refs[:n_tok]
    gh_ref, bd_ref, wo_ref, w1_ref, w2_ref, gf_ref, o_ref = refs[n_tok:]
    chains = []
    for p in range(per_step):
        x_refs = tok_refs[p * n_src:(p + 1) * n_src]
        yf_ref, yb_ref, gate_ref, mod_ref = (tok_refs[per_step * (n_src + k) + p] for k in range(4))

        def load_x(x_refs=x_refs, p=p):
            return _load_tokens(x_refs, (per_step * pl.program_id(0) + p) % tiles_per_row + t0,
                                n_ctx_tiles)

        def store(val, p=p):
            o_ref[p] = val

        chains.append((p, _out_tile_chain(final, load_x, yf_ref, yb_ref, gate_ref, mod_ref, gh_ref,
                                          bd_ref, wo_ref, w1_ref, w2_ref, gf_ref, store)))
    _interleave(chains)


def _out(xs, yf, yb, gate, mods, g_heads, w_out, w_ff1, w_ff2, g_final, n_ctx_tiles, final):
    B, NT, D = yf.shape
    T = ROW_TILE
    t0 = n_ctx_tiles if final else 0
    tpr = NT // T - t0
    n_all = B * tpr
    per_step = 2 if n_all % 2 == 0 else 1
    bd = jnp.asarray(_NP["bd256"], BF16)
    wspec = lambda shape: pl.BlockSpec(shape, lambda j: (0, 0), pipeline_mode=pl.Buffered(1))

    def row_tile(j, p):
        g = per_step * j + p
        return g // tpr, g % tpr + t0

    def x_specs(p):
        return _token_specs(xs, functools.partial(row_tile, p=p), n_ctx_tiles)

    def tok(p):
        return pl.BlockSpec((1, T, D), lambda j: (*row_tile(j, p), 0))

    def mod_spec(p):
        return pl.BlockSpec((1, 1, 1, 6 * D),
                            lambda j: (row_tile(j, p)[0], jnp.where(row_tile(j, p)[1] < n_ctx_tiles, 0, 1), 0, 0))

    slots = range(per_step)
    out = pl.pallas_call(
        functools.partial(_out_kernel, final, len(xs), n_ctx_tiles, t0, tpr, per_step),
        grid=(n_all // per_step,),
        in_specs=[s for p in slots for s in x_specs(p)]
        + [tok(p) for _ in range(3) for p in slots] + [mod_spec(p) for p in slots]
        + [pl.BlockSpec((1, D), lambda j: (0, 0)),
           pl.BlockSpec((GROUP_W, GROUP_W), lambda j: (0, 0)),
           wspec((D, D)), wspec((D, D_FF)), wspec((D_FF, D)),
           pl.BlockSpec((1, D), lambda j: (0, 0))],
        out_specs=pl.BlockSpec((per_step, T, D), lambda j: (j, 0, 0)),
        out_shape=jax.ShapeDtypeStruct((n_all, T, D), F32),
        compiler_params=pltpu.CompilerParams(dimension_semantics=("parallel",),
                                             vmem_limit_bytes=VMEM_LIMIT),
        name="out_final" if final else "out",
    )(*(list(xs) * per_step), *([yf] * per_step), *([yb] * per_step), *([gate] * per_step),
      *([mods] * per_step), g_heads.reshape(1, D), bd, w_out, w_ff1, w_ff2, g_final.reshape(1, D))
    return out.reshape(B, tpr * T, D)


_REF_LAYOUT = (('hg_q', 256), ('hg_f_fwd', 256), ('hg_f_bwd', 256), ('hg_i', 256), ('hg_g', 256),
               ('ml_q', 256), ('ml_k', 256), ('ml_v', 256), ('ml_if', 16), ('ml_o', 256),
               ('rt_q', 256), ('rt_k', 256), ('rt_v', 256), ('rt_g', 256),
               ('gl_q', 128), ('gl_k', 128), ('gl_v', 256), ('gl_a_fwd', 16), ('gl_a_bwd', 16),
               ('gl_g', 256))
_NEW_ORDER = ('hg_q', 'hg_f_fwd', 'hg_f_bwd', 'hg_i', 'ml_q', 'ml_k', 'ml_v', 'rt_q', 'rt_k', 'rt_v',
              'gl_q', 'gl_k', 'gl_v', 'ml_if', 'gl_a_fwd', 'gl_a_bwd', 'PAD', 'hg_g', 'ml_o', 'rt_g', 'gl_g')


def _reorder_columns(w):
    starts, off = {}, 0
    for name, width in _REF_LAYOUT:
        starts[name] = (off, width)
        off += width
    parts = []
    for name in _NEW_ORDER:
        if name == 'PAD':
            parts.append(jnp.zeros((w.shape[0], 128 - 48), BF16))
        else:
            a, width = starts[name]
            parts.append(w[:, a:a + width].astype(BF16))
    out = jnp.concatenate(parts, axis=1)
    assert out.shape[1] == IN_PAD
    return out


def _rope_tables(n_ctx, n_lat):
    rows = n_lat // GRID_W
    n_freq = HEAD_DIM // 4
    inv = np.float32(ROPE_BASE) ** (-np.arange(n_freq, dtype=np.float32) / n_freq)
    ar = np.arange(rows, dtype=np.float32)[:, None] * inv[None, :]
    ac = np.arange(GRID_W, dtype=np.float32)[:, None] * inv[None, :]

    def expand(per_row, per_col, ctx_value):
        lat = jnp.concatenate([jnp.repeat(jnp.asarray(per_row), GRID_W, axis=0),
                               jnp.tile(jnp.asarray(per_col), (rows, 1))], axis=-1)
        full = jnp.concatenate([jnp.full((n_ctx, HEAD_DIM), ctx_value, F32), lat], axis=0)
        return jnp.tile(full, (1, HEADS))

    cos = expand(np.concatenate([np.cos(ar), np.cos(ar)], -1), np.concatenate([np.cos(ac), np.cos(ac)], -1), 1.0)
    sin = expand(np.concatenate([-np.sin(ar), np.sin(ar)], -1), np.concatenate([-np.sin(ac), np.sin(ac)], -1), 0.0)
    return cos, sin


def kernel(x, c, ctx, c_ctx, w_ada, b_ada, w_in, g_heads, hgrn_lb_logits, ml_gate_bias,
           rt_decay_logit, gla_w_a, gla_b_a, w_out, w_ff1, w_ff2, g_final):
    B, L, D = x.shape
    Lc = ctx.shape[1]
    depth = w_in.shape[0]
    assert D == D_MODEL and L % ROW_TILE == 0 and Lc % ROW_TILE == 0 and L % GRID_W == 0
    n_ctx_tiles = Lc // ROW_TILE
    n_ctx_chunks = Lc // CHUNK

    xs = (ctx, x)
    cos_t, sin_t = _rope_tables(Lc, L)
    mix = _mix_call(B, Lc + L, n_ctx_chunks)

    n_c = -(-(B + 1) // 8) * 8
    cvecs = jnp.zeros((n_c, D), F32).at[:B].set(c).at[B].set(c_ctx)

    sm = jax.nn.softmax(hgrn_lb_logits.astype(F32), axis=0)
    lb_all = jnp.maximum(jnp.cumsum(sm, axis=0) - sm[:1], 0.0)

    out = None
    for layer in range(depth):
        final = layer == depth - 1
        mod = _ada(cvecs, w_ada[layer], b_ada[layer])
        mods = jnp.stack([jnp.broadcast_to(mod[B], (B, 6 * D)), mod[:B]], axis=1)[:, :, None, :]

        w_in_p = _reorder_columns(w_in[layer])
        lb = lb_all[layer]
        hg_par = jnp.stack([jnp.log(lb[0]) * LOG2E, jnp.log1p(-lb[0]) * LOG2E, 1.0 - lb[0],
                            jnp.log(lb[1]) * LOG2E, jnp.log1p(-lb[1]) * LOG2E, 1.0 - lb[1],
                            jnp.zeros_like(lb[0]), jnp.zeros_like(lb[0])], axis=0)
        ml_bias = jnp.zeros((1, 128), F32).at[0, :16].set(ml_gate_bias[layer].astype(F32).reshape(16))
        wa = jnp.zeros((2, 128, GLA_W), F32)
        wa = wa.at[0, 16:32].set(gla_w_a[layer, 0].astype(F32)).at[1, 32:48].set(gla_w_a[layer, 1].astype(F32))
        wa_hi = wa.astype(BF16)
        wa_lo = (wa - wa_hi.astype(F32)).astype(BF16)
        ba = jnp.zeros((8, GLA_W), F32).at[:2].set(gla_b_a[layer].astype(F32))
        ps, pk, pg, gate = _inp(xs, mods, w_in_p, cos_t, sin_t, hg_par, ml_bias, wa_hi, wa_lo, ba,
                                n_ctx_tiles)

        rt_par = jnp.zeros((8, GROUP_W), F32).at[:2].set(
            jnp.repeat(jax.nn.log_sigmoid(rt_decay_logit[layer].astype(F32)) * LOG2E, HEAD_DIM, axis=-1))
        yf, yb = mix(ps, pk, pg, rt_par)

        res = _out(xs, yf, yb, gate, mods, g_heads[layer], w_out[layer].astype(BF16),
                   w_ff1[layer].astype(BF16), w_ff2[layer].astype(BF16), g_final,
                   n_ctx_tiles, final)
        if final:
            out = res
        else:
            xs = (res,)
    return out
```

```python
import functools

import numpy as np
import jax
import jax.numpy as jnp
from jax import lax
from jax.experimental import pallas as pl
from jax.experimental.pallas import tpu as pltpu

F32 = jnp.float32
BF16 = jnp.bfloat16

D_MODEL = 1024
GROUP_W = 256
HEADS = 4
HEAD_DIM = 64
GLA_DK = 32
GLA_W = HEADS * GLA_DK
GLA_TAU = 16.0
D_FF = 4 * D_MODEL
GRID_W = 64
ROPE_BASE = 10000.0
RMS_EPS = 1e-6

LANES = 128
SUBLANES = 8

CHUNK = 64
STEP_CHUNKS = 4
CHUNK_STAGGER = 3
LEVELS = (32, 16, 8, 4, 2, 1)
ROW_TILE = 256
INP_STAGES = 14
INP_TILES_PER_STEP = 4

W_SMALL = 3072
W_GATES = 3200
IN_PAD = W_GATES + D_MODEL

O_HG_Q, O_HG_V = 0, 256
O_ML_Q, O_ML_K, O_ML_V = 512, 768, 1024
O_RT_Q, O_RT_K, O_RT_V = 1280, 1536, 1792
O_GL_Q, O_GL_K, O_GL_V = 2048, 2176, 2304
PS_W = 2560
G_HG_LF, G_GL_LA, G_SMALL = 0, 256, 384
PG_W = 512

VMEM_LIMIT = 56 * 1024 * 1024

NEG_INF = float("-inf")
LOG2E = 1.4426950408889634


def _dot(a, b):
    return jnp.dot(a, b, preferred_element_type=F32)


def _dot_nt(a, b):
    return lax.dot_general(a, b, (((1,), (1,)), ((), ())), preferred_element_type=F32)


def _dot_tn(a, b):
    return lax.dot_general(a, b, (((0,), (0,)), ((), ())), preferred_element_type=F32)


def _split2(a):
    hi = a.astype(BF16)
    return hi, (a - hi.astype(F32)).astype(BF16)


def _cumsum_rows(x, d):
    n, width = x.shape
    tiles = n // SUBLANES
    x3 = x.reshape(tiles, SUBLANES, width)
    row = lax.broadcasted_iota(jnp.int32, (1, SUBLANES, width), 1)
    s = 1
    while s < SUBLANES:
        if d == 0:
            x3 = x3 + jnp.where(row >= s, pltpu.roll(x3, s, 1), 0.0)
        else:
            x3 = x3 + jnp.where(row < SUBLANES - s, pltpu.roll(x3, SUBLANES - s, 1), 0.0)
        s *= 2
    order = range(tiles) if d == 0 else range(tiles - 1, -1, -1)
    edge = SUBLANES - 1 if d == 0 else 0
    out = [None] * tiles
    carry = None
    for j in order:
        out[j] = x3[j] if carry is None else x3[j] + carry
        carry = out[j][edge:edge + 1]
    return jnp.concatenate(out, axis=0)


def _sel_right(a, m):
    hi, lo = _split2(a)
    return _dot(hi, m) + _dot(lo, m)


def _neg_abs(x):
    return pltpu.bitcast(pltpu.bitcast(x, jnp.uint32) | jnp.uint32(0x80000000), F32)


def _log2_sigmoid(z2):
    return jnp.minimum(z2, 0.0) - jnp.log2(1.0 + jnp.exp2(_neg_abs(z2)))


def _sigmoid(z):
    return 1.0 / (1.0 + jnp.exp(-z))


def _silu(z):
    return z * _sigmoid(z)


def _rms_scale(x):
    return lax.rsqrt(jnp.mean(x * x, axis=-1, keepdims=True) + RMS_EPS)


def _np_consts():
    C = CHUNK
    t = np.arange(C)
    T, U = np.meshgrid(t, t, indexing="ij")
    incl = [U <= T, U >= T]

    lev = np.zeros((2, len(LEVELS) + 1, C, 4 * C), np.float32)
    lev[:, len(LEVELS)] = np.tile(np.eye(C, dtype=np.float32), (1, 4))
    for d in range(2):
        for li, h in enumerate(LEVELS):
            same = (T // (2 * h)) == (U // (2 * h))
            if d == 0:
                pair = same & (T % (2 * h) >= h) & (U % (2 * h) < h)
            else:
                pair = same & (T % (2 * h) < h) & (U % (2 * h) >= h)
            lev[d, li] = np.tile(pair.astype(np.float32), (1, 4))

    incl_neg = np.stack([np.tile(np.where(incl[d], 0.0, NEG_INF).astype(np.float32), (1, 4))
                         for d in range(2)])
    eye = np.tile(np.eye(C, dtype=np.float32), (1, 4))
    dist = np.tile(np.abs(T - U).astype(np.float32), (1, 4))

    head_of_row = np.arange(4 * C) // C
    head_of_lane = np.arange(GROUP_W) // HEAD_DIM
    parity = np.where(head_of_row < 2, head_of_row, -1)
    hmask = np.concatenate(
        [(head_of_lane % 2)[None, :] == parity[:, None],
         head_of_row[:, None] == (np.arange(GLA_W) // GLA_DK)[None, :]], axis=1).astype(np.float32)
    bd256 = ((np.arange(GROUP_W) // HEAD_DIM)[:, None]
             == (np.arange(GROUP_W) // HEAD_DIM)[None, :]).astype(np.float32)
    bd_gl = ((np.arange(GROUP_W) // HEAD_DIM)[:, None]
             == (np.arange(GLA_W) // GLA_DK)[None, :]).astype(np.float32)

    sel_i = np.zeros((2, 128, GROUP_W), np.float32)
    sel_f = np.zeros((2, 128, GROUP_W), np.float32)
    for d in range(2):
        for h in range(HEADS):
            sel_i[d, d * 8 + h, h * HEAD_DIM:(h + 1) * HEAD_DIM] = 1.0
            sel_f[d, d * 8 + 4 + h, h * HEAD_DIM:(h + 1) * HEAD_DIM] = 1.0
    is_f = np.zeros((1, 128), np.float32)
    is_f[0, 4:8] = 1.0
    is_f[0, 12:16] = 1.0
    return dict(lev=lev, incl_neg=incl_neg, eye=eye, dist=dist,
                hmask=hmask, bd256=bd256, bd_gl=bd_gl,
                sel_i=sel_i, sel_f=sel_f, is_f=is_f)


_NP = _np_consts()


def _full_spec(shape):
    n = len(shape)
    return pl.BlockSpec(tuple(shape), lambda *_: (0,) * n)


def _ada_kernel(c_ref, w_ref, b_ref, o_ref):
    act = _silu(c_ref[...]).astype(BF16)
    o_ref[...] = _dot(act, w_ref[0].astype(BF16)) + b_ref[0]


def _ada(cvecs, w_all, b_all, layer):
    rows = cvecs.shape[0]
    depth, _, n = w_all.shape
    tn = 1024
    return pl.pallas_call(
        _ada_kernel,
        grid=(n // tn,),
        in_specs=[pl.BlockSpec((rows, D_MODEL), lambda i: (0, 0)),
                  pl.BlockSpec((1, D_MODEL, tn), lambda i: (layer, 0, i)),
                  pl.BlockSpec((1, 1, tn), lambda i: (layer, 0, i))],
        out_specs=pl.BlockSpec((rows, tn), lambda i: (0, i)),
        out_shape=jax.ShapeDtypeStruct((rows, n), F32),
        compiler_params=pltpu.CompilerParams(dimension_semantics=("arbitrary",),
                                             vmem_limit_bytes=VMEM_LIMIT),
        name="ada",
    )(cvecs, w_all, b_all.reshape(depth, 1, n))


def _rope(x, cos, sin):
    lane = lax.broadcasted_iota(jnp.int32, x.shape, 1)
    low = (lane % 32) < 16
    partner = jnp.where(low, pltpu.roll(x, GROUP_W - 16, 1), pltpu.roll(x, 16, 1))
    return x * cos + partner * sin


def _token_specs(xs, row_tile, n_ctx_tiles):
    T = ROW_TILE
    if len(xs) == 1:
        return [pl.BlockSpec((1, T, D_MODEL), lambda j: (*row_tile(j), 0))]
    return [pl.BlockSpec((1, T, D_MODEL),
                         lambda j: (row_tile(j)[0], jnp.minimum(row_tile(j)[1], n_ctx_tiles - 1), 0)),
            pl.BlockSpec((1, T, D_MODEL),
                         lambda j: (row_tile(j)[0], jnp.maximum(row_tile(j)[1] - n_ctx_tiles, 0), 0))]


def _load_tokens(x_refs, tile, n_ctx_tiles):
    if len(x_refs) == 1:
        return x_refs[0][0]
    return jnp.where(tile < n_ctx_tiles, x_refs[0][0], x_refs[1][0])


def _inp_tile_chain(p, load_x, mod_ref, cos_ref, sin_ref, w_ref, hg_ref, mlb_ref, isf_ref,
                    wa_hi_ref, wa_lo_ref, ba_ref, ps_ref, pk_ref, pg_ref, gate_ref):
    x = load_x()
    sh = mod_ref[0, 0, :, 0:D_MODEL]
    sc = mod_ref[0, 0, :, D_MODEL:2 * D_MODEL]
    xn = ((x * _rms_scale(x)) * (1.0 + sc) + sh).astype(BF16)

    def proj(a, b):
        return _dot(xn, w_ref[:, a:b])

    G = GROUP_W

    def put(off, width, fn=None):
        def post(res):
            ps_ref[p, :, off:off + width] = (res if fn is None else fn(res)).astype(BF16)
        return post

    def hgrn_decay(d):
        def post(z):
            log_lb = hg_ref[3 * d + 0:3 * d + 1, :]
            log_1mlb = hg_ref[3 * d + 1:3 * d + 2, :]
            one_m_lb = hg_ref[3 * d + 2:3 * d + 3, :]
            z2 = z * LOG2E
            ls = _log2_sigmoid(z2)
            other = log_1mlb + ls
            pg_ref[d, p, :, G_HG_LF:G_HG_LF + G] = (
                jnp.maximum(log_lb, other) + jnp.log2(1.0 + jnp.exp2(_neg_abs(log_lb - other))))
            pk_ref[d, p] = (one_m_lb * jnp.exp2(ls - z2)).astype(BF16)
        return post

    def narrow(small):
        small_hi, small_lo = _split2(small)
        pre2 = (small + mlb_ref[...]) * LOG2E
        gate_logs = jnp.where(isf_ref[...] > 0.5, _log2_sigmoid(pre2), pre2)
        for d in range(2):
            za = (_dot(small_hi, wa_hi_ref[d]) + _dot(small_lo, wa_hi_ref[d])
                  + _dot(small_hi, wa_lo_ref[d]) + ba_ref[d:d + 1, :])
            pg_ref[d, p, :, G_GL_LA:G_GL_LA + GLA_W] = _log2_sigmoid(za * LOG2E) * (1.0 / GLA_TAU)
            pg_ref[d, p, :, G_SMALL:PG_W] = gate_logs

    def gates(half, fn):
        def post(g):
            gate_ref[p, :, half * 2 * G:(half + 1) * 2 * G] = fn(g).astype(BF16)
        return post

    def rope(scale):
        return lambda r: _rope(r * scale if scale != 1.0 else r, cos_ref[...], sin_ref[...])

    o = 10 * G
    stages = [
        (W_GATES, W_GATES + 2 * G, gates(0, _sigmoid)),
        (0, G, put(O_HG_Q, G, _silu)),
        (G, 2 * G, hgrn_decay(0)),
        (3 * G, 5 * G, put(O_HG_V, 2 * G)),
        (2 * G, 3 * G, hgrn_decay(1)),
        (5 * G, 6 * G, put(O_ML_K, G, lambda r: r * (HEAD_DIM ** -0.5))),
        (W_GATES + 2 * G, IN_PAD, gates(1, _silu)),
        (6 * G, 7 * G, put(O_ML_V, G)),
        (7 * G, 8 * G, put(O_RT_Q, G, rope(1.0))),
        (9 * G, o + GLA_W, put(O_RT_V, G + GLA_W)),
        (8 * G, 9 * G, put(O_RT_K, G, rope(HEAD_DIM ** -0.5))),
        (o + GLA_W, o + 2 * GLA_W, put(O_GL_K, GLA_W, lambda r: r * (GLA_DK ** -0.5))),
        (W_SMALL, W_GATES, narrow),
        (o + 2 * GLA_W, W_SMALL, put(O_GL_V, G)),
    ]
    assert len(stages) == INP_STAGES
    pending = None
    for a, b, post in stages:
        res = proj(a, b)
        if pending is not None:
            pending[1](pending[0])
        pending = (res, post)
        yield
    pending[1](pending[0])


def _inp_kernel(n_src, n_ctx_tiles, tiles_per_row, per_step, *refs):
    n_tok = per_step * (n_src + 3)
    tok_refs = refs[:n_tok]
    w_ref, hg_ref, mlb_ref, isf_ref, wa_hi_ref, wa_lo_ref, ba_ref = refs[n_tok:n_tok + 7]
    ps_ref, pk_ref, pg_ref, gate_ref = refs[n_tok + 7:]
    chains = []
    for p in range(per_step):
        x_refs = tok_refs[p * n_src:(p + 1) * n_src]
        mod_ref, cos_ref, sin_ref = (tok_refs[per_step * (n_src + k) + p] for k in range(3))

        def load_x(x_refs=x_refs, p=p):
            return _load_tokens(x_refs, (per_step * pl.program_id(0) + p) % tiles_per_row, n_ctx_tiles)

        chains.append((p * (INP_STAGES - 1),
                       _inp_tile_chain(p, load_x, mod_ref, cos_ref, sin_ref, w_ref, hg_ref, mlb_ref,
                                       isf_ref, wa_hi_ref, wa_lo_ref, ba_ref, ps_ref, pk_ref, pg_ref,
                                       gate_ref)))
    _interleave(chains)


def _inp(xs, mods, w_in_p, cos_t, sin_t, hg_par, ml_bias, wa_hi, wa_lo, ba, n_ctx_tiles):
    B = xs[0].shape[0]
    NT = sum(a.shape[1] for a in xs)
    T = ROW_TILE
    tpr = NT // T
    n_all = B * tpr
    most = INP_TILES_PER_STEP if len(xs) == 1 else 2
    per_step = next(n for n in (most, 2, 1) if n_all % n == 0)
    params = [hg_par, ml_bias, jnp.asarray(_NP["is_f"]), wa_hi, wa_lo, ba]

    def row_tile(j, p):
        g = per_step * j + p
        return g // tpr, g % tpr

    def x_specs(p):
        return _token_specs(xs, functools.partial(row_tile, p=p), n_ctx_tiles)

    def mod_spec(p):
        return pl.BlockSpec((1, 1, 1, 6 * D_MODEL),
                            lambda j: (row_tile(j, p)[0], jnp.where(row_tile(j, p)[1] < n_ctx_tiles, 0, 1), 0, 0))

    def rope_spec(p):
        return pl.BlockSpec((T, GROUP_W), lambda j: (row_tile(j, p)[1], 0))

    slots = range(per_step)
    outs = pl.pallas_call(
        functools.partial(_inp_kernel, len(xs), n_ctx_tiles, tpr, per_step),
        grid=(n_all // per_step,),
        in_specs=[s for p in slots for s in x_specs(p)]
        + [mod_spec(p) for p in slots] + [rope_spec(p) for p in slots] + [rope_spec(p) for p in slots]
        + [pl.BlockSpec((D_MODEL, IN_PAD), lambda j: (0, 0), pipeline_mode=pl.Buffered(1))]
        + [_full_spec(a.shape) for a in params],
        out_specs=[pl.BlockSpec((per_step, T, PS_W), lambda j: (j, 0, 0)),
                   pl.BlockSpec((2, per_step, T, GROUP_W), lambda j: (0, j, 0, 0)),
                   pl.BlockSpec((2, per_step, T, PG_W), lambda j: (0, j, 0, 0)),
                   pl.BlockSpec((per_step, T, D_MODEL), lambda j: (j, 0, 0))],
        out_shape=[jax.ShapeDtypeStruct((n_all, T, PS_W), BF16),
                   jax.ShapeDtypeStruct((2, n_all, T, GROUP_W), BF16),
                   jax.ShapeDtypeStruct((2, n_all, T, PG_W), F32),
                   jax.ShapeDtypeStruct((n_all, T, D_MODEL), BF16)],
        compiler_params=pltpu.CompilerParams(dimension_semantics=("parallel",),
                                             vmem_limit_bytes=VMEM_LIMIT),
        name="inp",
    )(*(list(xs) * per_step), *([mods] * per_step), *([cos_t] * per_step), *([sin_t] * per_step),
      w_in_p, *params)
    ps, pk, pg, gate = outs
    return (ps.reshape(B, NT, PS_W), pk.reshape(2, B, NT, GROUP_W), pg.reshape(2, B, NT, PG_W),
            gate.reshape(B, NT, D_MODEL))


def _head_blocks(a16, hmask_ref):
    C, W = a16.shape
    if W == LANES:
        return jnp.concatenate([a16, a16, a16, a16], axis=0) * hmask_ref[:, GROUP_W:GROUP_W + W]
    even = a16 * hmask_ref[0:C, 0:W]
    odd = a16 * hmask_ref[C:2 * C, 0:W]
    zero = jnp.zeros((C, LANES), a16.dtype)
    return jnp.concatenate([
        jnp.concatenate([even[:, 0:LANES], zero], axis=1),
        jnp.concatenate([odd[:, 0:LANES], zero], axis=1),
        jnp.concatenate([zero, even[:, LANES:W]], axis=1),
        jnp.concatenate([zero, odd[:, LANES:W]], axis=1)], axis=0)


def _last_row(a, d):
    return a[CHUNK - 1:CHUNK, :] if d == 0 else a[0:1, :]


def _level_exponent(d, h, log_a, b, row4):
    C, W = log_a.shape
    if h >= 4:
        b3 = b.reshape(C // (2 * h), 2 * h, W)
        r = h - 1 if d == 0 else h
        ref = jnp.broadcast_to(b3[:, r:r + 1, :], b3.shape).reshape(C, W)
        return _neg_abs(b - ref)
    up = pltpu.roll(log_a, C - 1, 0)
    dn = pltpu.roll(log_a, 1, 0)
    if d == 0:
        return jnp.where(row4 == 0, up, jnp.where(row4 == 1, 0.0,
                                                  log_a + jnp.where(row4 == 3, dn, 0.0)))
    return jnp.where(row4 == 0, log_a + up,
                     jnp.where(row4 == 1, log_a, jnp.where(row4 == 2, 0.0, dn)))


def _interleave(chains):
    live = list(chains)
    rnd = 0
    while live:
        for item in list(live):
            start, g = item
            if rnd >= start:
                try:
                    next(g)
                except StopIteration:
                    live.remove(item)
        rnd += 1


def _update_state(st_ref, d, st, decay_row, bd_state, upd):
    rows = st.shape[0] // HEADS
    per_head = st.shape[1] // HEADS
    for h in range(HEADS):
        j = (h * per_head) // LANES
        rs = slice(h * rows, (h + 1) * rows)
        cs = slice(j * LANES, (j + 1) * LANES)
        st_ref[d, rs, cs] = decay_row[:, cs] * st[rs, cs] + bd_state[rs, cs] * upd[rs, cs]


def _vector_decay_chain(d, q16, k16, v16, log_a, put, o_y, st_ref, lev_ref, hmask_ref, bd_state):
    C, W = log_a.shape
    G = GROUP_W
    b = _cumsum_rows(log_a, d)
    row4 = lax.broadcasted_iota(jnp.int32, (C, W), 0) & 3
    order = sorted(range(len(LEVELS)), key=lambda li: LEVELS[li] >= 4)
    attn = None
    pending = None
    for li in order:
        h = LEVELS[li]
        if h == 1:
            qa = jnp.concatenate([q16 * jnp.exp2(log_a).astype(BF16), q16], axis=0)
            kb = k16
        else:
            f16 = jnp.exp2(_level_exponent(d, h, log_a, b, row4)).astype(BF16)
            qa = q16 * f16
            kb = k16 * f16
        part = _dot_nt(qa, _head_blocks(kb, hmask_ref))
        if pending is not None:
            attn = pending if attn is None else attn + pending
        yield
        part = part.astype(BF16)
        pending = (part * lev_ref[d, li] if h > 1 else
                   part[0:C] * lev_ref[d, li] + part[C:2 * C] * lev_ref[d, len(LEVELS)])
    attn = attn + pending

    st = st_ref[d]
    b_end = _last_row(b, d)
    q_state = q16 * jnp.exp2(b).astype(BF16)
    k_end = k16 * jnp.exp2(b_end - b).astype(BF16)
    o = _dot(attn, _head_blocks(v16, hmask_ref)) + _dot_nt(q_state, st.astype(BF16))
    upd = _dot_tn(v16, k_end)
    yield
    put(o_y, o)
    _update_state(st_ref, d, st, jnp.exp2(b_end), bd_state, upd)


def _mlstm_chain(d, q16, k16, v16, small, put, s_ml, n_ml, m_ml, incl_neg, eye,
                 seli, self_, hmask_ref, bd256, bd256_16):
    C = CHUNK
    G = GROUP_W
    i_bc = _sel_right(small, seli)
    b_col = _sel_right(_cumsum_rows(small, d), self_)
    qk = _dot_nt(q16, _head_blocks(k16, hmask_ref))
    yield
    b_row = jnp.sum(b_col * eye, axis=0, keepdims=True)
    i_row = jnp.sum(i_bc * eye, axis=0, keepdims=True)
    w = b_col - b_row + i_row + incl_neg
    row_max = [jnp.max(w[:, h * HEAD_DIM:(h + 1) * HEAD_DIM], axis=-1, keepdims=True)
               for h in range(HEADS)]
    yield
    m_prev = m_ml[d, 0:1, :]
    inter = b_col + m_prev
    lane_head = lax.broadcasted_iota(jnp.int32, (C, G), 1) // HEAD_DIM
    m_t = None
    for h in range(HEADS):
        mh = jnp.maximum(row_max[h], inter[:, h * HEAD_DIM:h * HEAD_DIM + 1])
        mh = jnp.broadcast_to(mh, (C, G))
        m_t = mh if m_t is None else jnp.where(lane_head == h, mh, m_t)
    g_in = jnp.exp2(inter - m_t)
    s = (qk * jnp.exp2(w - m_t)).astype(BF16)
    c_st = s_ml[d]
    n_row = n_ml[d, 0:1, :]
    qn = (q16.astype(F32) * n_row).astype(BF16)
    num_a = _dot(s, _head_blocks(v16, hmask_ref))
    num_b = _dot(q16, c_st.astype(BF16))
    den_ab = _dot(jnp.concatenate([s, qn], axis=0), bd256_16)
    m_new = _last_row(m_t, d)
    b_end = _last_row(b_col, d)
    a_state = jnp.exp2(b_end + m_prev - m_new)
    kw = k16.astype(F32) * jnp.exp2(b_end - b_col + i_bc - m_new)
    upd = _dot_tn(kw.astype(BF16), v16)
    yield
    num = num_a + g_in * num_b
    den = den_ab[0:C] + g_in * den_ab[C:2 * C]
    put(G, num / jnp.maximum(jnp.abs(den), jnp.exp2(-m_t)))
    _update_state(s_ml, d, c_st, a_state, bd256, upd)
    n_ml[d] = jnp.broadcast_to(a_state * n_row + jnp.sum(kw, axis=0, keepdims=True), (8, G))
    m_ml[d] = jnp.broadcast_to(m_new, (8, G))


def _retention_chain(d, q16, k16, v16, put, s_rt, log_g, dist, incl_neg, hmask_ref, bd256):
    C = CHUNK
    G = GROUP_W
    qk = _dot_nt(q16, _head_blocks(k16, hmask_ref))
    decay = jnp.exp2(log_g * dist + incl_neg)
    yield
    rowc = lax.broadcasted_iota(jnp.int32, (C, 1), 0).astype(F32)
    cnt = (rowc + 1.0) if d == 0 else (C - rowc)
    q_state = q16 * jnp.exp2(log_g * cnt).astype(BF16)
    k_end = k16 * jnp.exp2(log_g * (C - cnt)).astype(BF16)
    st = s_rt[d]
    o = _dot((qk * decay).astype(BF16), _head_blocks(v16, hmask_ref)) + _dot(q_state, st.astype(BF16))
    upd = _dot_tn(k_end, v16)
    yield
    put(2 * G, o)
    _update_state(s_rt, d, st, jnp.exp2(log_g * float(C)), bd256, upd)


def _mix_kernel(psf_ref, psb_ref, pkf_ref, pkb_ref, pgf_ref, pgb_ref, rtg_ref,
                lev_ref, incl_ref, eye_ref, dist_ref,
                hmask_ref, bd256_ref, bdgl_ref, seli_ref, self_ref,
                yf_ref, yb_ref,
                s_hg, s_rt, s_gl, s_ml, n_ml, m_ml):
    C = CHUNK
    G = GROUP_W

    @pl.when(pl.program_id(1) == 0)
    def _():
        s_hg[...] = jnp.zeros_like(s_hg)
        s_rt[...] = jnp.zeros_like(s_rt)
        s_gl[...] = jnp.zeros_like(s_gl)
        s_ml[...] = jnp.zeros_like(s_ml)
        n_ml[...] = jnp.zeros_like(n_ml)
        m_ml[...] = jnp.zeros_like(m_ml)

    bd256 = bd256_ref[...]
    bd256_16 = bd256.astype(BF16)

    def deferred(make):
        yield from make()

    chains = []
    for i in range(STEP_CHUNKS):
        for d, (ps_ref, pk_ref, pg_ref, y_ref) in enumerate(((psf_ref, pkf_ref, pgf_ref, yf_ref),
                                                             (psb_ref, pkb_ref, pgb_ref, yb_ref))):
            sub = i if d == 0 else STEP_CHUNKS - 1 - i
            rows = slice(sub * C, (sub + 1) * C)

            def ps(off, width, ps_ref=ps_ref, rows=rows):
                return ps_ref[0, rows, off:off + width]

            def pg(off, width, pg_ref=pg_ref, rows=rows):
                return pg_ref[0, 0, rows, off:off + width]

            def put(off, val, y_ref=y_ref, rows=rows):
                y_ref[0, rows, off:off + G] = val

            def hgrn(d=d, ps=ps, pg=pg, put=put, pk_ref=pk_ref, rows=rows):
                return _vector_decay_chain(d, ps(O_HG_Q, G), pk_ref[0, 0, rows, :], ps(O_HG_V, G),
                                           pg(G_HG_LF, G), put, 0, s_hg, lev_ref, hmask_ref, bd256)

            def mlstm(d=d, ps=ps, pg=pg, put=put):
                return _mlstm_chain(d, ps(O_ML_Q, G), ps(O_ML_K, G), ps(O_ML_V, G),
                                    pg(G_SMALL, PG_W - G_SMALL), put, s_ml, n_ml, m_ml, incl_ref[d],
                                    eye_ref[...], seli_ref[d], self_ref[d], hmask_ref, bd256, bd256_16)

            def retention(d=d, ps=ps, put=put):
                return _retention_chain(d, ps(O_RT_Q, G), ps(O_RT_K, G), ps(O_RT_V, G), put, s_rt,
                                        rtg_ref[d:d + 1, :], dist_ref[...], incl_ref[d], hmask_ref, bd256)

            def gla(d=d, ps=ps, pg=pg, put=put):
                return _vector_decay_chain(d, ps(O_GL_Q, GLA_W), ps(O_GL_K, GLA_W), ps(O_GL_V, G),
                                           pg(G_GL_LA, GLA_W), put, 3 * G, s_gl, lev_ref, hmask_ref,
                                           bdgl_ref[...])

            chains += [(i * CHUNK_STAGGER, deferred(make)) for make in (hgrn, mlstm, retention, gla)]
    _interleave(chains)


def _mix_call(B, NT, n_ctx_chunks):
    C = CHUNK * STEP_CHUNKS
    assert NT % C == 0 and n_ctx_chunks % STEP_CHUNKS == 0
    NC = NT // C
    n_ctx_blocks = n_ctx_chunks // STEP_CHUNKS
    G = GROUP_W

    def bwd_chunk(j):
        return jnp.where(j < n_ctx_blocks, n_ctx_blocks - 1 - j, NC - 1 - (j - n_ctx_blocks))

    def fwd_map(b, j):
        return (b, j, 0)

    def bwd_map(b, j):
        return (b, bwd_chunk(j), 0)

    consts = [jnp.asarray(_NP["lev"], BF16), jnp.asarray(_NP["incl_neg"]),
              jnp.asarray(_NP["eye"]), jnp.asarray(_NP["dist"]),
              jnp.asarray(_NP["hmask"], BF16),
              jnp.asarray(_NP["bd256"]), jnp.asarray(_NP["bd_gl"]),
              jnp.asarray(_NP["sel_i"], BF16), jnp.asarray(_NP["sel_f"], BF16)]
    call = pl.pallas_call(
        _mix_kernel,
        grid=(B, NC),
        in_specs=[pl.BlockSpec((1, C, PS_W), fwd_map), pl.BlockSpec((1, C, PS_W), bwd_map),
                  pl.BlockSpec((1, 1, C, G), lambda b, j: (0, b, j, 0)),
                  pl.BlockSpec((1, 1, C, G), lambda b, j: (1, b, bwd_chunk(j), 0)),
                  pl.BlockSpec((1, 1, C, PG_W), lambda b, j: (0, b, j, 0)),
                  pl.BlockSpec((1, 1, C, PG_W), lambda b, j: (1, b, bwd_chunk(j), 0)),
                  _full_spec((8, G))] + [_full_spec(a.shape) for a in consts],
        out_specs=[pl.BlockSpec((1, C, D_MODEL), fwd_map), pl.BlockSpec((1, C, D_MODEL), bwd_map)],
        out_shape=[jax.ShapeDtypeStruct((B, NT, D_MODEL), F32)] * 2,
        scratch_shapes=[pltpu.VMEM((2, G, G), F32), pltpu.VMEM((2, G, G), F32),
                        pltpu.VMEM((2, G, GLA_W), F32), pltpu.VMEM((2, G, G), F32),
                        pltpu.VMEM((2, 8, G), F32), pltpu.VMEM((2, 8, G), F32)],
        compiler_params=pltpu.CompilerParams(dimension_semantics=("parallel", "arbitrary"),
                                             vmem_limit_bytes=VMEM_LIMIT),
        name="mix",
    )
    return lambda ps, pk, pg, rt_par: call(ps, ps, pk, pk, pg, pg, rt_par, *consts)


def _out_tile_chain(final, load_x, yf_ref, yb_ref, gate_ref, mod_ref, gh_ref, bd_ref,
                    wo_ref, w1_ref, w2_ref, gf_ref, store):
    D = D_MODEL
    bd = bd_ref[...]
    for g in range(D // GROUP_W):
        sl = slice(g * GROUP_W, (g + 1) * GROUP_W)
        y = yf_ref[0, :, sl] + yb_ref[0, :, sl]
        msq = _dot((y * y).astype(BF16), bd) * (1.0 / HEAD_DIM)
        yn = y * lax.rsqrt(msq + RMS_EPS) * gh_ref[:, sl] * gate_ref[0, :, sl].astype(F32)
        part = _dot(yn.astype(BF16), wo_ref[sl, :])
        mixed = part if g == 0 else mixed + part
    yield
    g1 = mod_ref[0, 0, :, 2 * D:3 * D]
    sh2 = mod_ref[0, 0, :, 3 * D:4 * D]
    sc2 = mod_ref[0, 0, :, 4 * D:5 * D]
    g2 = mod_ref[0, 0, :, 5 * D:6 * D]
    x1 = load_x() + g1 * mixed
    hin = ((x1 * _rms_scale(x1)) * (1.0 + sc2) + sh2).astype(BF16)
    hid = _dot(hin, w1_ref[...])
    yield
    hid = jnp.maximum(hid, 0.0)
    ff = _dot((hid * hid).astype(BF16), w2_ref[...])
    yield
    x2 = x1 + g2 * ff
    if final:
        x2 = (x2 * _rms_scale(x2)) * gf_ref[...]
    store(x2)


def _out_kernel(final, n_src, n_ctx_tiles, t0, tiles_per_row, per_step, *refs):
    n_tok = per_step * (n_src + 4)
    tok_refs = refs[:n_tok]
    gh_ref, bd_ref, wo_ref, w1_ref, w2_ref, gf_ref, o_ref = refs[n_tok:]
    chains = []
    for p in range(per_step):
        x_refs = tok_refs[p * n_src:(p + 1) * n_src]
        yf_ref, yb_ref, gate_ref, mod_ref = (tok_refs[per_step * (n_src + k) + p] for k in range(4))

        def load_x(x_refs=x_refs, p=p):
            return _load_tokens(x_refs, (per_step * pl.program_id(0) + p) % tiles_per_row + t0,
                                n_ctx_tiles)

        def store(val, p=p):
            o_ref[p] = val

        chains.append((p, _out_tile_chain(final, load_x, yf_ref, yb_ref, gate_ref, mod_ref, gh_ref,
                                          bd_ref, wo_ref, w1_ref, w2_ref, gf_ref, store)))
    _interleave(chains)


def _out(xs, yf, yb, gate, mods, g_heads, w_out, w_ff1, w_ff2, g_final, n_ctx_tiles, final):
    B, NT, D = yf.shape
    T = ROW_TILE
    t0 = n_ctx_tiles if final else 0
    tpr = NT // T - t0
    n_all = B * tpr
    per_step = 2 if n_all % 2 == 0 else 1
    bd = jnp.asarray(_NP["bd256"], BF16)
    wspec = lambda shape: pl.BlockSpec(shape, lambda j: (0, 0), pipeline_mode=pl.Buffered(1))

    def row_tile(j, p):
        g = per_step * j + p
        return g // tpr, g % tpr + t0

    def x_specs(p):
        return _token_specs(xs, functools.partial(row_tile, p=p), n_ctx_tiles)

    def tok(p):
        return pl.BlockSpec((1, T, D), lambda j: (*row_tile(j, p), 0))

    def mod_spec(p):
        return pl.BlockSpec((1, 1, 1, 6 * D),
                            lambda j: (row_tile(j, p)[0], jnp.where(row_tile(j, p)[1] < n_ctx_tiles, 0, 1), 0, 0))

    slots = range(per_step)
    out = pl.pallas_call(
        functools.partial(_out_kernel, final, len(xs), n_ctx_tiles, t0, tpr, per_step),
        grid=(n_all // per_step,),
        in_specs=[s for p in slots for s in x_specs(p)]
        + [tok(p) for _ in range(3) for p in slots] + [mod_spec(p) for p in slots]
        + [pl.BlockSpec((1, D), lambda j: (0, 0)),
           pl.BlockSpec((GROUP_W, GROUP_W), lambda j: (0, 0)),
           wspec((D, D)), wspec((D, D_FF)), wspec((D_FF, D)),
           pl.BlockSpec((1, D), lambda j: (0, 0))],
        out_specs=pl.BlockSpec((per_step, T, D), lambda j: (j, 0, 0)),
        out_shape=jax.ShapeDtypeStruct((n_all, T, D), F32),
        compiler_params=pltpu.CompilerParams(dimension_semantics=("parallel",),
                                             vmem_limit_bytes=VMEM_LIMIT),
        name="out_final" if final else "out",
    )(*(list(xs) * per_step), *([yf] * per_step), *([yb] * per_step), *([gate] * per_step),
      *([mods] * per_step), g_heads.reshape(1, D), bd, w_out, w_ff1, w_ff2, g_final.reshape(1, D))
    return out.reshape(B, tpr * T, D)


_REF_LAYOUT = (('hg_q', 256), ('hg_f_fwd', 256), ('hg_f_bwd', 256), ('hg_i', 256), ('hg_g', 256),
               ('ml_q', 256), ('ml_k', 256), ('ml_v', 256), ('ml_if', 16), ('ml_o', 256),
               ('rt_q', 256), ('rt_k', 256), ('rt_v', 256), ('rt_g', 256),
               ('gl_q', 128), ('gl_k', 128), ('gl_v', 256), ('gl_a_fwd', 16), ('gl_a_bwd', 16),
               ('gl_g', 256))
_NEW_ORDER = ('hg_q', 'hg_f_fwd', 'hg_f_bwd', 'hg_i', 'ml_q', 'ml_k', 'ml_v', 'rt_q', 'rt_k', 'rt_v',
              'gl_q', 'gl_k', 'gl_v', 'ml_if', 'gl_a_fwd', 'gl_a_bwd', 'PAD', 'hg_g', 'ml_o', 'rt_g', 'gl_g')


def _reorder_columns(w):
    starts, off = {}, 0
    for name, width in _REF_LAYOUT:
        starts[name] = (off, width)
        off += width
    parts = []
    for name in _NEW_ORDER:
        if name == 'PAD':
            parts.append(jnp.zeros((w.shape[0], 128 - 48), BF16))
        else:
            a, width = starts[name]
            parts.append(w[:, a:a + width].astype(BF16))
    out = jnp.concatenate(parts, axis=1)
    assert out.shape[1] == IN_PAD
    return out


def _rope_tables(n_ctx, n_lat):
    rows = n_lat // GRID_W
    n_freq = HEAD_DIM // 4
    inv = np.float32(ROPE_BASE) ** (-np.arange(n_freq, dtype=np.float32) / n_freq)
    ar = np.arange(rows, dtype=np.float32)[:, None] * inv[None, :]
    ac = np.arange(GRID_W, dtype=np.float32)[:, None] * inv[None, :]

    def expand(per_row, per_col, ctx_value):
        lat = jnp.concatenate([jnp.repeat(jnp.asarray(per_row), GRID_W, axis=0),
                               jnp.tile(jnp.asarray(per_col), (rows, 1))], axis=-1)
        full = jnp.concatenate([jnp.full((n_ctx, HEAD_DIM), ctx_value, F32), lat], axis=0)
        return jnp.tile(full, (1, HEADS))

    cos = expand(np.concatenate([np.cos(ar), np.cos(ar)], -1), np.concatenate([np.cos(ac), np.cos(ac)], -1), 1.0)
    sin = expand(np.concatenate([-np.sin(ar), np.sin(ar)], -1), np.concatenate([-np.sin(ac), np.sin(ac)], -1), 0.0)
    return cos, sin


def kernel(x, c, ctx, c_ctx, w_ada, b_ada, w_in, g_heads, hgrn_lb_logits, ml_gate_bias,
           rt_decay_logit, gla_w_a, gla_b_a, w_out, w_ff1, w_ff2, g_final):
    B, L, D = x.shape
    Lc = ctx.shape[1]
    depth = w_in.shape[0]
    assert D == D_MODEL and L % ROW_TILE == 0 and Lc % ROW_TILE == 0 and L % GRID_W == 0
    n_ctx_tiles = Lc // ROW_TILE
    n_ctx_chunks = Lc // CHUNK

    xs = (ctx, x)
    cos_t, sin_t = _rope_tables(Lc, L)
    mix = _mix_call(B, Lc + L, n_ctx_chunks)

    n_c = -(-(B + 1) // 8) * 8
    cvecs = jnp.zeros((n_c, D), F32).at[:B].set(c).at[B].set(c_ctx)

    sm = jax.nn.softmax(hgrn_lb_logits.astype(F32), axis=0)
    lb_all = jnp.maximum(jnp.cumsum(sm, axis=0) - sm[:1], 0.0)

    out = None
    for layer in range(depth):
        final = layer == depth - 1
        mod = _ada(cvecs, w_ada, b_ada, layer)
        mods = jnp.stack([jnp.broadcast_to(mod[B], (B, 6 * D)), mod[:B]], axis=1)[:, :, None, :]

        w_in_p = _reorder_columns(w_in[layer])
        lb = lb_all[layer]
        hg_par = jnp.stack([jnp.log(lb[0]) * LOG2E, jnp.log1p(-lb[0]) * LOG2E, 1.0 - lb[0],
                            jnp.log(lb[1]) * LOG2E, jnp.log1p(-lb[1]) * LOG2E, 1.0 - lb[1],
                            jnp.zeros_like(lb[0]), jnp.zeros_like(lb[0])], axis=0)
        ml_bias = jnp.zeros((1, 128), F32).at[0, :16].set(ml_gate_bias[layer].astype(F32).reshape(16))
        wa = jnp.zeros((2, 128, GLA_W), F32)
        wa = wa.at[0, 16:32].set(gla_w_a[layer, 0].astype(F32)).at[1, 32:48].set(gla_w_a[layer, 1].astype(F32))
        wa_hi = wa.astype(BF16)
        wa_lo = (wa - wa_hi.astype(F32)).astype(BF16)
        ba = jnp.zeros((8, GLA_W), F32).at[:2].set(gla_b_a[layer].astype(F32))
        ps, pk, pg, gate = _inp(xs, mods, w_in_p, cos_t, sin_t, hg_par, ml_bias, wa_hi, wa_lo, ba,
                                n_ctx_tiles)

        rt_par = jnp.zeros((8, GROUP_W), F32).at[:2].set(
            jnp.repeat(jax.nn.log_sigmoid(rt_decay_logit[layer].astype(F32)) * LOG2E, HEAD_DIM, axis=-1))
        yf, yb = mix(ps, pk, pg, rt_par)

        res = _out(xs, yf, yb, gate, mods, g_heads[layer], w_out[layer].astype(BF16),
                   w_ff1[layer].astype(BF16), w_ff2[layer].astype(BF16), g_final,
                   n_ctx_tiles, final)
        if final:
            out = res
        else:
            xs = (res,)
    return out
```

```python
import functools

import numpy as np
import jax
import jax.numpy as jnp
from jax import lax
from jax.experimental import pallas as pl
from jax.experimental.pallas import tpu as pltpu

F32 = jnp.float32
BF16 = jnp.bfloat16

D_MODEL = 1024
GROUP_W = 256
HEADS = 4
HEAD_DIM = 64
GLA_DK = 32
GLA_W = HEADS * GLA_DK
GLA_TAU = 16.0
D_FF = 4 * D_MODEL
GRID_W = 64
ROPE_BASE = 10000.0
RMS_EPS = 1e-6

LANES = 128
SUBLANES = 8

CHUNK = 64
STEP_CHUNKS = 4
CHUNK_STAGGER = 3
LEVELS = (32, 16, 8, 4, 2, 1)
ROW_TILE = 256
INP_STAGES = 14
INP_TILES_PER_STEP = 4

W_SMALL = 3072
W_GATES = 3200
IN_PAD = W_GATES + D_MODEL

O_HG_Q, O_HG_V = 0, 256
O_ML_Q, O_ML_K, O_ML_V = 512, 768, 1024
O_RT_Q, O_RT_K, O_RT_V = 1280, 1536, 1792
O_GL_Q, O_GL_K, O_GL_V = 2048, 2176, 2304
PS_W = 2560
G_HG_LF, G_GL_LA, G_SMALL = 0, 256, 384
PG_W = 512

VMEM_LIMIT = 56 * 1024 * 1024

NEG_INF = float("-inf")
LOG2E = 1.4426950408889634


def _dot(a, b):
    return jnp.dot(a, b, preferred_element_type=F32)


def _dot_nt(a, b):
    return lax.dot_general(a, b, (((1,), (1,)), ((), ())), preferred_element_type=F32)


def _dot_tn(a, b):
    return lax.dot_general(a, b, (((0,), (0,)), ((), ())), preferred_element_type=F32)


def _split2(a):
    hi = a.astype(BF16)
    return hi, (a - hi.astype(F32)).astype(BF16)


def _cumsum_rows(x, d):
    n, width = x.shape
    tiles = n // SUBLANES
    x3 = x.reshape(tiles, SUBLANES, width)
    row = lax.broadcasted_iota(jnp.int32, (1, SUBLANES, width), 1)
    s = 1
    while s < SUBLANES:
        if d == 0:
            x3 = x3 + jnp.where(row >= s, pltpu.roll(x3, s, 1), 0.0)
        else:
            x3 = x3 + jnp.where(row < SUBLANES - s, pltpu.roll(x3, SUBLANES - s, 1), 0.0)
        s *= 2
    order = range(tiles) if d == 0 else range(tiles - 1, -1, -1)
    edge = SUBLANES - 1 if d == 0 else 0
    out = [None] * tiles
    carry = None
    for j in order:
        out[j] = x3[j] if carry is None else x3[j] + carry
        carry = out[j][edge:edge + 1]
    return jnp.concatenate(out, axis=0)


def _sel_right(a, m):
    hi, lo = _split2(a)
    return _dot(hi, m) + _dot(lo, m)


def _neg_abs(x):
    return pltpu.bitcast(pltpu.bitcast(x, jnp.uint32) | jnp.uint32(0x80000000), F32)


def _log2_sigmoid(z2):
    return jnp.minimum(z2, 0.0) - jnp.log2(1.0 + jnp.exp2(_neg_abs(z2)))


def _sigmoid(z):
    return 1.0 / (1.0 + jnp.exp(-z))


def _silu(z):
    return z * _sigmoid(z)


def _rms_scale(x):
    return lax.rsqrt(jnp.mean(x * x, axis=-1, keepdims=True) + RMS_EPS)


def _np_consts():
    C = CHUNK
    t = np.arange(C)
    T, U = np.meshgrid(t, t, indexing="ij")
    incl = [U <= T, U >= T]

    lev = np.zeros((2, len(LEVELS) + 1, C, 4 * C), np.float32)
    lev[:, len(LEVELS)] = np.tile(np.eye(C, dtype=np.float32), (1, 4))
    for d in range(2):
        for li, h in enumerate(LEVELS):
            same = (T // (2 * h)) == (U // (2 * h))
            if d == 0:
                pair = same & (T % (2 * h) >= h) & (U % (2 * h) < h)
            else:
                pair = same & (T % (2 * h) < h) & (U % (2 * h) >= h)
            lev[d, li] = np.tile(pair.astype(np.float32), (1, 4))

    incl_neg = np.stack([np.tile(np.where(incl[d], 0.0, NEG_INF).astype(np.float32), (1, 4))
                         for d in range(2)])
    eye = np.tile(np.eye(C, dtype=np.float32), (1, 4))
    dist = np.tile(np.abs(T - U).astype(np.float32), (1, 4))

    head_of_row = np.arange(4 * C) // C
    head_of_lane = np.arange(GROUP_W) // HEAD_DIM
    parity = np.where(head_of_row < 2, head_of_row, -1)
    hmask = np.concatenate(
        [(head_of_lane % 2)[None, :] == parity[:, None],
         head_of_row[:, None] == (np.arange(GLA_W) // GLA_DK)[None, :]], axis=1).astype(np.float32)
    bd256 = ((np.arange(GROUP_W) // HEAD_DIM)[:, None]
             == (np.arange(GROUP_W) // HEAD_DIM)[None, :]).astype(np.float32)
    bd_gl = ((np.arange(GROUP_W) // HEAD_DIM)[:, None]
             == (np.arange(GLA_W) // GLA_DK)[None, :]).astype(np.float32)

    sel_i = np.zeros((2, 128, GROUP_W), np.float32)
    sel_f = np.zeros((2, 128, GROUP_W), np.float32)
    for d in range(2):
        for h in range(HEADS):
            sel_i[d, d * 8 + h, h * HEAD_DIM:(h + 1) * HEAD_DIM] = 1.0
            sel_f[d, d * 8 + 4 + h, h * HEAD_DIM:(h + 1) * HEAD_DIM] = 1.0
    is_f = np.zeros((1, 128), np.float32)
    is_f[0, 4:8] = 1.0
    is_f[0, 12:16] = 1.0
    return dict(lev=lev, incl_neg=incl_neg, eye=eye, dist=dist,
                hmask=hmask, bd256=bd256, bd_gl=bd_gl,
                sel_i=sel_i, sel_f=sel_f, is_f=is_f)


_NP = _np_consts()


def _full_spec(shape):
    n = len(shape)
    return pl.BlockSpec(tuple(shape), lambda *_: (0,) * n)


def _ada_kernel(c_ref, w_ref, b_ref, o_ref):
    act = _silu(c_ref[...]).astype(BF16)
    o_ref[...] = _dot(act, w_ref[0].astype(BF16)) + b_ref[0]


def _ada(cvecs, w_all, b_all, layer):
    rows = cvecs.shape[0]
    depth, _, n = w_all.shape
    tn = 1024
    return pl.pallas_call(
        _ada_kernel,
        grid=(n // tn,),
        in_specs=[pl.BlockSpec((rows, D_MODEL), lambda i: (0, 0)),
                  pl.BlockSpec((1, D_MODEL, tn), lambda i: (layer, 0, i)),
                  pl.BlockSpec((1, 1, tn), lambda i: (layer, 0, i))],
        out_specs=pl.BlockSpec((rows, tn), lambda i: (0, i)),
        out_shape=jax.ShapeDtypeStruct((rows, n), F32),
        compiler_params=pltpu.CompilerParams(dimension_semantics=("arbitrary",),
                                             vmem_limit_bytes=VMEM_LIMIT),
        name="ada",
    )(cvecs, w_all, b_all.reshape(depth, 1, n))


def _rope(x, cos, sin):
    lane = lax.broadcasted_iota(jnp.int32, x.shape, 1)
    low = (lane % 32) < 16
    partner = jnp.where(low, pltpu.roll(x, GROUP_W - 16, 1), pltpu.roll(x, 16, 1))
    return x * cos + partner * sin


def _token_specs(xs, row_tile, n_ctx_tiles):
    T = ROW_TILE
    if len(xs) == 1:
        return [pl.BlockSpec((1, T, D_MODEL), lambda j: (*row_tile(j), 0))]
    return [pl.BlockSpec((1, T, D_MODEL),
                         lambda j: (row_tile(j)[0], jnp.minimum(row_tile(j)[1], n_ctx_tiles - 1), 0)),
            pl.BlockSpec((1, T, D_MODEL),
                         lambda j: (row_tile(j)[0], jnp.maximum(row_tile(j)[1] - n_ctx_tiles, 0), 0))]


def _load_tokens(x_refs, tile, n_ctx_tiles):
    if len(x_refs) == 1:
        return x_refs[0][0]
    return jnp.where(tile < n_ctx_tiles, x_refs[0][0], x_refs[1][0])


def _inp_tile_chain(p, load_x, mod_ref, cos_ref, sin_ref, w_ref, hg_ref, mlb_ref, isf_ref,
                    wa_ref, ba_ref, ps_ref, pk_ref, pg_ref, gate_ref):
    x = load_x()
    sh = mod_ref[0, 0, :, 0:D_MODEL]
    sc = mod_ref[0, 0, :, D_MODEL:2 * D_MODEL]
    xn = ((x * _rms_scale(x)) * (1.0 + sc) + sh).astype(BF16)

    def proj(a, b):
        return _dot(xn, w_ref[:, a:b])

    G = GROUP_W

    def put(off, width, fn=None):
        def post(res):
            ps_ref[p, :, off:off + width] = (res if fn is None else fn(res)).astype(BF16)
        return post

    def hgrn_decay(d):
        def post(z):
            log_lb = hg_ref[3 * d + 0:3 * d + 1, :]
            log_1mlb = hg_ref[3 * d + 1:3 * d + 2, :]
            one_m_lb = hg_ref[3 * d + 2:3 * d + 3, :]
            z2 = z * LOG2E
            ls = _log2_sigmoid(z2)
            other = log_1mlb + ls
            pg_ref[d, p, :, G_HG_LF:G_HG_LF + G] = (
                jnp.maximum(log_lb, other) + jnp.log2(1.0 + jnp.exp2(_neg_abs(log_lb - other))))
            pk_ref[d, p] = (one_m_lb * jnp.exp2(ls - z2)).astype(BF16)
        return post

    def narrow(small):
        pre2 = (small + mlb_ref[...]) * LOG2E
        gate_logs = jnp.where(isf_ref[...] > 0.5, _log2_sigmoid(pre2), pre2)
        za = _dot(small.astype(BF16), wa_ref[...]) + ba_ref[0:1, :]
        log_a = _log2_sigmoid(za * LOG2E) * (1.0 / GLA_TAU)
        for d in range(2):
            pg_ref[d, p, :, G_GL_LA:G_GL_LA + GLA_W] = log_a[:, d * GLA_W:(d + 1) * GLA_W]
            pg_ref[d, p, :, G_SMALL:PG_W] = gate_logs

    def gates(half, fn):
        def post(g):
            gate_ref[p, :, half * 2 * G:(half + 1) * 2 * G] = fn(g).astype(BF16)
        return post

    def rope(scale):
        return lambda r: _rope(r * scale if scale != 1.0 else r, cos_ref[...], sin_ref[...])

    o = 10 * G
    stages = [
        (W_GATES, W_GATES + 2 * G, gates(0, _sigmoid)),
        (0, G, put(O_HG_Q, G, _silu)),
        (G, 2 * G, hgrn_decay(0)),
        (3 * G, 5 * G, put(O_HG_V, 2 * G)),
        (2 * G, 3 * G, hgrn_decay(1)),
        (5 * G, 6 * G, put(O_ML_K, G, lambda r: r * (HEAD_DIM ** -0.5))),
        (W_GATES + 2 * G, IN_PAD, gates(1, _silu)),
        (6 * G, 7 * G, put(O_ML_V, G)),
        (7 * G, 8 * G, put(O_RT_Q, G, rope(1.0))),
        (9 * G, o + GLA_W, put(O_RT_V, G + GLA_W)),
        (8 * G, 9 * G, put(O_RT_K, G, rope(HEAD_DIM ** -0.5))),
        (o + GLA_W, o + 2 * GLA_W, put(O_GL_K, GLA_W, lambda r: r * (GLA_DK ** -0.5))),
        (W_SMALL, W_GATES, narrow),
        (o + 2 * GLA_W, W_SMALL, put(O_GL_V, G)),
    ]
    assert len(stages) == INP_STAGES
    pending = None
    for a, b, post in stages:
        res = proj(a, b)
        if pending is not None:
            pending[1](pending[0])
        pending = (res, post)
        yield
    pending[1](pending[0])


def _inp_kernel(n_src, n_ctx_tiles, tiles_per_row, per_step, *refs):
    n_tok = per_step * (n_src + 3)
    tok_refs = refs[:n_tok]
    w_ref, hg_ref, mlb_ref, isf_ref, wa_ref, ba_ref = refs[n_tok:n_tok + 6]
    ps_ref, pk_ref, pg_ref, gate_ref = refs[n_tok + 6:]
    chains = []
    for p in range(per_step):
        x_refs = tok_refs[p * n_src:(p + 1) * n_src]
        mod_ref, cos_ref, sin_ref = (tok_refs[per_step * (n_src + k) + p] for k in range(3))

        def load_x(x_refs=x_refs, p=p):
            return _load_tokens(x_refs, (per_step * pl.program_id(0) + p) % tiles_per_row, n_ctx_tiles)

        chains.append((p * (INP_STAGES - 1),
                       _inp_tile_chain(p, load_x, mod_ref, cos_ref, sin_ref, w_ref, hg_ref, mlb_ref,
                                       isf_ref, wa_ref, ba_ref, ps_ref, pk_ref, pg_ref,
                                       gate_ref)))
    _interleave(chains)


def _inp(xs, mods, w_in_p, cos_t, sin_t, hg_par, ml_bias, wa, ba, n_ctx_tiles):
    B = xs[0].shape[0]
    NT = sum(a.shape[1] for a in xs)
    T = ROW_TILE
    tpr = NT // T
    n_all = B * tpr
    most = INP_TILES_PER_STEP if len(xs) == 1 else 2
    per_step = next(n for n in (most, 2, 1) if n_all % n == 0)
    params = [hg_par, ml_bias, jnp.asarray(_NP["is_f"]), wa, ba]

    def row_tile(j, p):
        g = per_step * j + p
        return g // tpr, g % tpr

    def x_specs(p):
        return _token_specs(xs, functools.partial(row_tile, p=p), n_ctx_tiles)

    def mod_spec(p):
        return pl.BlockSpec((1, 1, 1, 6 * D_MODEL),
                            lambda j: (row_tile(j, p)[0], jnp.where(row_tile(j, p)[1] < n_ctx_tiles, 0, 1), 0, 0))

    def rope_spec(p):
        return pl.BlockSpec((T, GROUP_W), lambda j: (row_tile(j, p)[1], 0))

    slots = range(per_step)
    outs = pl.pallas_call(
        functools.partial(_inp_kernel, len(xs), n_ctx_tiles, tpr, per_step),
        grid=(n_all // per_step,),
        in_specs=[s for p in slots for s in x_specs(p)]
        + [mod_spec(p) for p in slots] + [rope_spec(p) for p in slots] + [rope_spec(p) for p in slots]
        + [pl.BlockSpec((D_MODEL, IN_PAD), lambda j: (0, 0), pipeline_mode=pl.Buffered(1))]
        + [_full_spec(a.shape) for a in params],
        out_specs=[pl.BlockSpec((per_step, T, PS_W), lambda j: (j, 0, 0)),
                   pl.BlockSpec((2, per_step, T, GROUP_W), lambda j: (0, j, 0, 0)),
                   pl.BlockSpec((2, per_step, T, PG_W), lambda j: (0, j, 0, 0)),
                   pl.BlockSpec((per_step, T, D_MODEL), lambda j: (j, 0, 0))],
        out_shape=[jax.ShapeDtypeStruct((n_all, T, PS_W), BF16),
                   jax.ShapeDtypeStruct((2, n_all, T, GROUP_W), BF16),
                   jax.ShapeDtypeStruct((2, n_all, T, PG_W), F32),
                   jax.ShapeDtypeStruct((n_all, T, D_MODEL), BF16)],
        compiler_params=pltpu.CompilerParams(dimension_semantics=("parallel",),
                                             vmem_limit_bytes=VMEM_LIMIT),
        name="inp",
    )(*(list(xs) * per_step), *([mods] * per_step), *([cos_t] * per_step), *([sin_t] * per_step),
      w_in_p, *params)
    ps, pk, pg, gate = outs
    return (ps.reshape(B, NT, PS_W), pk.reshape(2, B, NT, GROUP_W), pg.reshape(2, B, NT, PG_W),
            gate.reshape(B, NT, D_MODEL))


def _head_blocks(a16, hmask_ref):
    C, W = a16.shape
    if W == LANES:
        return jnp.concatenate([a16, a16, a16, a16], axis=0) * hmask_ref[:, GROUP_W:GROUP_W + W]
    even = a16 * hmask_ref[0:C, 0:W]
    odd = a16 * hmask_ref[C:2 * C, 0:W]
    zero = jnp.zeros((C, LANES), a16.dtype)
    return jnp.concatenate([
        jnp.concatenate([even[:, 0:LANES], zero], axis=1),
        jnp.concatenate([odd[:, 0:LANES], zero], axis=1),
        jnp.concatenate([zero, even[:, LANES:W]], axis=1),
        jnp.concatenate([zero, odd[:, LANES:W]], axis=1)], axis=0)


def _last_row(a, d):
    return a[CHUNK - 1:CHUNK, :] if d == 0 else a[0:1, :]


def _level_exponent(d, h, log_a, b, row4):
    C, W = log_a.shape
    if h >= 4:
        b3 = b.reshape(C // (2 * h), 2 * h, W)
        r = h - 1 if d == 0 else h
        ref = jnp.broadcast_to(b3[:, r:r + 1, :], b3.shape).reshape(C, W)
        return _neg_abs(b - ref)
    up = pltpu.roll(log_a, C - 1, 0)
    dn = pltpu.roll(log_a, 1, 0)
    if d == 0:
        return jnp.where(row4 == 0, up, jnp.where(row4 == 1, 0.0,
                                                  log_a + jnp.where(row4 == 3, dn, 0.0)))
    return jnp.where(row4 == 0, log_a + up,
                     jnp.where(row4 == 1, log_a, jnp.where(row4 == 2, 0.0, dn)))


def _interleave(chains):
    live = list(chains)
    rnd = 0
    while live:
        for item in list(live):
            start, g = item
            if rnd >= start:
                try:
                    next(g)
                except StopIteration:
                    live.remove(item)
        rnd += 1


def _update_state(st_ref, d, st, decay_row, bd_state, upd):
    rows = st.shape[0] // HEADS
    per_head = st.shape[1] // HEADS
    for h in range(HEADS):
        j = (h * per_head) // LANES
        rs = slice(h * rows, (h + 1) * rows)
        cs = slice(j * LANES, (j + 1) * LANES)
        st_ref[d, rs, cs] = decay_row[:, cs] * st[rs, cs] + bd_state[rs, cs] * upd[rs, cs]


def _vector_decay_chain(d, q16, k16, v16, log_a, put, o_y, st_ref, lev_ref, hmask_ref, bd_state):
    C, W = log_a.shape
    G = GROUP_W
    b = _cumsum_rows(log_a, d)
    row4 = lax.broadcasted_iota(jnp.int32, (C, W), 0) & 3
    order = sorted(range(len(LEVELS)), key=lambda li: LEVELS[li] >= 4)
    attn = None
    pending = None
    for li in order:
        h = LEVELS[li]
        if h == 1:
            qa = jnp.concatenate([q16 * jnp.exp2(log_a).astype(BF16), q16], axis=0)
            kb = k16
        else:
            f16 = jnp.exp2(_level_exponent(d, h, log_a, b, row4)).astype(BF16)
            qa = q16 * f16
            kb = k16 * f16
        part = _dot_nt(qa, _head_blocks(kb, hmask_ref))
        if pending is not None:
            attn = pending if attn is None else attn + pending
        yield
        part = part.astype(BF16)
        pending = (part * lev_ref[d, li] if h > 1 else
                   part[0:C] * lev_ref[d, li] + part[C:2 * C] * lev_ref[d, len(LEVELS)])
    attn = attn + pending

    st = st_ref[d]
    b_end = _last_row(b, d)
    q_state = q16 * jnp.exp2(b).astype(BF16)
    k_end = k16 * jnp.exp2(b_end - b).astype(BF16)
    o = _dot(attn, _head_blocks(v16, hmask_ref)) + _dot_nt(q_state, st.astype(BF16))
    upd = _dot_tn(v16, k_end)
    yield
    put(o_y, o)
    _update_state(st_ref, d, st, jnp.exp2(b_end), bd_state, upd)


def _mlstm_chain(d, q16, k16, v16, small, put, s_ml, n_ml, m_ml, incl_neg, eye,
                 seli, self_, hmask_ref, bd256, bd256_16):
    C = CHUNK
    G = GROUP_W
    i_bc = _sel_right(small, seli)
    b_col = _sel_right(_cumsum_rows(small, d), self_)
    qk = _dot_nt(q16, _head_blocks(k16, hmask_ref))
    yield
    b_row = jnp.sum(b_col * eye, axis=0, keepdims=True)
    i_row = jnp.sum(i_bc * eye, axis=0, keepdims=True)
    w = b_col - b_row + i_row + incl_neg
    row_max = [jnp.max(w[:, h * HEAD_DIM:(h + 1) * HEAD_DIM], axis=-1, keepdims=True)
               for h in range(HEADS)]
    yield
    m_prev = m_ml[d, 0:1, :]
    inter = b_col + m_prev
    lane_head = lax.broadcasted_iota(jnp.int32, (C, G), 1) // HEAD_DIM
    m_t = None
    for h in range(HEADS):
        mh = jnp.maximum(row_max[h], inter[:, h * HEAD_DIM:h * HEAD_DIM + 1])
        mh = jnp.broadcast_to(mh, (C, G))
        m_t = mh if m_t is None else jnp.where(lane_head == h, mh, m_t)
    g_in = jnp.exp2(inter - m_t)
    s = (qk * jnp.exp2(w - m_t)).astype(BF16)
    c_st = s_ml[d]
    n_row = n_ml[d, 0:1, :]
    qn = (q16.astype(F32) * n_row).astype(BF16)
    num_a = _dot(s, _head_blocks(v16, hmask_ref))
    num_b = _dot(q16, c_st.astype(BF16))
    den_ab = _dot(jnp.concatenate([s, qn], axis=0), bd256_16)
    m_new = _last_row(m_t, d)
    b_end = _last_row(b_col, d)
    a_state = jnp.exp2(b_end + m_prev - m_new)
    kw = k16.astype(F32) * jnp.exp2(b_end - b_col + i_bc - m_new)
    upd = _dot_tn(kw.astype(BF16), v16)
    yield
    num = num_a + g_in * num_b
    den = den_ab[0:C] + g_in * den_ab[C:2 * C]
    put(G, num / jnp.maximum(jnp.abs(den), jnp.exp2(-m_t)))
    _update_state(s_ml, d, c_st, a_state, bd256, upd)
    n_ml[d] = jnp.broadcast_to(a_state * n_row + jnp.sum(kw, axis=0, keepdims=True), (8, G))
    m_ml[d] = jnp.broadcast_to(m_new, (8, G))


def _retention_chain(d, q16, k16, v16, put, s_rt, log_g, dist, incl_neg, hmask_ref, bd256):
    C = CHUNK
    G = GROUP_W
    qk = _dot_nt(q16, _head_blocks(k16, hmask_ref))
    decay = jnp.exp2(log_g * dist + incl_neg)
    yield
    rowc = lax.broadcasted_iota(jnp.int32, (C, 1), 0).astype(F32)
    cnt = (rowc + 1.0) if d == 0 else (C - rowc)
    q_state = q16 * jnp.exp2(log_g * cnt).astype(BF16)
    k_end = k16 * jnp.exp2(log_g * (C - cnt)).astype(BF16)
    st = s_rt[d]
    o = _dot((qk * decay).astype(BF16), _head_blocks(v16, hmask_ref)) + _dot(q_state, st.astype(BF16))
    upd = _dot_tn(k_end, v16)
    yield
    put(2 * G, o)
    _update_state(s_rt, d, st, jnp.exp2(log_g * float(C)), bd256, upd)


def _mix_kernel(psf_ref, psb_ref, pkf_ref, pkb_ref, pgf_ref, pgb_ref, rtg_ref,
                lev_ref, incl_ref, eye_ref, dist_ref,
                hmask_ref, bd256_ref, bdgl_ref, seli_ref, self_ref,
                yf_ref, yb_ref,
                s_hg, s_rt, s_gl, s_ml, n_ml, m_ml):
    C = CHUNK
    G = GROUP_W

    @pl.when(pl.program_id(1) == 0)
    def _():
        s_hg[...] = jnp.zeros_like(s_hg)
        s_rt[...] = jnp.zeros_like(s_rt)
        s_gl[...] = jnp.zeros_like(s_gl)
        s_ml[...] = jnp.zeros_like(s_ml)
        n_ml[...] = jnp.zeros_like(n_ml)
        m_ml[...] = jnp.zeros_like(m_ml)

    bd256 = bd256_ref[...]
    bd256_16 = bd256.astype(BF16)

    def deferred(make):
        yield from make()

    chains = []
    for i in range(STEP_CHUNKS):
        for d, (ps_ref, pk_ref, pg_ref, y_ref) in enumerate(((psf_ref, pkf_ref, pgf_ref, yf_ref),
                                                             (psb_ref, pkb_ref, pgb_ref, yb_ref))):
            sub = i if d == 0 else STEP_CHUNKS - 1 - i
            rows = slice(sub * C, (sub + 1) * C)

            def ps(off, width, ps_ref=ps_ref, rows=rows):
                return ps_ref[0, rows, off:off + width]

            def pg(off, width, pg_ref=pg_ref, rows=rows):
                return pg_ref[0, 0, rows, off:off + width]

            def put(off, val, y_ref=y_ref, rows=rows):
                y_ref[0, rows, off:off + G] = val

            def hgrn(d=d, ps=ps, pg=pg, put=put, pk_ref=pk_ref, rows=rows):
                return _vector_decay_chain(d, ps(O_HG_Q, G), pk_ref[0, 0, rows, :], ps(O_HG_V, G),
                                           pg(G_HG_LF, G), put, 0, s_hg, lev_ref, hmask_ref, bd256)

            def mlstm(d=d, ps=ps, pg=pg, put=put):
                return _mlstm_chain(d, ps(O_ML_Q, G), ps(O_ML_K, G), ps(O_ML_V, G),
                                    pg(G_SMALL, PG_W - G_SMALL), put, s_ml, n_ml, m_ml, incl_ref[d],
                                    eye_ref[...], seli_ref[d], self_ref[d], hmask_ref, bd256, bd256_16)

            def retention(d=d, ps=ps, put=put):
                return _retention_chain(d, ps(O_RT_Q, G), ps(O_RT_K, G), ps(O_RT_V, G), put, s_rt,
                                        rtg_ref[d:d + 1, :], dist_ref[...], incl_ref[d], hmask_ref, bd256)

            def gla(d=d, ps=ps, pg=pg, put=put):
                return _vector_decay_chain(d, ps(O_GL_Q, GLA_W), ps(O_GL_K, GLA_W), ps(O_GL_V, G),
                                           pg(G_GL_LA, GLA_W), put, 3 * G, s_gl, lev_ref, hmask_ref,
                                           bdgl_ref[...])

            chains += [(i * CHUNK_STAGGER, deferred(make)) for make in (hgrn, mlstm, retention, gla)]
    _interleave(chains)


def _mix_call(B, NT, n_ctx_chunks):
    C = CHUNK * STEP_CHUNKS
    assert NT % C == 0 and n_ctx_chunks % STEP_CHUNKS == 0
    NC = NT // C
    n_ctx_blocks = n_ctx_chunks // STEP_CHUNKS
    G = GROUP_W

    def bwd_chunk(j):
        return jnp.where(j < n_ctx_blocks, n_ctx_blocks - 1 - j, NC - 1 - (j - n_ctx_blocks))

    def fwd_map(b, j):
        return (b, j, 0)

    def bwd_map(b, j):
        return (b, bwd_chunk(j), 0)

    consts = [jnp.asarray(_NP["lev"], BF16), jnp.asarray(_NP["incl_neg"]),
              jnp.asarray(_NP["eye"]), jnp.asarray(_NP["dist"]),
              jnp.asarray(_NP["hmask"], BF16),
              jnp.asarray(_NP["bd256"]), jnp.asarray(_NP["bd_gl"]),
              jnp.asarray(_NP["sel_i"], BF16), jnp.asarray(_NP["sel_f"], BF16)]
    call = pl.pallas_call(
        _mix_kernel,
        grid=(B, NC),
        in_specs=[pl.BlockSpec((1, C, PS_W), fwd_map), pl.BlockSpec((1, C, PS_W), bwd_map),
                  pl.BlockSpec((1, 1, C, G), lambda b, j: (0, b, j, 0)),
                  pl.BlockSpec((1, 1, C, G), lambda b, j: (1, b, bwd_chunk(j), 0)),
                  pl.BlockSpec((1, 1, C, PG_W), lambda b, j: (0, b, j, 0)),
                  pl.BlockSpec((1, 1, C, PG_W), lambda b, j: (1, b, bwd_chunk(j), 0)),
                  _full_spec((8, G))] + [_full_spec(a.shape) for a in consts],
        out_specs=[pl.BlockSpec((1, C, D_MODEL), fwd_map), pl.BlockSpec((1, C, D_MODEL), bwd_map)],
        out_shape=[jax.ShapeDtypeStruct((B, NT, D_MODEL), F32)] * 2,
        scratch_shapes=[pltpu.VMEM((2, G, G), F32), pltpu.VMEM((2, G, G), F32),
                        pltpu.VMEM((2, G, GLA_W), F32), pltpu.VMEM((2, G, G), F32),
                        pltpu.VMEM((2, 8, G), F32), pltpu.VMEM((2, 8, G), F32)],
        compiler_params=pltpu.CompilerParams(dimension_semantics=("parallel", "arbitrary"),
                                             vmem_limit_bytes=VMEM_LIMIT),
        name="mix",
    )
    return lambda ps, pk, pg, rt_par: call(ps, ps, pk, pk, pg, pg, rt_par, *consts)


def _out_tile_chain(final, load_x, yf_ref, yb_ref, gate_ref, mod_ref, gh_ref, bd_ref,
                    wo_ref, w1_ref, w2_ref, gf_ref, store):
    D = D_MODEL
    bd = bd_ref[...]
    for g in range(D // GROUP_W):
        sl = slice(g * GROUP_W, (g + 1) * GROUP_W)
        y = yf_ref[0, :, sl] + yb_ref[0, :, sl]
        msq = _dot((y * y).astype(BF16), bd) * (1.0 / HEAD_DIM)
        yn = y * lax.rsqrt(msq + RMS_EPS) * gh_ref[:, sl] * gate_ref[0, :, sl].astype(F32)
        part = _dot(yn.astype(BF16), wo_ref[sl, :])
        mixed = part if g == 0 else mixed + part
    yield
    g1 = mod_ref[0, 0, :, 2 * D:3 * D]
    sh2 = mod_ref[0, 0, :, 3 * D:4 * D]
    sc2 = mod_ref[0, 0, :, 4 * D:5 * D]
    g2 = mod_ref[0, 0, :, 5 * D:6 * D]
    x1 = load_x() + g1 * mixed
    hin = ((x1 * _rms_scale(x1)) * (1.0 + sc2) + sh2).astype(BF16)
    hid = _dot(hin, w1_ref[...])
    yield
    hid = jnp.maximum(hid, 0.0)
    ff = _dot((hid * hid).astype(BF16), w2_ref[...])
    yield
    x2 = x1 + g2 * ff
    if final:
        x2 = (x2 * _rms_scale(x2)) * gf_ref[...]
    store(x2)


def _out_kernel(final, n_src, n_ctx_tiles, t0, tiles_per_row, per_step, *refs):
    n_tok = per_step * (n_src + 4)
    tok_refs = refs[:n_tok]
    gh_ref, bd_ref, wo_ref, w1_ref, w2_ref, gf_ref, o_ref = refs[n_tok:]
    chains = []
    for p in range(per_step):
        x_refs = tok_refs[p * n_src:(p + 1) * n_src]
        yf_ref, yb_ref, gate_ref, mod_ref = (tok_refs[per_step * (n_src + k) + p] for k in range(4))

        def load_x(x_refs=x_refs, p=p):
            return _load_tokens(x_refs, (per_step * pl.program_id(0) + p) % tiles_per_row + t0,
                                n_ctx_tiles)

        def store(val, p=p):
            o_ref[p] = val

        chains.append((p, _out_tile_chain(final, load_x, yf_ref, yb_ref, gate_ref, mod_ref, gh_ref,
                                          bd_ref, wo_ref, w1_ref, w2_ref, gf_ref, store)))
    _interleave(chains)


def _out(xs, yf, yb, gate, mods, g_heads, w_out, w_ff1, w_ff2, g_final, n_ctx_tiles, final):
    B, NT, D = yf.shape
    T = ROW_TILE
    t0 = n_ctx_tiles if final else 0
    tpr = NT // T - t0
    n_all = B * tpr
    per_step = 2 if n_all % 2 == 0 else 1
    bd = jnp.asarray(_NP["bd256"], BF16)
    wspec = lambda shape: pl.BlockSpec(shape, lambda j: (0, 0), pipeline_mode=pl.Buffered(1))

    def row_tile(j, p):
        g = per_step * j + p
        return g // tpr, g % tpr + t0

    def x_specs(p):
        return _token_specs(xs, functools.partial(row_tile, p=p), n_ctx_tiles)

    def tok(p):
        return pl.BlockSpec((1, T, D), lambda j: (*row_tile(j, p), 0))

    def mod_spec(p):
        return pl.BlockSpec((1, 1, 1, 6 * D),
                            lambda j: (row_tile(j, p)[0], jnp.where(row_tile(j, p)[1] < n_ctx_tiles, 0, 1), 0, 0))

    slots = range(per_step)
    out = pl.pallas_call(
        functools.partial(_out_kernel, final, len(xs), n_ctx_tiles, t0, tpr, per_step),
        grid=(n_all // per_step,),
        in_specs=[s for p in slots for s in x_specs(p)]
        + [tok(p) for _ in range(3) for p in slots] + [mod_spec(p) for p in slots]
        + [pl.BlockSpec((1, D), lambda j: (0, 0)),
           pl.BlockSpec((GROUP_W, GROUP_W), lambda j: (0, 0)),
           wspec((D, D)), wspec((D, D_FF)), wspec((D_FF, D)),
           pl.BlockSpec((1, D), lambda j: (0, 0))],
        out_specs=pl.BlockSpec((per_step, T, D), lambda j: (j, 0, 0)),
        out_shape=jax.ShapeDtypeStruct((n_all, T, D), F32),
        compiler_params=pltpu.CompilerParams(dimension_semantics=("parallel",),
                                             vmem_limit_bytes=VMEM_LIMIT),
        name="out_final" if final else "out",
    )(*(list(xs) * per_step), *([yf] * per_step), *([yb] * per_step), *([gate] * per_step),
      *([mods] * per_step), g_heads.reshape(1, D), bd, w_out, w_ff1, w_ff2, g_final.reshape(1, D))
    return out.reshape(B, tpr * T, D)


_REF_LAYOUT = (('hg_q', 256), ('hg_f_fwd', 256), ('hg_f_bwd', 256), ('hg_i', 256), ('hg_g', 256),
               ('ml_q', 256), ('ml_k', 256), ('ml_v', 256), ('ml_if', 16), ('ml_o', 256),
               ('rt_q', 256), ('rt_k', 256), ('rt_v', 256), ('rt_g', 256),
               ('gl_q', 128), ('gl_k', 128), ('gl_v', 256), ('gl_a_fwd', 16), ('gl_a_bwd', 16),
               ('gl_g', 256))
_NEW_ORDER = ('hg_q', 'hg_f_fwd', 'hg_f_bwd', 'hg_i', 'ml_q', 'ml_k', 'ml_v', 'rt_q', 'rt_k', 'rt_v',
              'gl_q', 'gl_k', 'gl_v', 'ml_if', 'gl_a_fwd', 'gl_a_bwd', 'PAD', 'hg_g', 'ml_o', 'rt_g', 'gl_g')


def _reorder_columns(w):
    starts, off = {}, 0
    for name, width in _REF_LAYOUT:
        starts[name] = (off, width)
        off += width
    parts = []
    for name in _NEW_ORDER:
        if name == 'PAD':
            parts.append(jnp.zeros((w.shape[0], 128 - 48), BF16))
        else:
            a, width = starts[name]
            parts.append(w[:, a:a + width].astype(BF16))
    out = jnp.concatenate(parts, axis=1)
    assert out.shape[1] == IN_PAD
    return out


def _rope_tables(n_ctx, n_lat):
    rows = n_lat // GRID_W
    n_freq = HEAD_DIM // 4
    inv = np.float32(ROPE_BASE) ** (-np.arange(n_freq, dtype=np.float32) / n_freq)
    ar = np.arange(rows, dtype=np.float32)[:, None] * inv[None, :]
    ac = np.arange(GRID_W, dtype=np.float32)[:, None] * inv[None, :]

    def expand(per_row, per_col, ctx_value):
        lat = jnp.concatenate([jnp.repeat(jnp.asarray(per_row), GRID_W, axis=0),
                               jnp.tile(jnp.asarray(per_col), (rows, 1))], axis=-1)
        full = jnp.concatenate([jnp.full((n_ctx, HEAD_DIM), ctx_value, F32), lat], axis=0)
        return jnp.tile(full, (1, HEADS))

    cos = expand(np.concatenate([np.cos(ar), np.cos(ar)], -1), np.concatenate([np.cos(ac), np.cos(ac)], -1), 1.0)
    sin = expand(np.concatenate([-np.sin(ar), np.sin(ar)], -1), np.concatenate([-np.sin(ac), np.sin(ac)], -1), 0.0)
    return cos, sin


def kernel(x, c, ctx, c_ctx, w_ada, b_ada, w_in, g_heads, hgrn_lb_logits, ml_gate_bias,
           rt_decay_logit, gla_w_a, gla_b_a, w_out, w_ff1, w_ff2, g_final):
    B, L, D = x.shape
    Lc = ctx.shape[1]
    depth = w_in.shape[0]
    assert D == D_MODEL and L % ROW_TILE == 0 and Lc % ROW_TILE == 0 and L % GRID_W == 0
    n_ctx_tiles = Lc // ROW_TILE
    n_ctx_chunks = Lc // CHUNK

    xs = (ctx, x)
    cos_t, sin_t = _rope_tables(Lc, L)
    mix = _mix_call(B, Lc + L, n_ctx_chunks)

    n_c = -(-(B + 1) // 8) * 8
    cvecs = jnp.zeros((n_c, D), F32).at[:B].set(c).at[B].set(c_ctx)

    sm = jax.nn.softmax(hgrn_lb_logits.astype(F32), axis=0)
    lb_all = jnp.maximum(jnp.cumsum(sm, axis=0) - sm[:1], 0.0)

    out = None
    for layer in range(depth):
        final = layer == depth - 1
        mod = _ada(cvecs, w_ada, b_ada, layer)
        mods = jnp.stack([jnp.broadcast_to(mod[B], (B, 6 * D)), mod[:B]], axis=1)[:, :, None, :]

        w_in_p = _reorder_columns(w_in[layer])
        lb = lb_all[layer]
        hg_par = jnp.stack([jnp.log(lb[0]) * LOG2E, jnp.log1p(-lb[0]) * LOG2E, 1.0 - lb[0],
                            jnp.log(lb[1]) * LOG2E, jnp.log1p(-lb[1]) * LOG2E, 1.0 - lb[1],
                            jnp.zeros_like(lb[0]), jnp.zeros_like(lb[0])], axis=0)
        ml_bias = jnp.zeros((1, 128), F32).at[0, :16].set(ml_gate_bias[layer].astype(F32).reshape(16))
        wa = jnp.zeros((128, 2 * GLA_W), F32)
        wa = wa.at[16:32, :GLA_W].set(gla_w_a[layer, 0].astype(F32))
        wa = wa.at[32:48, GLA_W:].set(gla_w_a[layer, 1].astype(F32)).astype(BF16)
        ba = jnp.zeros((8, 2 * GLA_W), F32).at[0].set(gla_b_a[layer].astype(F32).reshape(2 * GLA_W))
        ps, pk, pg, gate = _inp(xs, mods, w_in_p, cos_t, sin_t, hg_par, ml_bias, wa, ba, n_ctx_tiles)

        rt_par = jnp.zeros((8, GROUP_W), F32).at[:2].set(
            jnp.repeat(jax.nn.log_sigmoid(rt_decay_logit[layer].astype(F32)) * LOG2E, HEAD_DIM, axis=-1))
        yf, yb = mix(ps, pk, pg, rt_par)

        res = _out(xs, yf, yb, gate, mods, g_heads[layer], w_out[layer].astype(BF16),
                   w_ff1[layer].astype(BF16), w_ff2[layer].astype(BF16), g_final,
                   n_ctx_tiles, final)
        if final:
            out = res
        else:
            xs = (res,)
    return out
```

```python
import functools

import numpy as np
import jax
import jax.numpy as jnp
from jax import lax
from jax.experimental import pallas as pl
from jax.experimental.pallas import tpu as pltpu

F32 = jnp.float32
BF16 = jnp.bfloat16

D_MODEL = 1024
GROUP_W = 256
HEADS = 4
HEAD_DIM = 64
GLA_DK = 32
GLA_W = HEADS * GLA_DK
GLA_TAU = 16.0
D_FF = 4 * D_MODEL
GRID_W = 64
ROPE_BASE = 10000.0
RMS_EPS = 1e-6

LANES = 128
SUBLANES = 8

CHUNK = 64
STEP_CHUNKS = 4
CHUNK_STAGGER = 3
LEVELS = (32, 16, 8, 4, 2, 1)
ROW_TILE = 256
INP_STAGES = 14
INP_TILES_PER_STEP = 4

W_SMALL = 3072
W_GATES = 3200
IN_PAD = W_GATES + D_MODEL

O_HG_Q, O_HG_V = 0, 256
O_ML_Q, O_ML_K, O_ML_V = 512, 768, 1024
O_RT_Q, O_RT_K, O_RT_V = 1280, 1536, 1792
O_GL_Q, O_GL_K, O_GL_V = 2048, 2176, 2304
PS_W = 2560
G_HG_LF, G_GL_LA, G_SMALL = 0, 256, 384
PG_W = 512

VMEM_LIMIT = 58 * 1024 * 1024

NEG_INF = float("-inf")
LOG2E = 1.4426950408889634


def _dot(a, b):
    return jnp.dot(a, b, preferred_element_type=F32)


def _dot_nt(a, b):
    return lax.dot_general(a, b, (((1,), (1,)), ((), ())), preferred_element_type=F32)


def _dot_tn(a, b):
    return lax.dot_general(a, b, (((0,), (0,)), ((), ())), preferred_element_type=F32)


def _split2(a):
    hi = a.astype(BF16)
    return hi, (a - hi.astype(F32)).astype(BF16)


def _cumsum_rows(x, d):
    n, width = x.shape
    tiles = n // SUBLANES
    x3 = x.reshape(tiles, SUBLANES, width)
    row = lax.broadcasted_iota(jnp.int32, (1, SUBLANES, width), 1)
    s = 1
    while s < SUBLANES:
        if d == 0:
            x3 = x3 + jnp.where(row >= s, pltpu.roll(x3, s, 1), 0.0)
        else:
            x3 = x3 + jnp.where(row < SUBLANES - s, pltpu.roll(x3, SUBLANES - s, 1), 0.0)
        s *= 2
    order = range(tiles) if d == 0 else range(tiles - 1, -1, -1)
    edge = SUBLANES - 1 if d == 0 else 0
    out = [None] * tiles
    carry = None
    for j in order:
        out[j] = x3[j] if carry is None else x3[j] + carry
        carry = out[j][edge:edge + 1]
    return jnp.concatenate(out, axis=0)


def _sel_right(a, m):
    hi, lo = _split2(a)
    return _dot(hi, m) + _dot(lo, m)


def _neg_abs(x):
    return pltpu.bitcast(pltpu.bitcast(x, jnp.uint32) | jnp.uint32(0x80000000), F32)


def _log2_sigmoid(z2):
    return jnp.minimum(z2, 0.0) - jnp.log2(1.0 + jnp.exp2(_neg_abs(z2)))


def _sigmoid(z):
    return 1.0 / (1.0 + jnp.exp(-z))


def _silu(z):
    return z * _sigmoid(z)


def _rms_scale(x):
    return lax.rsqrt(jnp.mean(x * x, axis=-1, keepdims=True) + RMS_EPS)


def _np_consts():
    C = CHUNK
    t = np.arange(C)
    T, U = np.meshgrid(t, t, indexing="ij")
    incl = [U <= T, U >= T]

    lev = np.zeros((2, len(LEVELS) + 1, C, 4 * C), np.float32)
    lev[:, len(LEVELS)] = np.tile(np.eye(C, dtype=np.float32), (1, 4))
    for d in range(2):
        for li, h in enumerate(LEVELS):
            same = (T // (2 * h)) == (U // (2 * h))
            if d == 0:
                pair = same & (T % (2 * h) >= h) & (U % (2 * h) < h)
            else:
                pair = same & (T % (2 * h) < h) & (U % (2 * h) >= h)
            lev[d, li] = np.tile(pair.astype(np.float32), (1, 4))

    incl_neg = np.stack([np.tile(np.where(incl[d], 0.0, NEG_INF).astype(np.float32), (1, 4))
                         for d in range(2)])
    eye = np.tile(np.eye(C, dtype=np.float32), (1, 4))
    dist = np.tile(np.abs(T - U).astype(np.float32), (1, 4))

    head_of_row = np.arange(4 * C) // C
    head_of_lane = np.arange(GROUP_W) // HEAD_DIM
    parity = np.where(head_of_row < 2, head_of_row, -1)
    hmask = np.concatenate(
        [(head_of_lane % 2)[None, :] == parity[:, None],
         head_of_row[:, None] == (np.arange(GLA_W) // GLA_DK)[None, :]], axis=1).astype(np.float32)
    bd256 = ((np.arange(GROUP_W) // HEAD_DIM)[:, None]
             == (np.arange(GROUP_W) // HEAD_DIM)[None, :]).astype(np.float32)
    bd_gl = ((np.arange(GROUP_W) // HEAD_DIM)[:, None]
             == (np.arange(GLA_W) // GLA_DK)[None, :]).astype(np.float32)

    sel_i = np.zeros((2, 128, GROUP_W), np.float32)
    sel_f = np.zeros((2, 128, GROUP_W), np.float32)
    for d in range(2):
        for h in range(HEADS):
            sel_i[d, d * 8 + h, h * HEAD_DIM:(h + 1) * HEAD_DIM] = 1.0
            sel_f[d, d * 8 + 4 + h, h * HEAD_DIM:(h + 1) * HEAD_DIM] = 1.0
    is_f = np.zeros((1, 128), np.float32)
    is_f[0, 4:8] = 1.0
    is_f[0, 12:16] = 1.0
    return dict(lev=lev, incl_neg=incl_neg, eye=eye, dist=dist,
                hmask=hmask, bd256=bd256, bd_gl=bd_gl,
                sel_i=sel_i, sel_f=sel_f, is_f=is_f)


_NP = _np_consts()


def _full_spec(shape):
    n = len(shape)
    return pl.BlockSpec(tuple(shape), lambda *_: (0,) * n)


def _ada_kernel(c_ref, w_ref, b_ref, o_ref):
    act = _silu(c_ref[...]).astype(BF16)
    o_ref[...] = _dot(act, w_ref[0].astype(BF16)) + b_ref[0]


def _ada(cvecs, w_all, b_all, layer):
    rows = cvecs.shape[0]
    depth, _, n = w_all.shape
    tn = 1024
    return pl.pallas_call(
        _ada_kernel,
        grid=(n // tn,),
        in_specs=[pl.BlockSpec((rows, D_MODEL), lambda i: (0, 0)),
                  pl.BlockSpec((1, D_MODEL, tn), lambda i: (layer, 0, i)),
                  pl.BlockSpec((1, 1, tn), lambda i: (layer, 0, i))],
        out_specs=pl.BlockSpec((rows, tn), lambda i: (0, i)),
        out_shape=jax.ShapeDtypeStruct((rows, n), F32),
        compiler_params=pltpu.CompilerParams(dimension_semantics=("arbitrary",),
                                             vmem_limit_bytes=VMEM_LIMIT),
        name="ada",
    )(cvecs, w_all, b_all.reshape(depth, 1, n))


def _rope(x, cos, sin):
    lane = lax.broadcasted_iota(jnp.int32, x.shape, 1)
    low = (lane % 32) < 16
    partner = jnp.where(low, pltpu.roll(x, GROUP_W - 16, 1), pltpu.roll(x, 16, 1))
    return x * cos + partner * sin


def _token_specs(xs, row_tile, n_ctx_tiles):
    T = ROW_TILE
    if len(xs) == 1:
        return [pl.BlockSpec((1, T, D_MODEL), lambda j: (*row_tile(j), 0))]
    return [pl.BlockSpec((1, T, D_MODEL),
                         lambda j: (row_tile(j)[0], jnp.minimum(row_tile(j)[1], n_ctx_tiles - 1), 0)),
            pl.BlockSpec((1, T, D_MODEL),
                         lambda j: (row_tile(j)[0], jnp.maximum(row_tile(j)[1] - n_ctx_tiles, 0), 0))]


def _load_tokens(x_refs, tile, n_ctx_tiles):
    if len(x_refs) == 1:
        return x_refs[0][0]
    return jnp.where(tile < n_ctx_tiles, x_refs[0][0], x_refs[1][0])


def _inp_tile_chain(p, load_x, mod_ref, cos_ref, sin_ref, w_ref, hg_ref, mlb_ref, isf_ref,
                    wa_ref, ba_ref, ps_ref, pk_ref, pg_ref, gate_ref):
    x = load_x()
    sh = mod_ref[0, 0, :, 0:D_MODEL]
    sc = mod_ref[0, 0, :, D_MODEL:2 * D_MODEL]
    xn = ((x * _rms_scale(x)) * (1.0 + sc) + sh).astype(BF16)

    def proj(a, b):
        return _dot(xn, w_ref[:, a:b])

    G = GROUP_W

    def put(off, width, fn=None):
        def post(res):
            ps_ref[p, :, off:off + width] = (res if fn is None else fn(res)).astype(BF16)
        return post

    def hgrn_decay(d):
        def post(z):
            log_lb = hg_ref[3 * d + 0:3 * d + 1, :]
            log_1mlb = hg_ref[3 * d + 1:3 * d + 2, :]
            one_m_lb = hg_ref[3 * d + 2:3 * d + 3, :]
            z2 = z * LOG2E
            ls = _log2_sigmoid(z2)
            other = log_1mlb + ls
            pg_ref[d, p, :, G_HG_LF:G_HG_LF + G] = (
                jnp.maximum(log_lb, other) + jnp.log2(1.0 + jnp.exp2(_neg_abs(log_lb - other))))
            pk_ref[d, p] = (one_m_lb * jnp.exp2(ls - z2)).astype(BF16)
        return post

    def narrow(small):
        pre2 = (small + mlb_ref[...]) * LOG2E
        gate_logs = jnp.where(isf_ref[...] > 0.5, _log2_sigmoid(pre2), pre2)
        za = _dot(small.astype(BF16), wa_ref[...]) + ba_ref[0:1, :]
        log_a = _log2_sigmoid(za * LOG2E) * (1.0 / GLA_TAU)
        for d in range(2):
            pg_ref[d, p, :, G_GL_LA:G_GL_LA + GLA_W] = log_a[:, d * GLA_W:(d + 1) * GLA_W]
            pg_ref[d, p, :, G_SMALL:PG_W] = gate_logs

    def gates(half, fn):
        def post(g):
            gate_ref[p, :, half * 2 * G:(half + 1) * 2 * G] = fn(g).astype(BF16)
        return post

    def rope(scale):
        return lambda r: _rope(r * scale if scale != 1.0 else r, cos_ref[...], sin_ref[...])

    o = 10 * G
    stages = [
        (W_GATES, W_GATES + 2 * G, gates(0, _sigmoid)),
        (0, G, put(O_HG_Q, G, _silu)),
        (G, 2 * G, hgrn_decay(0)),
        (3 * G, 5 * G, put(O_HG_V, 2 * G)),
        (2 * G, 3 * G, hgrn_decay(1)),
        (5 * G, 6 * G, put(O_ML_K, G, lambda r: r * (HEAD_DIM ** -0.5))),
        (W_GATES + 2 * G, IN_PAD, gates(1, _silu)),
        (6 * G, 7 * G, put(O_ML_V, G)),
        (7 * G, 8 * G, put(O_RT_Q, G, rope(1.0))),
        (9 * G, o + GLA_W, put(O_RT_V, G + GLA_W)),
        (8 * G, 9 * G, put(O_RT_K, G, rope(HEAD_DIM ** -0.5))),
        (o + GLA_W, o + 2 * GLA_W, put(O_GL_K, GLA_W, lambda r: r * (GLA_DK ** -0.5))),
        (W_SMALL, W_GATES, narrow),
        (o + 2 * GLA_W, W_SMALL, put(O_GL_V, G)),
    ]
    assert len(stages) == INP_STAGES
    pending = None
    for a, b, post in stages:
        res = proj(a, b)
        if pending is not None:
            pending[1](pending[0])
        pending = (res, post)
        yield
    pending[1](pending[0])


def _inp_kernel(n_src, n_ctx_tiles, tiles_per_row, per_step, *refs):
    n_tok = per_step * (n_src + 3)
    tok_refs = refs[:n_tok]
    w_ref, hg_ref, mlb_ref, isf_ref, wa_ref, ba_ref = refs[n_tok:n_tok + 6]
    ps_ref, pk_ref, pg_ref, gate_ref = refs[n_tok + 6:]
    chains = []
    for p in range(per_step):
        x_refs = tok_refs[p * n_src:(p + 1) * n_src]
        mod_ref, cos_ref, sin_ref = (tok_refs[per_step * (n_src + k) + p] for k in range(3))

        def load_x(x_refs=x_refs, p=p):
            return _load_tokens(x_refs, (per_step * pl.program_id(0) + p) % tiles_per_row, n_ctx_tiles)

        chains.append((p * (INP_STAGES - 1),
                       _inp_tile_chain(p, load_x, mod_ref, cos_ref, sin_ref, w_ref, hg_ref, mlb_ref,
                                       isf_ref, wa_ref, ba_ref, ps_ref, pk_ref, pg_ref,
                                       gate_ref)))
    _interleave(chains)


def _inp(xs, mods, w_in_p, cos_t, sin_t, hg_par, ml_bias, wa, ba, n_ctx_tiles):
    B = xs[0].shape[0]
    NT = sum(a.shape[1] for a in xs)
    T = ROW_TILE
    tpr = NT // T
    n_all = B * tpr
    per_step = next(n for n in (INP_TILES_PER_STEP, 2, 1) if n_all % n == 0)
    params = [hg_par, ml_bias, jnp.asarray(_NP["is_f"]), wa, ba]

    def row_tile(j, p):
        g = per_step * j + p
        return g // tpr, g % tpr

    def x_specs(p):
        return _token_specs(xs, functools.partial(row_tile, p=p), n_ctx_tiles)

    def mod_spec(p):
        return pl.BlockSpec((1, 1, 1, 6 * D_MODEL),
                            lambda j: (row_tile(j, p)[0], jnp.where(row_tile(j, p)[1] < n_ctx_tiles, 0, 1), 0, 0))

    def rope_spec(p):
        return pl.BlockSpec((T, GROUP_W), lambda j: (row_tile(j, p)[1], 0))

    slots = range(per_step)
    outs = pl.pallas_call(
        functools.partial(_inp_kernel, len(xs), n_ctx_tiles, tpr, per_step),
        grid=(n_all // per_step,),
        in_specs=[s for p in slots for s in x_specs(p)]
        + [mod_spec(p) for p in slots] + [rope_spec(p) for p in slots] + [rope_spec(p) for p in slots]
        + [pl.BlockSpec((D_MODEL, IN_PAD), lambda j: (0, 0), pipeline_mode=pl.Buffered(1))]
        + [_full_spec(a.shape) for a in params],
        out_specs=[pl.BlockSpec((per_step, T, PS_W), lambda j: (j, 0, 0)),
                   pl.BlockSpec((2, per_step, T, GROUP_W), lambda j: (0, j, 0, 0)),
                   pl.BlockSpec((2, per_step, T, PG_W), lambda j: (0, j, 0, 0)),
                   pl.BlockSpec((per_step, T, D_MODEL), lambda j: (j, 0, 0))],
        out_shape=[jax.ShapeDtypeStruct((n_all, T, PS_W), BF16),
                   jax.ShapeDtypeStruct((2, n_all, T, GROUP_W), BF16),
                   jax.ShapeDtypeStruct((2, n_all, T, PG_W), F32),
                   jax.ShapeDtypeStruct((n_all, T, D_MODEL), BF16)],
        compiler_params=pltpu.CompilerParams(dimension_semantics=("parallel",),
                                             vmem_limit_bytes=VMEM_LIMIT),
        name="inp",
    )(*(list(xs) * per_step), *([mods] * per_step), *([cos_t] * per_step), *([sin_t] * per_step),
      w_in_p, *params)
    ps, pk, pg, gate = outs
    return (ps.reshape(B, NT, PS_W), pk.reshape(2, B, NT, GROUP_W), pg.reshape(2, B, NT, PG_W),
            gate.reshape(B, NT, D_MODEL))


def _head_blocks(a16, hmask_ref):
    C, W = a16.shape
    if W == LANES:
        return jnp.concatenate([a16, a16, a16, a16], axis=0) * hmask_ref[:, GROUP_W:GROUP_W + W]
    even = a16 * hmask_ref[0:C, 0:W]
    odd = a16 * hmask_ref[C:2 * C, 0:W]
    zero = jnp.zeros((C, LANES), a16.dtype)
    return jnp.concatenate([
        jnp.concatenate([even[:, 0:LANES], zero], axis=1),
        jnp.concatenate([odd[:, 0:LANES], zero], axis=1),
        jnp.concatenate([zero, even[:, LANES:W]], axis=1),
        jnp.concatenate([zero, odd[:, LANES:W]], axis=1)], axis=0)


def _last_row(a, d):
    return a[CHUNK - 1:CHUNK, :] if d == 0 else a[0:1, :]


def _level_exponent(d, h, log_a, b, row4):
    C, W = log_a.shape
    if h >= 4:
        b3 = b.reshape(C // (2 * h), 2 * h, W)
        r = h - 1 if d == 0 else h
        ref = jnp.broadcast_to(b3[:, r:r + 1, :], b3.shape).reshape(C, W)
        return _neg_abs(b - ref)
    up = pltpu.roll(log_a, C - 1, 0)
    dn = pltpu.roll(log_a, 1, 0)
    if d == 0:
        return jnp.where(row4 == 0, up, jnp.where(row4 == 1, 0.0,
                                                  log_a + jnp.where(row4 == 3, dn, 0.0)))
    return jnp.where(row4 == 0, log_a + up,
                     jnp.where(row4 == 1, log_a, jnp.where(row4 == 2, 0.0, dn)))


def _interleave(chains):
    live = list(chains)
    rnd = 0
    while live:
        for item in list(live):
            start, g = item
            if rnd >= start:
                try:
                    next(g)
                except StopIteration:
                    live.remove(item)
        rnd += 1


def _update_state(st_ref, d, st, decay_row, bd_state, upd):
    rows = st.shape[0] // HEADS
    per_head = st.shape[1] // HEADS
    for h in range(HEADS):
        j = (h * per_head) // LANES
        rs = slice(h * rows, (h + 1) * rows)
        cs = slice(j * LANES, (j + 1) * LANES)
        st_ref[d, rs, cs] = decay_row[:, cs] * st[rs, cs] + bd_state[rs, cs] * upd[rs, cs]


def _vector_decay_chain(d, q16, k16, v16, log_a, put, o_y, st_ref, lev_ref, hmask_ref, bd_state):
    C, W = log_a.shape
    G = GROUP_W
    b = _cumsum_rows(log_a, d)
    row4 = lax.broadcasted_iota(jnp.int32, (C, W), 0) & 3
    order = sorted(range(len(LEVELS)), key=lambda li: LEVELS[li] >= 4)
    attn = None
    pending = None
    for li in order:
        h = LEVELS[li]
        if h == 1:
            qa = jnp.concatenate([q16 * jnp.exp2(log_a).astype(BF16), q16], axis=0)
            kb = k16
        else:
            f16 = jnp.exp2(_level_exponent(d, h, log_a, b, row4)).astype(BF16)
            qa = q16 * f16
            kb = k16 * f16
        part = _dot_nt(qa, _head_blocks(kb, hmask_ref))
        if pending is not None:
            attn = pending if attn is None else attn + pending
        yield
        part = part.astype(BF16)
        pending = (part * lev_ref[d, li] if h > 1 else
                   part[0:C] * lev_ref[d, li] + part[C:2 * C] * lev_ref[d, len(LEVELS)])
    attn = attn + pending

    st = st_ref[d]
    b_end = _last_row(b, d)
    q_state = q16 * jnp.exp2(b).astype(BF16)
    k_end = k16 * jnp.exp2(b_end - b).astype(BF16)
    o = _dot(attn, _head_blocks(v16, hmask_ref)) + _dot_nt(q_state, st.astype(BF16))
    upd = _dot_tn(v16, k_end)
    yield
    put(o_y, o)
    _update_state(st_ref, d, st, jnp.exp2(b_end), bd_state, upd)


def _mlstm_chain(d, q16, k16, v16, small, put, s_ml, n_ml, m_ml, incl_neg, eye,
                 seli, self_, hmask_ref, bd256, bd256_16):
    C = CHUNK
    G = GROUP_W
    i_bc = _sel_right(small, seli)
    b_col = _sel_right(_cumsum_rows(small, d), self_)
    qk = _dot_nt(q16, _head_blocks(k16, hmask_ref))
    yield
    b_row = jnp.sum(b_col * eye, axis=0, keepdims=True)
    i_row = jnp.sum(i_bc * eye, axis=0, keepdims=True)
    w = b_col - b_row + i_row + incl_neg
    row_max = [jnp.max(w[:, h * HEAD_DIM:(h + 1) * HEAD_DIM], axis=-1, keepdims=True)
               for h in range(HEADS)]
    yield
    m_prev = m_ml[d, 0:1, :]
    inter = b_col + m_prev
    lane_head = lax.broadcasted_iota(jnp.int32, (C, G), 1) // HEAD_DIM
    m_t = None
    for h in range(HEADS):
        mh = jnp.maximum(row_max[h], inter[:, h * HEAD_DIM:h * HEAD_DIM + 1])
        mh = jnp.broadcast_to(mh, (C, G))
        m_t = mh if m_t is None else jnp.where(lane_head == h, mh, m_t)
    g_in = jnp.exp2(inter - m_t)
    s = (qk * jnp.exp2(w - m_t)).astype(BF16)
    c_st = s_ml[d]
    n_row = n_ml[d, 0:1, :]
    qn = (q16.astype(F32) * n_row).astype(BF16)
    num_a = _dot(s, _head_blocks(v16, hmask_ref))
    num_b = _dot(q16, c_st.astype(BF16))
    den_ab = _dot(jnp.concatenate([s, qn], axis=0), bd256_16)
    m_new = _last_row(m_t, d)
    b_end = _last_row(b_col, d)
    a_state = jnp.exp2(b_end + m_prev - m_new)
    kw = k16.astype(F32) * jnp.exp2(b_end - b_col + i_bc - m_new)
    upd = _dot_tn(kw.astype(BF16), v16)
    yield
    num = num_a + g_in * num_b
    den = den_ab[0:C] + g_in * den_ab[C:2 * C]
    put(G, num / jnp.maximum(jnp.abs(den), jnp.exp2(-m_t)))
    _update_state(s_ml, d, c_st, a_state, bd256, upd)
    n_ml[d] = jnp.broadcast_to(a_state * n_row + jnp.sum(kw, axis=0, keepdims=True), (8, G))
    m_ml[d] = jnp.broadcast_to(m_new, (8, G))


def _retention_chain(d, q16, k16, v16, put, s_rt, log_g, dist, incl_neg, hmask_ref, bd256):
    C = CHUNK
    G = GROUP_W
    qk = _dot_nt(q16, _head_blocks(k16, hmask_ref))
    decay = jnp.exp2(log_g * dist + incl_neg)
    yield
    rowc = lax.broadcasted_iota(jnp.int32, (C, 1), 0).astype(F32)
    cnt = (rowc + 1.0) if d == 0 else (C - rowc)
    q_state = q16 * jnp.exp2(log_g * cnt).astype(BF16)
    k_end = k16 * jnp.exp2(log_g * (C - cnt)).astype(BF16)
    st = s_rt[d]
    o = _dot((qk * decay).astype(BF16), _head_blocks(v16, hmask_ref)) + _dot(q_state, st.astype(BF16))
    upd = _dot_tn(k_end, v16)
    yield
    put(2 * G, o)
    _update_state(s_rt, d, st, jnp.exp2(log_g * float(C)), bd256, upd)


def _mix_kernel(psf_ref, psb_ref, pkf_ref, pkb_ref, pgf_ref, pgb_ref, rtg_ref,
                lev_ref, incl_ref, eye_ref, dist_ref,
                hmask_ref, bd256_ref, bdgl_ref, seli_ref, self_ref,
                yf_ref, yb_ref,
                s_hg, s_rt, s_gl, s_ml, n_ml, m_ml):
    C = CHUNK
    G = GROUP_W

    @pl.when(pl.program_id(1) == 0)
    def _():
        s_hg[...] = jnp.zeros_like(s_hg)
        s_rt[...] = jnp.zeros_like(s_rt)
        s_gl[...] = jnp.zeros_like(s_gl)
        s_ml[...] = jnp.zeros_like(s_ml)
        n_ml[...] = jnp.zeros_like(n_ml)
        m_ml[...] = jnp.zeros_like(m_ml)

    bd256 = bd256_ref[...]
    bd256_16 = bd256.astype(BF16)

    def deferred(make):
        yield from make()

    chains = []
    for i in range(STEP_CHUNKS):
        for d, (ps_ref, pk_ref, pg_ref, y_ref) in enumerate(((psf_ref, pkf_ref, pgf_ref, yf_ref),
                                                             (psb_ref, pkb_ref, pgb_ref, yb_ref))):
            sub = i if d == 0 else STEP_CHUNKS - 1 - i
            rows = slice(sub * C, (sub + 1) * C)

            def ps(off, width, ps_ref=ps_ref, rows=rows):
                return ps_ref[0, rows, off:off + width]

            def pg(off, width, pg_ref=pg_ref, rows=rows):
                return pg_ref[0, 0, rows, off:off + width]

            def put(off, val, y_ref=y_ref, rows=rows):
                y_ref[0, rows, off:off + G] = val

            def hgrn(d=d, ps=ps, pg=pg, put=put, pk_ref=pk_ref, rows=rows):
                return _vector_decay_chain(d, ps(O_HG_Q, G), pk_ref[0, 0, rows, :], ps(O_HG_V, G),
                                           pg(G_HG_LF, G), put, 0, s_hg, lev_ref, hmask_ref, bd256)

            def mlstm(d=d, ps=ps, pg=pg, put=put):
                return _mlstm_chain(d, ps(O_ML_Q, G), ps(O_ML_K, G), ps(O_ML_V, G),
                                    pg(G_SMALL, PG_W - G_SMALL), put, s_ml, n_ml, m_ml, incl_ref[d],
                                    eye_ref[...], seli_ref[d], self_ref[d], hmask_ref, bd256, bd256_16)

            def retention(d=d, ps=ps, put=put):
                return _retention_chain(d, ps(O_RT_Q, G), ps(O_RT_K, G), ps(O_RT_V, G), put, s_rt,
                                        rtg_ref[d:d + 1, :], dist_ref[...], incl_ref[d], hmask_ref, bd256)

            def gla(d=d, ps=ps, pg=pg, put=put):
                return _vector_decay_chain(d, ps(O_GL_Q, GLA_W), ps(O_GL_K, GLA_W), ps(O_GL_V, G),
                                           pg(G_GL_LA, GLA_W), put, 3 * G, s_gl, lev_ref, hmask_ref,
                                           bdgl_ref[...])

            chains += [(i * CHUNK_STAGGER, deferred(make)) for make in (hgrn, mlstm, retention, gla)]
    _interleave(chains)


def _mix_call(B, NT, n_ctx_chunks):
    C = CHUNK * STEP_CHUNKS
    assert NT % C == 0 and n_ctx_chunks % STEP_CHUNKS == 0
    NC = NT // C
    n_ctx_blocks = n_ctx_chunks // STEP_CHUNKS
    G = GROUP_W

    def bwd_chunk(j):
        return jnp.where(j < n_ctx_blocks, n_ctx_blocks - 1 - j, NC - 1 - (j - n_ctx_blocks))

    def fwd_map(b, j):
        return (b, j, 0)

    def bwd_map(b, j):
        return (b, bwd_chunk(j), 0)

    consts = [jnp.asarray(_NP["lev"], BF16), jnp.asarray(_NP["incl_neg"]),
              jnp.asarray(_NP["eye"]), jnp.asarray(_NP["dist"]),
              jnp.asarray(_NP["hmask"], BF16),
              jnp.asarray(_NP["bd256"]), jnp.asarray(_NP["bd_gl"]),
              jnp.asarray(_NP["sel_i"], BF16), jnp.asarray(_NP["sel_f"], BF16)]
    call = pl.pallas_call(
        _mix_kernel,
        grid=(B, NC),
        in_specs=[pl.BlockSpec((1, C, PS_W), fwd_map), pl.BlockSpec((1, C, PS_W), bwd_map),
                  pl.BlockSpec((1, 1, C, G), lambda b, j: (0, b, j, 0)),
                  pl.BlockSpec((1, 1, C, G), lambda b, j: (1, b, bwd_chunk(j), 0)),
                  pl.BlockSpec((1, 1, C, PG_W), lambda b, j: (0, b, j, 0)),
                  pl.BlockSpec((1, 1, C, PG_W), lambda b, j: (1, b, bwd_chunk(j), 0)),
                  _full_spec((8, G))] + [_full_spec(a.shape) for a in consts],
        out_specs=[pl.BlockSpec((1, C, D_MODEL), fwd_map), pl.BlockSpec((1, C, D_MODEL), bwd_map)],
        out_shape=[jax.ShapeDtypeStruct((B, NT, D_MODEL), F32)] * 2,
        scratch_shapes=[pltpu.VMEM((2, G, G), F32), pltpu.VMEM((2, G, G), F32),
                        pltpu.VMEM((2, G, GLA_W), F32), pltpu.VMEM((2, G, G), F32),
                        pltpu.VMEM((2, 8, G), F32), pltpu.VMEM((2, 8, G), F32)],
        compiler_params=pltpu.CompilerParams(dimension_semantics=("parallel", "arbitrary"),
                                             vmem_limit_bytes=VMEM_LIMIT),
        name="mix",
    )
    return lambda ps, pk, pg, rt_par: call(ps, ps, pk, pk, pg, pg, rt_par, *consts)


def _out_tile_chain(final, load_x, yf_ref, yb_ref, gate_ref, mod_ref, gh_ref, bd_ref,
                    wo_ref, w1_ref, w2_ref, gf_ref, store):
    D = D_MODEL
    bd = bd_ref[...]
    for g in range(D // GROUP_W):
        sl = slice(g * GROUP_W, (g + 1) * GROUP_W)
        y = yf_ref[0, :, sl] + yb_ref[0, :, sl]
        msq = _dot((y * y).astype(BF16), bd) * (1.0 / HEAD_DIM)
        yn = y * lax.rsqrt(msq + RMS_EPS) * gh_ref[:, sl] * gate_ref[0, :, sl].astype(F32)
        part = _dot(yn.astype(BF16), wo_ref[sl, :])
        mixed = part if g == 0 else mixed + part
    yield
    g1 = mod_ref[0, 0, :, 2 * D:3 * D]
    sh2 = mod_ref[0, 0, :, 3 * D:4 * D]
    sc2 = mod_ref[0, 0, :, 4 * D:5 * D]
    g2 = mod_ref[0, 0, :, 5 * D:6 * D]
    x1 = load_x() + g1 * mixed
    hin = ((x1 * _rms_scale(x1)) * (1.0 + sc2) + sh2).astype(BF16)
    hid = _dot(hin, w1_ref[...])
    yield
    hid = jnp.maximum(hid, 0.0)
    ff = _dot((hid * hid).astype(BF16), w2_ref[...])
    yield
    x2 = x1 + g2 * ff
    if final:
        x2 = (x2 * _rms_scale(x2)) * gf_ref[...]
    store(x2)


def _out_kernel(final, n_src, n_ctx_tiles, t0, tiles_per_row, per_step, *refs):
    n_tok = per_step * (n_src + 4)
    tok_refs = refs[:n_tok]
    gh_ref, bd_ref, wo_ref, w1_ref, w2_ref, gf_ref, o_ref = refs[n_tok:]
    chains = []
    for p in range(per_step):
        x_refs = tok_refs[p * n_src:(p + 1) * n_src]
        yf_ref, yb_ref, gate_ref, mod_ref = (tok_refs[per_step * (n_src + k) + p] for k in range(4))

        def load_x(x_refs=x_refs, p=p):
            return _load_tokens(x_refs, (per_step * pl.program_id(0) + p) % tiles_per_row + t0,
                                n_ctx_tiles)

        def store(val, p=p):
            o_ref[p] = val

        chains.append((p, _out_tile_chain(final, load_x, yf_ref, yb_ref, gate_ref, mod_ref, gh_ref,
                                          bd_ref, wo_ref, w1_ref, w2_ref, gf_ref, store)))
    _interleave(chains)


def _out(xs, yf, yb, gate, mods, g_heads, w_out, w_ff1, w_ff2, g_final, n_ctx_tiles, final):
    B, NT, D = yf.shape
    T = ROW_TILE
    t0 = n_ctx_tiles if final else 0
    tpr = NT // T - t0
    n_all = B * tpr
    per_step = 2 if n_all % 2 == 0 else 1
    bd = jnp.asarray(_NP["bd256"], BF16)
    wspec = lambda shape: pl.BlockSpec(shape, lambda j: (0, 0), pipeline_mode=pl.Buffered(1))

    def row_tile(j, p):
        g = per_step * j + p
        return g // tpr, g % tpr + t0

    def x_specs(p):
        return _token_specs(xs, functools.partial(row_tile, p=p), n_ctx_tiles)

    def tok(p):
        return pl.BlockSpec((1, T, D), lambda j: (*row_tile(j, p), 0))

    def mod_spec(p):
        return pl.BlockSpec((1, 1, 1, 6 * D),
                            lambda j: (row_tile(j, p)[0], jnp.where(row_tile(j, p)[1] < n_ctx_tiles, 0, 1), 0, 0))

    slots = range(per_step)
    out = pl.pallas_call(
        functools.partial(_out_kernel, final, len(xs), n_ctx_tiles, t0, tpr, per_step),
        grid=(n_all // per_step,),
        in_specs=[s for p in slots for s in x_specs(p)]
        + [tok(p) for _ in range(3) for p in slots] + [mod_spec(p) for p in slots]
        + [pl.BlockSpec((1, D), lambda j: (0, 0)),
           pl.BlockSpec((GROUP_W, GROUP_W), lambda j: (0, 0)),
           wspec((D, D)), wspec((D, D_FF)), wspec((D_FF, D)),
           pl.BlockSpec((1, D), lambda j: (0, 0))],
        out_specs=pl.BlockSpec((per_step, T, D), lambda j: (j, 0, 0)),
        out_shape=jax.ShapeDtypeStruct((n_all, T, D), F32),
        compiler_params=pltpu.CompilerParams(dimension_semantics=("parallel",),
                                             vmem_limit_bytes=VMEM_LIMIT),
        name="out_final" if final else "out",
    )(*(list(xs) * per_step), *([yf] * per_step), *([yb] * per_step), *([gate] * per_step),
      *([mods] * per_step), g_heads.reshape(1, D), bd, w_out, w_ff1, w_ff2, g_final.reshape(1, D))
    return out.reshape(B, tpr * T, D)


_REF_LAYOUT = (('hg_q', 256), ('hg_f_fwd', 256), ('hg_f_bwd', 256), ('hg_i', 256), ('hg_g', 256),
               ('ml_q', 256), ('ml_k', 256), ('ml_v', 256), ('ml_if', 16), ('ml_o', 256),
               ('rt_q', 256), ('rt_k', 256), ('rt_v', 256), ('rt_g', 256),
               ('gl_q', 128), ('gl_k', 128), ('gl_v', 256), ('gl_a_fwd', 16), ('gl_a_bwd', 16),
               ('gl_g', 256))
_NEW_ORDER = ('hg_q', 'hg_f_fwd', 'hg_f_bwd', 'hg_i', 'ml_q', 'ml_k', 'ml_v', 'rt_q', 'rt_k', 'rt_v',
              'gl_q', 'gl_k', 'gl_v', 'ml_if', 'gl_a_fwd', 'gl_a_bwd', 'PAD', 'hg_g', 'ml_o', 'rt_g', 'gl_g')


def _reorder_columns(w):
    starts, off = {}, 0
    for name, width in _REF_LAYOUT:
        starts[name] = (off, width)
        off += width
    parts = []
    for name in _NEW_ORDER:
        if name == 'PAD':
            parts.append(jnp.zeros((w.shape[0], 128 - 48), BF16))
        else:
            a, width = starts[name]
            parts.append(w[:, a:a + width].astype(BF16))
    out = jnp.concatenate(parts, axis=1)
    assert out.shape[1] == IN_PAD
    return out


def _rope_tables(n_ctx, n_lat):
    rows = n_lat // GRID_W
    n_freq = HEAD_DIM // 4
    inv = np.float32(ROPE_BASE) ** (-np.arange(n_freq, dtype=np.float32) / n_freq)
    ar = np.arange(rows, dtype=np.float32)[:, None] * inv[None, :]
    ac = np.arange(GRID_W, dtype=np.float32)[:, None] * inv[None, :]

    def expand(per_row, per_col, ctx_value):
        lat = jnp.concatenate([jnp.repeat(jnp.asarray(per_row), GRID_W, axis=0),
                               jnp.tile(jnp.asarray(per_col), (rows, 1))], axis=-1)
        full = jnp.concatenate([jnp.full((n_ctx, HEAD_DIM), ctx_value, F32), lat], axis=0)
        return jnp.tile(full, (1, HEADS))

    cos = expand(np.concatenate([np.cos(ar), np.cos(ar)], -1), np.concatenate([np.cos(ac), np.cos(ac)], -1), 1.0)
    sin = expand(np.concatenate([-np.sin(ar), np.sin(ar)], -1), np.concatenate([-np.sin(ac), np.sin(ac)], -1), 0.0)
    return cos, sin


def kernel(x, c, ctx, c_ctx, w_ada, b_ada, w_in, g_heads, hgrn_lb_logits, ml_gate_bias,
           rt_decay_logit, gla_w_a, gla_b_a, w_out, w_ff1, w_ff2, g_final):
    B, L, D = x.shape
    Lc = ctx.shape[1]
    depth = w_in.shape[0]
    assert D == D_MODEL and L % ROW_TILE == 0 and Lc % ROW_TILE == 0 and L % GRID_W == 0
    n_ctx_tiles = Lc // ROW_TILE
    n_ctx_chunks = Lc // CHUNK

    xs = (ctx, x)
    cos_t, sin_t = _rope_tables(Lc, L)
    mix = _mix_call(B, Lc + L, n_ctx_chunks)

    n_c = -(-(B + 1) // 8) * 8
    cvecs = jnp.zeros((n_c, D), F32).at[:B].set(c).at[B].set(c_ctx)

    sm = jax.nn.softmax(hgrn_lb_logits.astype(F32), axis=0)
    lb_all = jnp.maximum(jnp.cumsum(sm, axis=0) - sm[:1], 0.0)

    out = None
    for layer in range(depth):
        final = layer == depth - 1
        mod = _ada(cvecs, w_ada, b_ada, layer)
        mods = jnp.stack([jnp.broadcast_to(mod[B], (B, 6 * D)), mod[:B]], axis=1)[:, :, None, :]

        w_in_p = _reorder_columns(w_in[layer])
        lb = lb_all[layer]
        hg_par = jnp.stack([jnp.log(lb[0]) * LOG2E, jnp.log1p(-lb[0]) * LOG2E, 1.0 - lb[0],
                            jnp.log(lb[1]) * LOG2E, jnp.log1p(-lb[1]) * LOG2E, 1.0 - lb[1],
                            jnp.zeros_like(lb[0]), jnp.zeros_like(lb[0])], axis=0)
        ml_bias = jnp.zeros((1, 128), F32).at[0, :16].set(ml_gate_bias[layer].astype(F32).reshape(16))
        wa = jnp.zeros((128, 2 * GLA_W), F32)
        wa = wa.at[16:32, :GLA_W].set(gla_w_a[layer, 0].astype(F32))
        wa = wa.at[32:48, GLA_W:].set(gla_w_a[layer, 1].astype(F32)).astype(BF16)
        ba = jnp.zeros((8, 2 * GLA_W), F32).at[0].set(gla_b_a[layer].astype(F32).reshape(2 * GLA_W))
        ps, pk, pg, gate = _inp(xs, mods, w_in_p, cos_t, sin_t, hg_par, ml_bias, wa, ba, n_ctx_tiles)

        rt_par = jnp.zeros((8, GROUP_W), F32).at[:2].set(
            jnp.repeat(jax.nn.log_sigmoid(rt_decay_logit[layer].astype(F32)) * LOG2E, HEAD_DIM, axis=-1))
        yf, yb = mix(ps, pk, pg, rt_par)

        res = _out(xs, yf, yb, gate, mods, g_heads[layer], w_out[layer].astype(BF16),
                   w_ff1[layer].astype(BF16), w_ff2[layer].astype(BF16), g_final,
                   n_ctx_tiles, final)
        if final:
            out = res
        else:
            xs = (res,)
    return out
```

```python
import functools

import numpy as np
import jax
import jax.numpy as jnp
from jax import lax
from jax.experimental import pallas as pl
from jax.experimental.pallas import tpu as pltpu

F32 = jnp.float32
BF16 = jnp.bfloat16

D_MODEL = 1024
GROUP_W = 256
HEADS = 4
HEAD_DIM = 64
GLA_DK = 32
GLA_W = HEADS * GLA_DK
GLA_TAU = 16.0
D_FF = 4 * D_MODEL
GRID_W = 64
ROPE_BASE = 10000.0
RMS_EPS = 1e-6

LANES = 128
SUBLANES = 8

CHUNK = 64
STEP_CHUNKS = 4
CHUNK_STAGGER = 3
LEVELS = (32, 16, 8, 4, 2, 1)
ROW_TILE = 256
INP_STAGES = 14
INP_TILES_PER_STEP = 4
INP_TILE_STAGGER = INP_STAGES - 3

W_SMALL = 3072
W_GATES = 3200
IN_PAD = W_GATES + D_MODEL

O_HG_Q, O_HG_V = 0, 256
O_ML_Q, O_ML_K, O_ML_V = 512, 768, 1024
O_RT_Q, O_RT_K, O_RT_V = 1280, 1536, 1792
O_GL_Q, O_GL_K, O_GL_V = 2048, 2176, 2304
PS_W = 2560
G_HG_LF, G_GL_LA, G_SMALL = 0, 256, 384
PG_W = 512

VMEM_LIMIT = 56 * 1024 * 1024

NEG_INF = float("-inf")
LOG2E = 1.4426950408889634


def _dot(a, b):
    return jnp.dot(a, b, preferred_element_type=F32)


def _dot_nt(a, b):
    return lax.dot_general(a, b, (((1,), (1,)), ((), ())), preferred_element_type=F32)


def _dot_tn(a, b):
    return lax.dot_general(a, b, (((0,), (0,)), ((), ())), preferred_element_type=F32)


def _split2(a):
    hi = a.astype(BF16)
    return hi, (a - hi.astype(F32)).astype(BF16)


def _cumsum_rows(x, d):
    n, width = x.shape
    tiles = n // SUBLANES
    x3 = x.reshape(tiles, SUBLANES, width)
    row = lax.broadcasted_iota(jnp.int32, (1, SUBLANES, width), 1)
    s = 1
    while s < SUBLANES:
        if d == 0:
            x3 = x3 + jnp.where(row >= s, pltpu.roll(x3, s, 1), 0.0)
        else:
            x3 = x3 + jnp.where(row < SUBLANES - s, pltpu.roll(x3, SUBLANES - s, 1), 0.0)
        s *= 2
    order = range(tiles) if d == 0 else range(tiles - 1, -1, -1)
    edge = SUBLANES - 1 if d == 0 else 0
    out = [None] * tiles
    carry = None
    for j in order:
        out[j] = x3[j] if carry is None else x3[j] + carry
        carry = out[j][edge:edge + 1]
    return jnp.concatenate(out, axis=0)


def _sel_right(a, m):
    hi, lo = _split2(a)
    return _dot(hi, m) + _dot(lo, m)


def _neg_abs(x):
    return pltpu.bitcast(pltpu.bitcast(x, jnp.uint32) | jnp.uint32(0x80000000), F32)


def _log2_sigmoid(z2):
    return jnp.minimum(z2, 0.0) - jnp.log2(1.0 + jnp.exp2(_neg_abs(z2)))


def _sigmoid(z):
    return 1.0 / (1.0 + jnp.exp(-z))


def _silu(z):
    return z * _sigmoid(z)


def _rms_scale(x):
    return lax.rsqrt(jnp.mean(x * x, axis=-1, keepdims=True) + RMS_EPS)


def _np_consts():
    C = CHUNK
    t = np.arange(C)
    T, U = np.meshgrid(t, t, indexing="ij")
    incl = [U <= T, U >= T]

    lev = np.zeros((2, len(LEVELS) + 1, C, 4 * C), np.float32)
    lev[:, len(LEVELS)] = np.tile(np.eye(C, dtype=np.float32), (1, 4))
    for d in range(2):
        for li, h in enumerate(LEVELS):
            same = (T // (2 * h)) == (U // (2 * h))
            if d == 0:
                pair = same & (T % (2 * h) >= h) & (U % (2 * h) < h)
            else:
                pair = same & (T % (2 * h) < h) & (U % (2 * h) >= h)
            lev[d, li] = np.tile(pair.astype(np.float32), (1, 4))

    incl_neg = np.stack([np.tile(np.where(incl[d], 0.0, NEG_INF).astype(np.float32), (1, 4))
                         for d in range(2)])
    eye = np.tile(np.eye(C, dtype=np.float32), (1, 4))
    dist = np.tile(np.abs(T - U).astype(np.float32), (1, 4))

    head_of_row = np.arange(4 * C) // C
    head_of_lane = np.arange(GROUP_W) // HEAD_DIM
    parity = np.where(head_of_row < 2, head_of_row, -1)
    hmask = np.concatenate(
        [(head_of_lane % 2)[None, :] == parity[:, None],
         head_of_row[:, None] == (np.arange(GLA_W) // GLA_DK)[None, :]], axis=1).astype(np.float32)
    bd256 = ((np.arange(GROUP_W) // HEAD_DIM)[:, None]
             == (np.arange(GROUP_W) // HEAD_DIM)[None, :]).astype(np.float32)
    bd_gl = ((np.arange(GROUP_W) // HEAD_DIM)[:, None]
             == (np.arange(GLA_W) // GLA_DK)[None, :]).astype(np.float32)

    sel_i = np.zeros((2, 128, GROUP_W), np.float32)
    sel_f = np.zeros((2, 128, GROUP_W), np.float32)
    for d in range(2):
        for h in range(HEADS):
            sel_i[d, d * 8 + h, h * HEAD_DIM:(h + 1) * HEAD_DIM] = 1.0
            sel_f[d, d * 8 + 4 + h, h * HEAD_DIM:(h + 1) * HEAD_DIM] = 1.0
    is_f = np.zeros((1, 128), np.float32)
    is_f[0, 4:8] = 1.0
    is_f[0, 12:16] = 1.0
    return dict(lev=lev, incl_neg=incl_neg, eye=eye, dist=dist,
                hmask=hmask, bd256=bd256, bd_gl=bd_gl,
                sel_i=sel_i, sel_f=sel_f, is_f=is_f)


_NP = _np_consts()


def _full_spec(shape):
    n = len(shape)
    return pl.BlockSpec(tuple(shape), lambda *_: (0,) * n)


def _ada_kernel(c_ref, w_ref, b_ref, o_ref):
    act = _silu(c_ref[...]).astype(BF16)
    o_ref[...] = _dot(act, w_ref[0].astype(BF16)) + b_ref[0]


def _ada(cvecs, w_all, b_all, layer):
    rows = cvecs.shape[0]
    depth, _, n = w_all.shape
    tn = 1024
    return pl.pallas_call(
        _ada_kernel,
        grid=(n // tn,),
        in_specs=[pl.BlockSpec((rows, D_MODEL), lambda i: (0, 0)),
                  pl.BlockSpec((1, D_MODEL, tn), lambda i: (layer, 0, i)),
                  pl.BlockSpec((1, 1, tn), lambda i: (layer, 0, i))],
        out_specs=pl.BlockSpec((rows, tn), lambda i: (0, i)),
        out_shape=jax.ShapeDtypeStruct((rows, n), F32),
        compiler_params=pltpu.CompilerParams(dimension_semantics=("arbitrary",),
                                             vmem_limit_bytes=VMEM_LIMIT),
        name="ada",
    )(cvecs, w_all, b_all.reshape(depth, 1, n))


def _rope(x, cos, sin):
    lane = lax.broadcasted_iota(jnp.int32, x.shape, 1)
    low = (lane % 32) < 16
    partner = jnp.where(low, pltpu.roll(x, GROUP_W - 16, 1), pltpu.roll(x, 16, 1))
    return x * cos + partner * sin


def _token_specs(xs, row_tile, n_ctx_tiles):
    T = ROW_TILE
    if len(xs) == 1:
        return [pl.BlockSpec((1, T, D_MODEL), lambda j: (*row_tile(j), 0))]
    return [pl.BlockSpec((1, T, D_MODEL),
                         lambda j: (row_tile(j)[0], jnp.minimum(row_tile(j)[1], n_ctx_tiles - 1), 0)),
            pl.BlockSpec((1, T, D_MODEL),
                         lambda j: (row_tile(j)[0], jnp.maximum(row_tile(j)[1] - n_ctx_tiles, 0), 0))]


def _load_tokens(x_refs, tile, n_ctx_tiles):
    if len(x_refs) == 1:
        return x_refs[0][0]
    return jnp.where(tile < n_ctx_tiles, x_refs[0][0], x_refs[1][0])


def _inp_tile_chain(p, load_x, mod_ref, cos_ref, sin_ref, w_ref, hg_ref, mlb_ref, isf_ref,
                    wa_ref, ba_ref, ps_ref, pk_ref, pg_ref, gate_ref):
    x = load_x()
    sh = mod_ref[0, 0, :, 0:D_MODEL]
    sc = mod_ref[0, 0, :, D_MODEL:2 * D_MODEL]
    xn = ((x * _rms_scale(x)) * (1.0 + sc) + sh).astype(BF16)

    def proj(a, b):
        return _dot(xn, w_ref[:, a:b])

    G = GROUP_W

    def put(off, width, fn=None):
        def post(res):
            ps_ref[p, :, off:off + width] = (res if fn is None else fn(res)).astype(BF16)
        return post

    def hgrn_decay(d):
        def post(z):
            log_lb = hg_ref[3 * d + 0:3 * d + 1, :]
            log_1mlb = hg_ref[3 * d + 1:3 * d + 2, :]
            one_m_lb = hg_ref[3 * d + 2:3 * d + 3, :]
            z2 = z * LOG2E
            ls = _log2_sigmoid(z2)
            other = log_1mlb + ls
            pg_ref[d, p, :, G_HG_LF:G_HG_LF + G] = (
                jnp.maximum(log_lb, other) + jnp.log2(1.0 + jnp.exp2(_neg_abs(log_lb - other))))
            pk_ref[d, p] = (one_m_lb * jnp.exp2(ls - z2)).astype(BF16)
        return post

    def narrow(small):
        pre2 = (small + mlb_ref[...]) * LOG2E
        gate_logs = jnp.where(isf_ref[...] > 0.5, _log2_sigmoid(pre2), pre2)
        za = _dot(small.astype(BF16), wa_ref[...]) + ba_ref[0:1, :]
        log_a = _log2_sigmoid(za * LOG2E) * (1.0 / GLA_TAU)
        for d in range(2):
            pg_ref[d, p, :, G_GL_LA:G_GL_LA + GLA_W] = log_a[:, d * GLA_W:(d + 1) * GLA_W]
            pg_ref[d, p, :, G_SMALL:PG_W] = gate_logs

    def gates(half, fn):
        def post(g):
            gate_ref[p, :, half * 2 * G:(half + 1) * 2 * G] = fn(g).astype(BF16)
        return post

    def rope(scale):
        return lambda r: _rope(r * scale if scale != 1.0 else r, cos_ref[...], sin_ref[...])

    o = 10 * G
    stages = [
        (W_GATES, W_GATES + 2 * G, gates(0, _sigmoid)),
        (0, G, put(O_HG_Q, G, _silu)),
        (G, 2 * G, hgrn_decay(0)),
        (3 * G, 5 * G, put(O_HG_V, 2 * G)),
        (2 * G, 3 * G, hgrn_decay(1)),
        (5 * G, 6 * G, put(O_ML_K, G, lambda r: r * (HEAD_DIM ** -0.5))),
        (W_GATES + 2 * G, IN_PAD, gates(1, _silu)),
        (6 * G, 7 * G, put(O_ML_V, G)),
        (7 * G, 8 * G, put(O_RT_Q, G, rope(1.0))),
        (9 * G, o + GLA_W, put(O_RT_V, G + GLA_W)),
        (8 * G, 9 * G, put(O_RT_K, G, rope(HEAD_DIM ** -0.5))),
        (o + GLA_W, o + 2 * GLA_W, put(O_GL_K, GLA_W, lambda r: r * (GLA_DK ** -0.5))),
        (W_SMALL, W_GATES, narrow),
        (o + 2 * GLA_W, W_SMALL, put(O_GL_V, G)),
    ]
    assert len(stages) == INP_STAGES
    pending = None
    for a, b, post in stages:
        res = proj(a, b)
        if pending is not None:
            pending[1](pending[0])
        pending = (res, post)
        yield
    pending[1](pending[0])


def _inp_kernel(n_src, n_ctx_tiles, tiles_per_row, per_step, *refs):
    n_tok = per_step * (n_src + 3)
    tok_refs = refs[:n_tok]
    w_ref, hg_ref, mlb_ref, isf_ref, wa_ref, ba_ref = refs[n_tok:n_tok + 6]
    ps_ref, pk_ref, pg_ref, gate_ref = refs[n_tok + 6:]
    chains = []
    for p in range(per_step):
        x_refs = tok_refs[p * n_src:(p + 1) * n_src]
        mod_ref, cos_ref, sin_ref = (tok_refs[per_step * (n_src + k) + p] for k in range(3))

        def load_x(x_refs=x_refs, p=p):
            return _load_tokens(x_refs, (per_step * pl.program_id(0) + p) % tiles_per_row, n_ctx_tiles)

        chains.append((p * INP_TILE_STAGGER,
                       _inp_tile_chain(p, load_x, mod_ref, cos_ref, sin_ref, w_ref, hg_ref, mlb_ref,
                                       isf_ref, wa_ref, ba_ref, ps_ref, pk_ref, pg_ref,
                                       gate_ref)))
    _interleave(chains)


def _inp(xs, mods, w_in_p, cos_t, sin_t, hg_par, ml_bias, wa, ba, n_ctx_tiles):
    B = xs[0].shape[0]
    NT = sum(a.shape[1] for a in xs)
    T = ROW_TILE
    tpr = NT // T
    n_all = B * tpr
    most = INP_TILES_PER_STEP if len(xs) == 1 else 2
    per_step = next(n for n in (most, 2, 1) if n_all % n == 0)
    params = [hg_par, ml_bias, jnp.asarray(_NP["is_f"]), wa, ba]

    def row_tile(j, p):
        g = per_step * j + p
        return g // tpr, g % tpr

    def x_specs(p):
        return _token_specs(xs, functools.partial(row_tile, p=p), n_ctx_tiles)

    def mod_spec(p):
        return pl.BlockSpec((1, 1, 1, 6 * D_MODEL),
                            lambda j: (row_tile(j, p)[0], jnp.where(row_tile(j, p)[1] < n_ctx_tiles, 0, 1), 0, 0))

    def rope_spec(p):
        return pl.BlockSpec((T, GROUP_W), lambda j: (row_tile(j, p)[1], 0))

    slots = range(per_step)
    outs = pl.pallas_call(
        functools.partial(_inp_kernel, len(xs), n_ctx_tiles, tpr, per_step),
        grid=(n_all // per_step,),
        in_specs=[s for p in slots for s in x_specs(p)]
        + [mod_spec(p) for p in slots] + [rope_spec(p) for p in slots] + [rope_spec(p) for p in slots]
        + [pl.BlockSpec((D_MODEL, IN_PAD), lambda j: (0, 0), pipeline_mode=pl.Buffered(1))]
        + [_full_spec(a.shape) for a in params],
        out_specs=[pl.BlockSpec((per_step, T, PS_W), lambda j: (j, 0, 0)),
                   pl.BlockSpec((2, per_step, T, GROUP_W), lambda j: (0, j, 0, 0)),
                   pl.BlockSpec((2, per_step, T, PG_W), lambda j: (0, j, 0, 0)),
                   pl.BlockSpec((per_step, T, D_MODEL), lambda j: (j, 0, 0))],
        out_shape=[jax.ShapeDtypeStruct((n_all, T, PS_W), BF16),
                   jax.ShapeDtypeStruct((2, n_all, T, GROUP_W), BF16),
                   jax.ShapeDtypeStruct((2, n_all, T, PG_W), F32),
                   jax.ShapeDtypeStruct((n_all, T, D_MODEL), BF16)],
        compiler_params=pltpu.CompilerParams(dimension_semantics=("parallel",),
                                             vmem_limit_bytes=VMEM_LIMIT),
        name="inp",
    )(*(list(xs) * per_step), *([mods] * per_step), *([cos_t] * per_step), *([sin_t] * per_step),
      w_in_p, *params)
    ps, pk, pg, gate = outs
    return (ps.reshape(B, NT, PS_W), pk.reshape(2, B, NT, GROUP_W), pg.reshape(2, B, NT, PG_W),
            gate.reshape(B, NT, D_MODEL))


def _head_blocks(a16, hmask_ref):
    C, W = a16.shape
    if W == LANES:
        return jnp.concatenate([a16, a16, a16, a16], axis=0) * hmask_ref[:, GROUP_W:GROUP_W + W]
    even = a16 * hmask_ref[0:C, 0:W]
    odd = a16 * hmask_ref[C:2 * C, 0:W]
    zero = jnp.zeros((C, LANES), a16.dtype)
    return jnp.concatenate([
        jnp.concatenate([even[:, 0:LANES], zero], axis=1),
        jnp.concatenate([odd[:, 0:LANES], zero], axis=1),
        jnp.concatenate([zero, even[:, LANES:W]], axis=1),
        jnp.concatenate([zero, odd[:, LANES:W]], axis=1)], axis=0)


def _last_row(a, d):
    return a[CHUNK - 1:CHUNK, :] if d == 0 else a[0:1, :]


def _level_exponent(d, h, log_a, b, row4):
    C, W = log_a.shape
    if h >= 4:
        b3 = b.reshape(C // (2 * h), 2 * h, W)
        r = h - 1 if d == 0 else h
        ref = jnp.broadcast_to(b3[:, r:r + 1, :], b3.shape).reshape(C, W)
        return _neg_abs(b - ref)
    up = pltpu.roll(log_a, C - 1, 0)
    dn = pltpu.roll(log_a, 1, 0)
    if d == 0:
        return jnp.where(row4 == 0, up, jnp.where(row4 == 1, 0.0,
                                                  log_a + jnp.where(row4 == 3, dn, 0.0)))
    return jnp.where(row4 == 0, log_a + up,
                     jnp.where(row4 == 1, log_a, jnp.where(row4 == 2, 0.0, dn)))


def _interleave(chains):
    live = list(chains)
    rnd = 0
    while live:
        for item in list(live):
            start, g = item
            if rnd >= start:
                try:
                    next(g)
                except StopIteration:
                    live.remove(item)
        rnd += 1


def _update_state(st_ref, d, st, decay_row, bd_state, upd):
    rows = st.shape[0] // HEADS
    per_head = st.shape[1] // HEADS
    for h in range(HEADS):
        j = (h * per_head) // LANES
        rs = slice(h * rows, (h + 1) * rows)
        cs = slice(j * LANES, (j + 1) * LANES)
        st_ref[d, rs, cs] = decay_row[:, cs] * st[rs, cs] + bd_state[rs, cs] * upd[rs, cs]


def _vector_decay_chain(d, q16, k16, v16, log_a, put, o_y, st_ref, lev_ref, hmask_ref, bd_state):
    C, W = log_a.shape
    G = GROUP_W
    b = _cumsum_rows(log_a, d)
    row4 = lax.broadcasted_iota(jnp.int32, (C, W), 0) & 3
    order = sorted(range(len(LEVELS)), key=lambda li: LEVELS[li] >= 4)
    attn = None
    pending = None
    for li in order:
        h = LEVELS[li]
        if h == 1:
            qa = jnp.concatenate([q16 * jnp.exp2(log_a).astype(BF16), q16], axis=0)
            kb = k16
        else:
            f16 = jnp.exp2(_level_exponent(d, h, log_a, b, row4)).astype(BF16)
            qa = q16 * f16
            kb = k16 * f16
        part = _dot_nt(qa, _head_blocks(kb, hmask_ref))
        if pending is not None:
            attn = pending if attn is None else attn + pending
        yield
        part = part.astype(BF16)
        pending = (part * lev_ref[d, li] if h > 1 else
                   part[0:C] * lev_ref[d, li] + part[C:2 * C] * lev_ref[d, len(LEVELS)])
    attn = attn + pending

    st = st_ref[d]
    b_end = _last_row(b, d)
    q_state = q16 * jnp.exp2(b).astype(BF16)
    k_end = k16 * jnp.exp2(b_end - b).astype(BF16)
    o = _dot(attn, _head_blocks(v16, hmask_ref)) + _dot_nt(q_state, st.astype(BF16))
    upd = _dot_tn(v16, k_end)
    yield
    put(o_y, o)
    _update_state(st_ref, d, st, jnp.exp2(b_end), bd_state, upd)


def _mlstm_chain(d, q16, k16, v16, small, put, s_ml, n_ml, m_ml, incl_neg, eye,
                 seli, self_, hmask_ref, bd256, bd256_16):
    C = CHUNK
    G = GROUP_W
    i_bc = _sel_right(small, seli)
    b_col = _sel_right(_cumsum_rows(small, d), self_)
    qk = _dot_nt(q16, _head_blocks(k16, hmask_ref))
    yield
    b_row = jnp.sum(b_col * eye, axis=0, keepdims=True)
    i_row = jnp.sum(i_bc * eye, axis=0, keepdims=True)
    w = b_col - b_row + i_row + incl_neg
    row_max = [jnp.max(w[:, h * HEAD_DIM:(h + 1) * HEAD_DIM], axis=-1, keepdims=True)
               for h in range(HEADS)]
    yield
    m_prev = m_ml[d, 0:1, :]
    inter = b_col + m_prev
    lane_head = lax.broadcasted_iota(jnp.int32, (C, G), 1) // HEAD_DIM
    m_t = None
    for h in range(HEADS):
        mh = jnp.maximum(row_max[h], inter[:, h * HEAD_DIM:h * HEAD_DIM + 1])
        mh = jnp.broadcast_to(mh, (C, G))
        m_t = mh if m_t is None else jnp.where(lane_head == h, mh, m_t)
    g_in = jnp.exp2(inter - m_t)
    s = (qk * jnp.exp2(w - m_t)).astype(BF16)
    c_st = s_ml[d]
    n_row = n_ml[d, 0:1, :]
    qn = (q16.astype(F32) * n_row).astype(BF16)
    num_a = _dot(s, _head_blocks(v16, hmask_ref))
    num_b = _dot(q16, c_st.astype(BF16))
    den_ab = _dot(jnp.concatenate([s, qn], axis=0), bd256_16)
    m_new = _last_row(m_t, d)
    b_end = _last_row(b_col, d)
    a_state = jnp.exp2(b_end + m_prev - m_new)
    kw = k16.astype(F32) * jnp.exp2(b_end - b_col + i_bc - m_new)
    upd = _dot_tn(kw.astype(BF16), v16)
    yield
    num = num_a + g_in * num_b
    den = den_ab[0:C] + g_in * den_ab[C:2 * C]
    put(G, num / jnp.maximum(jnp.abs(den), jnp.exp2(-m_t)))
    _update_state(s_ml, d, c_st, a_state, bd256, upd)
    n_ml[d] = jnp.broadcast_to(a_state * n_row + jnp.sum(kw, axis=0, keepdims=True), (8, G))
    m_ml[d] = jnp.broadcast_to(m_new, (8, G))


def _retention_chain(d, q16, k16, v16, put, s_rt, log_g, dist, incl_neg, hmask_ref, bd256):
    C = CHUNK
    G = GROUP_W
    qk = _dot_nt(q16, _head_blocks(k16, hmask_ref))
    decay = jnp.exp2(log_g * dist + incl_neg)
    yield
    rowc = lax.broadcasted_iota(jnp.int32, (C, 1), 0).astype(F32)
    cnt = (rowc + 1.0) if d == 0 else (C - rowc)
    q_state = q16 * jnp.exp2(log_g * cnt).astype(BF16)
    k_end = k16 * jnp.exp2(log_g * (C - cnt)).astype(BF16)
    st = s_rt[d]
    o = _dot((qk * decay).astype(BF16), _head_blocks(v16, hmask_ref)) + _dot(q_state, st.astype(BF16))
    upd = _dot_tn(k_end, v16)
    yield
    put(2 * G, o)
    _update_state(s_rt, d, st, jnp.exp2(log_g * float(C)), bd256, upd)


def _mix_kernel(psf_ref, psb_ref, pkf_ref, pkb_ref, pgf_ref, pgb_ref, rtg_ref,
                lev_ref, incl_ref, eye_ref, dist_ref,
                hmask_ref, bd256_ref, bdgl_ref, seli_ref, self_ref,
                yf_ref, yb_ref,
                s_hg, s_rt, s_gl, s_ml, n_ml, m_ml):
    C = CHUNK
    G = GROUP_W

    @pl.when(pl.program_id(1) == 0)
    def _():
        s_hg[...] = jnp.zeros_like(s_hg)
        s_rt[...] = jnp.zeros_like(s_rt)
        s_gl[...] = jnp.zeros_like(s_gl)
        s_ml[...] = jnp.zeros_like(s_ml)
        n_ml[...] = jnp.zeros_like(n_ml)
        m_ml[...] = jnp.zeros_like(m_ml)

    bd256 = bd256_ref[...]
    bd256_16 = bd256.astype(BF16)

    def deferred(make):
        yield from make()

    chains = []
    for i in range(STEP_CHUNKS):
        for d, (ps_ref, pk_ref, pg_ref, y_ref) in enumerate(((psf_ref, pkf_ref, pgf_ref, yf_ref),
                                                             (psb_ref, pkb_ref, pgb_ref, yb_ref))):
            sub = i if d == 0 else STEP_CHUNKS - 1 - i
            rows = slice(sub * C, (sub + 1) * C)

            def ps(off, width, ps_ref=ps_ref, rows=rows):
                return ps_ref[0, rows, off:off + width]

            def pg(off, width, pg_ref=pg_ref, rows=rows):
                return pg_ref[0, 0, rows, off:off + width]

            def put(off, val, y_ref=y_ref, rows=rows):
                y_ref[0, rows, off:off + G] = val

            def hgrn(d=d, ps=ps, pg=pg, put=put, pk_ref=pk_ref, rows=rows):
                return _vector_decay_chain(d, ps(O_HG_Q, G), pk_ref[0, 0, rows, :], ps(O_HG_V, G),
                                           pg(G_HG_LF, G), put, 0, s_hg, lev_ref, hmask_ref, bd256)

            def mlstm(d=d, ps=ps, pg=pg, put=put):
                return _mlstm_chain(d, ps(O_ML_Q, G), ps(O_ML_K, G), ps(O_ML_V, G),
                                    pg(G_SMALL, PG_W - G_SMALL), put, s_ml, n_ml, m_ml, incl_ref[d],
                                    eye_ref[...], seli_ref[d], self_ref[d], hmask_ref, bd256, bd256_16)

            def retention(d=d, ps=ps, put=put):
                return _retention_chain(d, ps(O_RT_Q, G), ps(O_RT_K, G), ps(O_RT_V, G), put, s_rt,
                                        rtg_ref[d:d + 1, :], dist_ref[...], incl_ref[d], hmask_ref, bd256)

            def gla(d=d, ps=ps, pg=pg, put=put):
                return _vector_decay_chain(d, ps(O_GL_Q, GLA_W), ps(O_GL_K, GLA_W), ps(O_GL_V, G),
                                           pg(G_GL_LA, GLA_W), put, 3 * G, s_gl, lev_ref, hmask_ref,
                                           bdgl_ref[...])

            chains += [(i * CHUNK_STAGGER, deferred(make)) for make in (hgrn, mlstm, retention, gla)]
    _interleave(chains)


def _mix_call(B, NT, n_ctx_chunks):
    C = CHUNK * STEP_CHUNKS
    assert NT % C == 0 and n_ctx_chunks % STEP_CHUNKS == 0
    NC = NT // C
    n_ctx_blocks = n_ctx_chunks // STEP_CHUNKS
    G = GROUP_W

    def bwd_chunk(j):
        return jnp.where(j < n_ctx_blocks, n_ctx_blocks - 1 - j, NC - 1 - (j - n_ctx_blocks))

    def fwd_map(b, j):
        return (b, j, 0)

    def bwd_map(b, j):
        return (b, bwd_chunk(j), 0)

    consts = [jnp.asarray(_NP["lev"], BF16), jnp.asarray(_NP["incl_neg"]),
              jnp.asarray(_NP["eye"]), jnp.asarray(_NP["dist"]),
              jnp.asarray(_NP["hmask"], BF16),
              jnp.asarray(_NP["bd256"]), jnp.asarray(_NP["bd_gl"]),
              jnp.asarray(_NP["sel_i"], BF16), jnp.asarray(_NP["sel_f"], BF16)]
    call = pl.pallas_call(
        _mix_kernel,
        grid=(B, NC),
        in_specs=[pl.BlockSpec((1, C, PS_W), fwd_map), pl.BlockSpec((1, C, PS_W), bwd_map),
                  pl.BlockSpec((1, 1, C, G), lambda b, j: (0, b, j, 0)),
                  pl.BlockSpec((1, 1, C, G), lambda b, j: (1, b, bwd_chunk(j), 0)),
                  pl.BlockSpec((1, 1, C, PG_W), lambda b, j: (0, b, j, 0)),
                  pl.BlockSpec((1, 1, C, PG_W), lambda b, j: (1, b, bwd_chunk(j), 0)),
                  _full_spec((8, G))] + [_full_spec(a.shape) for a in consts],
        out_specs=[pl.BlockSpec((1, C, D_MODEL), fwd_map), pl.BlockSpec((1, C, D_MODEL), bwd_map)],
        out_shape=[jax.ShapeDtypeStruct((B, NT, D_MODEL), F32)] * 2,
        scratch_shapes=[pltpu.VMEM((2, G, G), F32), pltpu.VMEM((2, G, G), F32),
                        pltpu.VMEM((2, G, GLA_W), F32), pltpu.VMEM((2, G, G), F32),
                        pltpu.VMEM((2, 8, G), F32), pltpu.VMEM((2, 8, G), F32)],
        compiler_params=pltpu.CompilerParams(dimension_semantics=("parallel", "arbitrary"),
                                             vmem_limit_bytes=VMEM_LIMIT),
        name="mix",
    )
    return lambda ps, pk, pg, rt_par: call(ps, ps, pk, pk, pg, pg, rt_par, *consts)


def _out_tile_chain(final, load_x, yf_ref, yb_ref, gate_ref, mod_ref, gh_ref, bd_ref,
                    wo_ref, w1_ref, w2_ref, gf_ref, store):
    D = D_MODEL
    bd = bd_ref[...]
    for g in range(D // GROUP_W):
        sl = slice(g * GROUP_W, (g + 1) * GROUP_W)
        y = yf_ref[0, :, sl] + yb_ref[0, :, sl]
        msq = _dot((y * y).astype(BF16), bd) * (1.0 / HEAD_DIM)
        yn = y * lax.rsqrt(msq + RMS_EPS) * gh_ref[:, sl] * gate_ref[0, :, sl].astype(F32)
        part = _dot(yn.astype(BF16), wo_ref[sl, :])
        mixed = part if g == 0 else mixed + part
    yield
    g1 = mod_ref[0, 0, :, 2 * D:3 * D]
    sh2 = mod_ref[0, 0, :, 3 * D:4 * D]
    sc2 = mod_ref[0, 0, :, 4 * D:5 * D]
    g2 = mod_ref[0, 0, :, 5 * D:6 * D]
    x1 = load_x() + g1 * mixed
    hin = ((x1 * _rms_scale(x1)) * (1.0 + sc2) + sh2).astype(BF16)
    hid = _dot(hin, w1_ref[...])
    yield
    hid = jnp.maximum(hid, 0.0)
    ff = _dot((hid * hid).astype(BF16), w2_ref[...])
    yield
    x2 = x1 + g2 * ff
    if final:
        x2 = (x2 * _rms_scale(x2)) * gf_ref[...]
    store(x2)


def _out_kernel(final, n_src, n_ctx_tiles, t0, tiles_per_row, per_step, *refs):
    n_tok = per_step * (n_src + 4)
    tok_refs = refs[:n_tok]
    gh_ref, bd_ref, wo_ref, w1_ref, w2_ref, gf_ref, o_ref = refs[n_tok:]
    chains = []
    for p in range(per_step):
        x_refs = tok_refs[p * n_src:(p + 1) * n_src]
        yf_ref, yb_ref, gate_ref, mod_ref = (tok_refs[per_step * (n_src + k) + p] for k in range(4))

        def load_x(x_refs=x_refs, p=p):
            return _load_tokens(x_refs, (per_step * pl.program_id(0) + p) % tiles_per_row + t0,
                                n_ctx_tiles)

        def store(val, p=p):
            o_ref[p] = val

        chains.append((0, _out_tile_chain(final, load_x, yf_ref, yb_ref, gate_ref, mod_ref, gh_ref,
                                          bd_ref, wo_ref, w1_ref, w2_ref, gf_ref, store)))
    _interleave(chains)


def _out(xs, yf, yb, gate, mods, g_heads, w_out, w_ff1, w_ff2, g_final, n_ctx_tiles, final):
    B, NT, D = yf.shape
    T = ROW_TILE
    t0 = n_ctx_tiles if final else 0
    tpr = NT // T - t0
    n_all = B * tpr
    per_step = 2 if n_all % 2 == 0 else 1
    bd = jnp.asarray(_NP["bd256"], BF16)
    wspec = lambda shape: pl.BlockSpec(shape, lambda j: (0, 0), pipeline_mode=pl.Buffered(1))

    def row_tile(j, p):
        g = per_step * j + p
        return g // tpr, g % tpr + t0

    def x_specs(p):
        return _token_specs(xs, functools.partial(row_tile, p=p), n_ctx_tiles)

    def tok(p):
        return pl.BlockSpec((1, T, D), lambda j: (*row_tile(j, p), 0))

    def mod_spec(p):
        return pl.BlockSpec((1, 1, 1, 6 * D),
                            lambda j: (row_tile(j, p)[0], jnp.where(row_tile(j, p)[1] < n_ctx_tiles, 0, 1), 0, 0))

    slots = range(per_step)
    out = pl.pallas_call(
        functools.partial(_out_kernel, final, len(xs), n_ctx_tiles, t0, tpr, per_step),
        grid=(n_all // per_step,),
        in_specs=[s for p in slots for s in x_specs(p)]
        + [tok(p) for _ in range(3) for p in slots] + [mod_spec(p) for p in slots]
        + [pl.BlockSpec((1, D), lambda j: (0, 0)),
           pl.BlockSpec((GROUP_W, GROUP_W), lambda j: (0, 0)),
           wspec((D, D)), wspec((D, D_FF)), wspec((D_FF, D)),
           pl.BlockSpec((1, D), lambda j: (0, 0))],
        out_specs=pl.BlockSpec((per_step, T, D), lambda j: (j, 0, 0)),
        out_shape=jax.ShapeDtypeStruct((n_all, T, D), F32),
        compiler_params=pltpu.CompilerParams(dimension_semantics=("parallel",),
                                             vmem_limit_bytes=VMEM_LIMIT),
        name="out_final" if final else "out",
    )(*(list(xs) * per_step), *([yf] * per_step), *([yb] * per_step), *([gate] * per_step),
      *([mods] * per_step), g_heads.reshape(1, D), bd, w_out, w_ff1, w_ff2, g_final.reshape(1, D))
    return out.reshape(B, tpr * T, D)


_REF_LAYOUT = (('hg_q', 256), ('hg_f_fwd', 256), ('hg_f_bwd', 256), ('hg_i', 256), ('hg_g', 256),
               ('ml_q', 256), ('ml_k', 256), ('ml_v', 256), ('ml_if', 16), ('ml_o', 256),
               ('rt_q', 256), ('rt_k', 256), ('rt_v', 256), ('rt_g', 256),
               ('gl_q', 128), ('gl_k', 128), ('gl_v', 256), ('gl_a_fwd', 16), ('gl_a_bwd', 16),
               ('gl_g', 256))
_NEW_ORDER = ('hg_q', 'hg_f_fwd', 'hg_f_bwd', 'hg_i', 'ml_q', 'ml_k', 'ml_v', 'rt_q', 'rt_k', 'rt_v',
              'gl_q', 'gl_k', 'gl_v', 'ml_if', 'gl_a_fwd', 'gl_a_bwd', 'PAD', 'hg_g', 'ml_o', 'rt_g', 'gl_g')


def _reorder_columns(w):
    starts, off = {}, 0
    for name, width in _REF_LAYOUT:
        starts[name] = (off, width)
        off += width
    parts = []
    for name in _NEW_ORDER:
        if name == 'PAD':
            parts.append(jnp.zeros((w.shape[0], 128 - 48), BF16))
        else:
            a, width = starts[name]
            parts.append(w[:, a:a + width].astype(BF16))
    out = jnp.concatenate(parts, axis=1)
    assert out.shape[1] == IN_PAD
    return out


def _rope_tables(n_ctx, n_lat):
    rows = n_lat // GRID_W
    n_freq = HEAD_DIM // 4
    inv = np.float32(ROPE_BASE) ** (-np.arange(n_freq, dtype=np.float32) / n_freq)
    ar = np.arange(rows, dtype=np.float32)[:, None] * inv[None, :]
    ac = np.arange(GRID_W, dtype=np.float32)[:, None] * inv[None, :]

    def expand(per_row, per_col, ctx_value):
        lat = jnp.concatenate([jnp.repeat(jnp.asarray(per_row), GRID_W, axis=0),
                               jnp.tile(jnp.asarray(per_col), (rows, 1))], axis=-1)
        full = jnp.concatenate([jnp.full((n_ctx, HEAD_DIM), ctx_value, F32), lat], axis=0)
        return jnp.tile(full, (1, HEADS))

    cos = expand(np.concatenate([np.cos(ar), np.cos(ar)], -1), np.concatenate([np.cos(ac), np.cos(ac)], -1), 1.0)
    sin = expand(np.concatenate([-np.sin(ar), np.sin(ar)], -1), np.concatenate([-np.sin(ac), np.sin(ac)], -1), 0.0)
    return cos, sin


def kernel(x, c, ctx, c_ctx, w_ada, b_ada, w_in, g_heads, hgrn_lb_logits, ml_gate_bias,
           rt_decay_logit, gla_w_a, gla_b_a, w_out, w_ff1, w_ff2, g_final):
    B, L, D = x.shape
    Lc = ctx.shape[1]
    depth = w_in.shape[0]
    assert D == D_MODEL and L % ROW_TILE == 0 and Lc % ROW_TILE == 0 and L % GRID_W == 0
    n_ctx_tiles = Lc // ROW_TILE
    n_ctx_chunks = Lc // CHUNK

    xs = (ctx, x)
    cos_t, sin_t = _rope_tables(Lc, L)
    mix = _mix_call(B, Lc + L, n_ctx_chunks)

    n_c = -(-(B + 1) // 8) * 8
    cvecs = jnp.zeros((n_c, D), F32).at[:B].set(c).at[B].set(c_ctx)

    sm = jax.nn.softmax(hgrn_lb_logits.astype(F32), axis=0)
    lb_all = jnp.maximum(jnp.cumsum(sm, axis=0) - sm[:1], 0.0)

    out = None
    for layer in range(depth):
        final = layer == depth - 1
        mod = _ada(cvecs, w_ada, b_ada, layer)
        mods = jnp.stack([jnp.broadcast_to(mod[B], (B, 6 * D)), mod[:B]], axis=1)[:, :, None, :]

        w_in_p = _reorder_columns(w_in[layer])
        lb = lb_all[layer]
        hg_par = jnp.stack([jnp.log(lb[0]) * LOG2E, jnp.log1p(-lb[0]) * LOG2E, 1.0 - lb[0],
                            jnp.log(lb[1]) * LOG2E, jnp.log1p(-lb[1]) * LOG2E, 1.0 - lb[1],
                            jnp.zeros_like(lb[0]), jnp.zeros_like(lb[0])], axis=0)
        ml_bias = jnp.zeros((1, 128), F32).at[0, :16].set(ml_gate_bias[layer].astype(F32).reshape(16))
        wa = jnp.zeros((128, 2 * GLA_W), F32)
        wa = wa.at[16:32, :GLA_W].set(gla_w_a[layer, 0].astype(F32))
        wa = wa.at[32:48, GLA_W:].set(gla_w_a[layer, 1].astype(F32)).astype(BF16)
        ba = jnp.zeros((8, 2 * GLA_W), F32).at[0].set(gla_b_a[layer].astype(F32).reshape(2 * GLA_W))
        ps, pk, pg, gate = _inp(xs, mods, w_in_p, cos_t, sin_t, hg_par, ml_bias, wa, ba, n_ctx_tiles)

        rt_par = jnp.zeros((8, GROUP_W), F32).at[:2].set(
            jnp.repeat(jax.nn.log_sigmoid(rt_decay_logit[layer].astype(F32)) * LOG2E, HEAD_DIM, axis=-1))
        yf, yb = mix(ps, pk, pg, rt_par)

        res = _out(xs, yf, yb, gate, mods, g_heads[layer], w_out[layer].astype(BF16),
                   w_ff1[layer].astype(BF16), w_ff2[layer].astype(BF16), g_final,
                   n_ctx_tiles, final)
        if final:
            out = res
        else:
            xs = (res,)
    return out
```
